```python
import math
import jax, jax.numpy as jnp
from jax import lax
import numpy as np

D_MODEL = 1024
BATCH = 8
SEQ = 4096
DEPTH = 1

RET_HEADS = 4
RET_DK = 128
RET_DV = 256
RET_CHUNK = 128
DIFF_HEADS = 4
DIFF_DK = 128
DIFF_DV = 256
Q_BLOCK = 128
ROPE_THETA = 10000.0
N_GROUPS = 4
EXPERTS_PER_GROUP = 8
N_EXPERTS = 32
TOP_K = 2
D_EXPERT = 512
MOE_BLOCK = 128
EPS = 1e-6

RET_QK = 512
RET_V = 1024
DIFF_QK = 1024
DIFF_V = 1024
IN_OFFSETS = (512, 1024, 2048, 3072, 4096, 5120, 6144, 7168)
D_IN = 8192

kernel_name = "hybrid_retention_diffattn_hmoe_block"


def rmsnorm(x, g):
    xf = x.astype(jnp.float32)
    y = xf * lax.rsqrt(jnp.mean(xf * xf, axis=-1, keepdims=True) + EPS)
    return (y * g.astype(jnp.float32)).astype(x.dtype)


def head_rms(o):
    of = o.astype(jnp.float32)
    return of * lax.rsqrt(jnp.mean(of * of, axis=-1, keepdims=True) + EPS)


def apply_rope(t, positions):
    d = t.shape[-1]
    inv = ROPE_THETA ** (-jnp.arange(0, d, 2, dtype=jnp.float32) / d)
    ang = positions.astype(jnp.float32)[:, :, None] * inv
    cos = jnp.cos(ang)[:, :, None, :]
    sin = jnp.sin(ang)[:, :, None, :]
    tf = t.astype(jnp.float32)
    t1, t2 = tf[..., : d // 2], tf[..., d // 2:]
    return jnp.concatenate([t1 * cos - t2 * sin, t2 * cos + t1 * sin], axis=-1).astype(t.dtype)


def retention(q, k, v):
    B, S, H, dk = q.shape
    dv = v.shape[-1]
    C = RET_CHUNK
    N = S // C
    gamma = 1.0 - jnp.exp2(-5.0 - jnp.arange(H, dtype=jnp.float32))
    log_g = jnp.log(gamma)
    idx = jnp.arange(C, dtype=jnp.float32)
    rel = idx[:, None] - idx[None, :]
    dmask = jnp.where(rel >= 0, jnp.exp(log_g[:, None, None] * jnp.maximum(rel, 0.0)), 0.0)
    zeta = jnp.exp(log_g[:, None] * (C - 1 - idx)).T
    xi = jnp.exp(log_g[:, None] * (idx + 1)).T
    chunk_decay = jnp.exp(log_g * C)
    qc = q.astype(jnp.float32).reshape(B, N, C, H, dk)
    kc = k.astype(jnp.float32).reshape(B, N, C, H, dk) * (dk ** -0.5)
    vc = v.astype(jnp.float32).reshape(B, N, C, H, dv)
    s = jnp.einsum('bnihd,bnjhd->bnhij', qc, kc) * dmask
    o_in = jnp.einsum('bnhij,bnjhe->bnihe', s, vc)
    kv = jnp.einsum('bnjhd,bnjhe->bnhde', kc * zeta[None, None, :, :, None], vc)

    def step(state, kv_n):
        return chunk_decay[None, :, None, None] * state + kv_n, state

    _, state_prev = lax.scan(step, jnp.zeros((B, H, dk, dv), jnp.float32), jnp.swapaxes(kv, 0, 1))
    o_cross = jnp.einsum('bnihd,nbhde->bnihe', qc * xi[None, None, :, :, None], state_prev)
    return (o_in + o_cross).reshape(B, S, H, dv)


def diff_attention(q, k, v, lam):
    B, S, H, _, dk = q.shape
    dv = v.shape[-1]
    nq = S // Q_BLOCK
    qb = jnp.moveaxis(q.reshape(B, nq, Q_BLOCK, H, 2, dk), 1, 0)
    kpos = jnp.arange(S)
    scale = dk ** -0.5
    neg = jnp.finfo(jnp.float32).min

    def block(args):
        i, qi = args
        s = jnp.einsum('bqhcd,bkhcd->bhcqk', qi, k).astype(jnp.float32) * scale
        qpos = i * Q_BLOCK + jnp.arange(Q_BLOCK)
        s = jnp.where(kpos[None, :] <= qpos[:, None], s, neg)
        p = jax.nn.softmax(s, axis=-1)
        a = p[:, :, 0] - lam * p[:, :, 1]
        return jnp.einsum('bhqk,bkhe->bqhe', a.astype(v.dtype), v)

    o = lax.map(block, (jnp.arange(nq), qb))
    return jnp.moveaxis(o, 0, 1).reshape(B, S, H, dv)


def hierarchical_moe(h, w_rg, b_rg, w_re, b_re, w1, w3, w2):
    T, D = h.shape
    gp = jax.nn.softmax(jnp.dot(h, w_rg).astype(jnp.float32) + b_rg, axis=-1)
    g_top, g_idx = lax.top_k(gp, 1)
    el_all = (jnp.dot(h, w_re).astype(jnp.float32) + b_re).reshape(T, N_GROUPS, EXPERTS_PER_GROUP)
    el = jnp.take_along_axis(el_all, g_idx[:, :, None], axis=1)[:, 0]
    e_val, e_idx = lax.top_k(el, TOP_K)
    weights = (g_top * jax.nn.softmax(e_val, axis=-1)).astype(h.dtype)
    expert = g_idx * EXPERTS_PER_GROUP + e_idx

    flat_e = expert.reshape(-1)
    A = flat_e.shape[0]
    order = jnp.argsort(flat_e)
    sorted_e = flat_e[order]
    counts = jnp.bincount(flat_e, length=N_EXPERTS)
    starts = jnp.cumsum(counts) - counts
    padded = ((counts + MOE_BLOCK - 1) // MOE_BLOCK) * MOE_BLOCK
    pad_ends = jnp.cumsum(padded)
    pad_starts = pad_ends - padded
    dest = pad_starts[sorted_e] + (jnp.arange(A) - starts[sorted_e])
    n_blocks = -(-A // MOE_BLOCK) + N_EXPERTS
    P = n_blocks * MOE_BLOCK
    row_token = jnp.full((P,), T, jnp.int32).at[dest].set((order // TOP_K).astype(jnp.int32))
    block_e = jnp.minimum(jnp.searchsorted(pad_ends, jnp.arange(n_blocks) * MOE_BLOCK, side='right'),
                          N_EXPERTS - 1)
    h_pad = jnp.concatenate([h, jnp.zeros((1, D), h.dtype)], axis=0)[row_token]
    h_pad = h_pad.reshape(n_blocks, MOE_BLOCK, D)

    def expert_block(args):
        xb, e = args
        return jnp.dot(jax.nn.silu(jnp.dot(xb, w1[e])) * jnp.dot(xb, w3[e]), w2[e])

    y_pad = lax.map(expert_block, (h_pad, block_e)).reshape(P, D)
    dest_of_flat = jnp.zeros((A,), jnp.int32).at[order].set(dest.astype(jnp.int32))
    y_assign = y_pad[dest_of_flat].reshape(T, TOP_K, D)
    return jnp.sum(y_assign * weights[:, :, None], axis=1)


def setup_inputs(seed: int = 0) -> dict:
    key = jax.random.key(seed)
    ks = jax.random.split(key, 24)
    f32 = jnp.float32
    D = D_MODEL

    def nrm(k, shape, scale):
        return jax.random.normal(k, shape, f32) * scale

    return {
        "x": nrm(ks[0], (BATCH, SEQ, D), 1.0),
        "c": nrm(ks[1], (BATCH, D), 1.0),
        "positions": jnp.broadcast_to(jnp.arange(SEQ, dtype=jnp.int32)[None, :], (BATCH, SEQ)),
        "w_ada": nrm(ks[2], (DEPTH, D, 6 * D), D ** -0.5),
        "b_ada": nrm(ks[3], (DEPTH, 6 * D), 0.02),
        "norm_mix": 1.0 + nrm(ks[4], (DEPTH, D), 0.02),
        "w_in": nrm(ks[5], (DEPTH, D, D_IN), D ** -0.5),
        "w_ret_o": nrm(ks[6], (DEPTH, RET_V, D), RET_V ** -0.5),
        "w_diff_o": nrm(ks[7], (DEPTH, DIFF_V, D), DIFF_V ** -0.5),
        "lam_q1": nrm(ks[8], (DEPTH, DIFF_DK), 0.1),
        "lam_k1": nrm(ks[9], (DEPTH, DIFF_DK), 0.1),
        "lam_q2": nrm(ks[10], (DEPTH, DIFF_DK), 0.1),
        "lam_k2": nrm(ks[11], (DEPTH, DIFF_DK), 0.1),
        "diff_norm": 1.0 + nrm(ks[12], (DEPTH, DIFF_DV), 0.02),
        "w_out": nrm(ks[13], (DEPTH, D, D), D ** -0.5),
        "norm_ffn": 1.0 + nrm(ks[14], (DEPTH, D), 0.02),
        "w_router_group": nrm(ks[15], (DEPTH, D, N_GROUPS), D ** -0.5),
        "b_router_group": nrm(ks[16], (DEPTH, N_GROUPS), 0.01),
        "w_router_expert": nrm(ks[17], (DEPTH, D, N_GROUPS * EXPERTS_PER_GROUP), D ** -0.5),
        "b_router_expert": nrm(ks[18], (DEPTH, N_GROUPS * EXPERTS_PER_GROUP), 0.01),
        "w_exp_gate": nrm(ks[19], (DEPTH, N_EXPERTS, D, D_EXPERT), D ** -0.5),
        "w_exp_up": nrm(ks[20], (DEPTH, N_EXPERTS, D, D_EXPERT), D ** -0.5),
        "w_exp_down": nrm(ks[21], (DEPTH, N_EXPERTS, D_EXPERT, D), D_EXPERT ** -0.5),
        "norm_final": 1.0 + nrm(ks[22], (D,), 0.02),
    }


def reference(x, c, positions, w_ada, b_ada, norm_mix, w_in, w_ret_o, w_diff_o,
              lam_q1, lam_k1, lam_q2, lam_k2, diff_norm, w_out, norm_ffn,
              w_router_group, b_router_group, w_router_expert, b_router_expert,
              w_exp_gate, w_exp_up, w_exp_down, norm_final):
    B, S, D = x.shape
    for l in range(DEPTH):
        lambda_init = 0.8 - 0.6 * math.exp(-0.3 * l)
        mod = (jnp.dot(jax.nn.silu(c), w_ada[l]) + b_ada[l])[:, None, :]
        sh1, sc1, g1, sh2, sc2, g2 = jnp.split(mod, 6, axis=-1)

        h = rmsnorm(x, norm_mix[l]) * (1.0 + sc1) + sh1
        proj = jnp.dot(h, w_in[l])
        rq, rk, rv, rg, dq, dk, dv, gate_r, gate_d = jnp.split(proj, list(IN_OFFSETS), axis=-1)

        rq = apply_rope(rq.reshape(B, S, RET_HEADS, RET_DK), positions)
        rk = apply_rope(rk.reshape(B, S, RET_HEADS, RET_DK), positions)
        ro = retention(rq, rk, rv.reshape(B, S, RET_HEADS, RET_DV))
        ro = head_rms(ro).reshape(B, S, RET_V).astype(x.dtype) * jax.nn.silu(rg)
        ret_out = jnp.dot(ro, w_ret_o[l])

        dq = apply_rope(dq.reshape(B, S, 2 * DIFF_HEADS, DIFF_DK), positions).reshape(B, S, DIFF_HEADS, 2, DIFF_DK)
        dk = apply_rope(dk.reshape(B, S, 2 * DIFF_HEADS, DIFF_DK), positions).reshape(B, S, DIFF_HEADS, 2, DIFF_DK)
        lam = (jnp.exp(jnp.sum(lam_q1[l].astype(jnp.float32) * lam_k1[l].astype(jnp.float32)))
               - jnp.exp(jnp.sum(lam_q2[l].astype(jnp.float32) * lam_k2[l].astype(jnp.float32)))
               + lambda_init)
        do = diff_attention(dq, dk, dv.reshape(B, S, DIFF_HEADS, DIFF_DV), lam)
        do = head_rms(do) * diff_norm[l].astype(jnp.float32) * (1.0 - lambda_init)
        diff_out = jnp.dot(do.reshape(B, S, DIFF_V).astype(x.dtype), w_diff_o[l])

        merged = jax.nn.sigmoid(gate_r) * ret_out + jax.nn.sigmoid(gate_d) * diff_out
        x = x + g1 * jnp.dot(merged, w_out[l])

        h2 = rmsnorm(x, norm_ffn[l]) * (1.0 + sc2) + sh2
        y = hierarchical_moe(h2.reshape(B * S, D), w_router_group[l], b_router_group[l],
                             w_router_expert[l], b_router_expert[l],
                             w_exp_gate[l], w_exp_up[l], w_exp_down[l]).reshape(B, S, D)
        x = x + g2 * y
    return rmsnorm(x, norm_final)
```

```python
import functools
import math

import jax
import jax.numpy as jnp
from jax import lax
from jax.experimental import pallas as pl
from jax.experimental.pallas import tpu as pltpu

F32 = jnp.float32
BF16 = jnp.bfloat16

EPS = 1e-6
ROPE_THETA = 10000.0
LANES = 128
RET_HEADS, RET_DK, RET_DV, RET_CHUNK = 4, 128, 256, 128
DIFF_HEADS, DIFF_DK, DIFF_DV = 4, 128, 256
N_GROUPS, EXPERTS_PER_GROUP, N_EXPERTS, D_EXPERT = 4, 8, 32, 512
OFF_RQ, OFF_RK, OFF_RV, OFF_RG, OFF_DQ, OFF_DK, OFF_DV, OFF_GR, OFF_GD, D_IN = (
    0, 512, 1024, 2048, 3072, 4096, 5120, 6144, 7168, 8192)
NEG_BIG = -1e30
VMEM_LIMIT_BYTES = 48 * 1024 * 1024

TM_INPROJ, TN_INPROJ = 1024, 1024
BQ_ATTN = 512
TM_POST = 512
TB_MOE = 256
TS_SCATTER = 256
TC_COMBINE = 256


def _params(*sem):
    return pltpu.CompilerParams(dimension_semantics=sem, vmem_limit_bytes=VMEM_LIMIT_BYTES)


def _sigmoid(v):
    return 1.0 / (1.0 + jnp.exp(-v))


def _rms(v):
    return v * lax.rsqrt(jnp.mean(v * v, axis=-1, keepdims=True) + EPS)


def _dot(a, b):
    return jnp.dot(a, b, preferred_element_type=F32)


def _dot_nt(a, b):
    return lax.dot_general(a, b, (((1,), (1,)), ((), ())), preferred_element_type=F32)


def _ada_kernel(c_ref, w_ref, b_ref, o_ref):
    c = c_ref[...]
    a = (c * _sigmoid(c)).astype(BF16)
    o_ref[...] = _dot(a, w_ref[...].astype(BF16)) + b_ref[...]


def _adaln(c, w, b):
    B, D = c.shape
    n = w.shape[1] // D
    return pl.pallas_call(
        _ada_kernel,
        grid=(n,),
        in_specs=[pl.BlockSpec((B, D), lambda j: (0, 0)),
                  pl.BlockSpec((D, D), lambda j: (0, j)),
                  pl.BlockSpec((1, D), lambda j: (0, j))],
        out_specs=pl.BlockSpec((B, D), lambda j: (0, j)),
        out_shape=jax.ShapeDtypeStruct((B, n * D), F32),
        compiler_params=_params("parallel"),
        name="adaln",
    )(c, w, b.reshape(1, -1))


def _inproj_kernel(x_ref, pos_ref, mod_ref, nw_ref, inv_ref, sgn_ref, cs_ref, w_ref, o_ref,
                   h_ref, cos_ref, sin_ref, *, tn):
    j = pl.program_id(1)

    @pl.when(j == 0)
    def _():
        mod = mod_ref[0]
        h = _rms(x_ref[...]) * nw_ref[...] * (1.0 + mod[1:2]) + mod[0:1]
        h_ref[...] = h.astype(BF16)
        ang = pos_ref[...].astype(F32) * inv_ref[...]
        cos_ref[...] = jnp.cos(ang)
        sin_ref[...] = jnp.sin(ang) * sgn_ref[...]

    acc = _dot(h_ref[...], w_ref[...])
    col = j * tn
    is_rope = (col < OFF_RV) | ((col >= OFF_DQ) & (col < OFF_DV))

    @pl.when(is_rope)
    def _():
        cos = cos_ref[...]
        sin = sin_ref[...]
        for k in range(tn // LANES):
            sl = slice(k * LANES, (k + 1) * LANES)
            a = acc[:, sl]
            r = a * cos + pltpu.roll(a, LANES // 2, 1) * sin
            o_ref[:, sl] = (r * cs_ref[:, sl]).astype(BF16)

    @pl.when(jnp.logical_not(is_rope))
    def _():
        o_ref[...] = acc.astype(BF16)


def _inproj(xf, pos, mod, norm_w, w_bf16, seq):
    T, D = xf.shape
    tm = min(TM_INPROJ, seq)
    tn = TN_INPROJ
    half = LANES // 2
    inv = ROPE_THETA ** (-jnp.arange(0, LANES, 2, dtype=F32) / LANES)
    inv = jnp.concatenate([inv, inv]).reshape(1, LANES)
    sgn = jnp.concatenate([-jnp.ones((half,), F32), jnp.ones((half,), F32)]).reshape(1, LANES)
    scale = RET_DK ** -0.5
    colscale = jnp.ones((D_IN,), F32)
    colscale = colscale.at[OFF_RK:OFF_RV].set(scale)
    colscale = colscale.at[OFF_DQ:OFF_DK].set(DIFF_DK ** -0.5)
    colscale = colscale.reshape(1, D_IN)
    per_batch = seq // tm
    return pl.pallas_call(
        functools.partial(_inproj_kernel, tn=tn),
        grid=(T // tm, D_IN // tn),
        in_specs=[pl.BlockSpec((tm, D), lambda i, j: (i, 0)),
                  pl.BlockSpec((tm, 1), lambda i, j: (i, 0)),
                  pl.BlockSpec((1, 6, D), lambda i, j: (i // per_batch, 0, 0)),
                  pl.BlockSpec((1, D), lambda i, j: (0, 0)),
                  pl.BlockSpec((1, LANES), lambda i, j: (0, 0)),
                  pl.BlockSpec((1, LANES), lambda i, j: (0, 0)),
                  pl.BlockSpec((1, tn), lambda i, j: (0, j)),
                  pl.BlockSpec((D, tn), lambda i, j: (0, j))],
        out_specs=pl.BlockSpec((tm, tn), lambda i, j: (i, j)),
        out_shape=jax.ShapeDtypeStruct((T, D_IN), BF16),
        scratch_shapes=[pltpu.VMEM((tm, D), BF16),
                        pltpu.VMEM((tm, LANES), F32),
                        pltpu.VMEM((tm, LANES), F32)],
        compiler_params=_params("parallel", "arbitrary"),
        name="inproj",
    )(xf, pos, mod, norm_w.reshape(1, D), inv, sgn, colscale, w_bf16)


def _ret_kernel(q_ref, k_ref, v_ref, g_ref, dmask_ref, xi_ref, zeta_ref, cd_ref, o_ref, state_ref):
    @pl.when(pl.program_id(1) == 0)
    def _():
        state_ref[...] = jnp.zeros(state_ref.shape, F32)

    for h in range(RET_HEADS):
        qk = slice(h * RET_DK, (h + 1) * RET_DK)
        vv = slice(h * RET_DV, (h + 1) * RET_DV)
        q = q_ref[:, qk]
        k = k_ref[:, qk]
        v = v_ref[:, vv]
        st = state_ref[h]
        s = _dot_nt(q, k) * dmask_ref[h]
        qx = (q.astype(F32) * xi_ref[h]).astype(BF16)
        o = _dot(s.astype(BF16), v) + _dot(qx, st.astype(BF16))
        kz_t = (k.astype(F32) * zeta_ref[h]).T.astype(BF16)
        state_ref[h] = cd_ref[h] * st + _dot(kz_t, v)
        g = g_ref[:, vv].astype(F32)
        o_ref[:, vv] = (_rms(o) * (g * _sigmoid(g))).astype(BF16)


def _retention(proj, batch, seq):
    T = proj.shape[0]
    C = RET_CHUNK
    H = RET_HEADS
    nc = seq // C
    gamma = 1.0 - jnp.exp2(-5.0 - jnp.arange(H, dtype=F32))
    log_g = jnp.log(gamma)
    idx = jnp.arange(C, dtype=F32)
    rel = idx[:, None] - idx[None, :]
    dmask = jnp.where(rel >= 0, jnp.exp(log_g[:, None, None] * jnp.maximum(rel, 0.0)), 0.0)
    zeta = jnp.exp(log_g[:, None] * (C - 1 - idx))
    xi = jnp.exp(log_g[:, None] * (idx + 1))
    cd = jnp.exp(log_g * C)
    zeta_b = jnp.broadcast_to(zeta[:, :, None], (H, C, RET_DK))
    xi_b = jnp.broadcast_to(xi[:, :, None], (H, C, RET_DK))
    cd_b = jnp.broadcast_to(cd[:, None, None], (H, 1, RET_DV))
    wq = H * RET_DK
    wv = H * RET_DV
    row = lambda b, n: b * nc + n
    const3 = lambda b, n: (0, 0, 0)
    return pl.pallas_call(
        _ret_kernel,
        grid=(batch, nc),
        in_specs=[pl.BlockSpec((C, wq), lambda b, n: (row(b, n), OFF_RQ // wq)),
                  pl.BlockSpec((C, wq), lambda b, n: (row(b, n), OFF_RK // wq)),
                  pl.BlockSpec((C, wv), lambda b, n: (row(b, n), OFF_RV // wv)),
                  pl.BlockSpec((C, wv), lambda b, n: (row(b, n), OFF_RG // wv)),
                  pl.BlockSpec((H, C, C), const3),
                  pl.BlockSpec((H, C, RET_DK), const3),
                  pl.BlockSpec((H, C, RET_DK), const3),
                  pl.BlockSpec((H, 1, RET_DV), const3)],
        out_specs=pl.BlockSpec((C, wv), lambda b, n: (row(b, n), 0)),
        out_shape=jax.ShapeDtypeStruct((T, wv), BF16),
        scratch_shapes=[pltpu.VMEM((H, RET_DK, RET_DV), F32)],
        compiler_params=_params("parallel", "arbitrary"),
        name="retention",
    )(proj, proj, proj, proj, dmask, xi_b, zeta_b, cd_b)


def _attn_kernel(lam_ref, dn_ref, q1_ref, q2_ref, k1_ref, k2_ref, v_ref, o_ref,
                 m_ref, l_ref, acc_ref, *, bq, lambda_init):
    i = pl.program_id(2)
    m_ref[...] = jnp.full(m_ref.shape, NEG_BIG, F32)
    l_ref[...] = jnp.zeros(l_ref.shape, F32)
    acc_ref[...] = jnp.zeros(acc_ref.shape, F32)
    qs = (q1_ref[...], q2_ref[...])
    k_refs = (k1_ref, k2_ref)

    def block(j, diagonal):
        start = pl.multiple_of(j * bq, bq)
        v = v_ref[pl.ds(start, bq), :]
        for c in range(2):
            k = k_refs[c][pl.ds(start, bq), :]
            s = _dot_nt(qs[c], k)
            if diagonal:
                row = lax.broadcasted_iota(jnp.int32, (bq, bq), 0)
                col = lax.broadcasted_iota(jnp.int32, (bq, bq), 1)
                s = jnp.where(col <= row, s, NEG_BIG)
            m_prev = m_ref[c]
            m_new = jnp.maximum(m_prev, jnp.max(s, axis=1, keepdims=True))
            p = jnp.exp(s - m_new)
            alpha = jnp.exp(m_prev - m_new)
            l_ref[c] = alpha * l_ref[c] + jnp.sum(p, axis=1, keepdims=True)
            acc_ref[c] = alpha * acc_ref[c] + _dot(p.astype(BF16), v)
            m_ref[c] = m_new

    def body(j, carry):
        block(j, False)
        return carry

    lax.fori_loop(0, i, body, 0)
    block(i, True)

    lam4 = lam_ref[...]
    lam = (jnp.exp(jnp.sum(lam4[0:1] * lam4[1:2], axis=1, keepdims=True))
           - jnp.exp(jnp.sum(lam4[2:3] * lam4[3:4], axis=1, keepdims=True)) + lambda_init)
    o = acc_ref[0] / l_ref[0] - lam * (acc_ref[1] / l_ref[1])
    o_ref[...] = (_rms(o) * dn_ref[...] * (1.0 - lambda_init)).astype(BF16)


def _diff_attention(proj, lam4, diff_norm, lambda_init, batch, seq):
    T = proj.shape[0]
    H = DIFF_HEADS
    bq = min(BQ_ATTN, seq)
    nq = seq // bq
    q0 = OFF_DQ // DIFF_DK
    k0 = OFF_DK // DIFF_DK
    v0 = OFF_DV // DIFF_DV
    return pl.pallas_call(
        functools.partial(_attn_kernel, bq=bq, lambda_init=lambda_init),
        grid=(batch, H, nq),
        in_specs=[pl.BlockSpec((4, DIFF_DK), lambda b, h, i: (0, 0)),
                  pl.BlockSpec((1, DIFF_DV), lambda b, h, i: (0, 0)),
                  pl.BlockSpec((bq, DIFF_DK), lambda b, h, i: (b * nq + i, q0 + 2 * h)),
                  pl.BlockSpec((bq, DIFF_DK), lambda b, h, i: (b * nq + i, q0 + 2 * h + 1)),
                  pl.BlockSpec((seq, DIFF_DK), lambda b, h, i: (b, k0 + 2 * h)),
                  pl.BlockSpec((seq, DIFF_DK), lambda b, h, i: (b, k0 + 2 * h + 1)),
                  pl.BlockSpec((seq, DIFF_DV), lambda b, h, i: (b, v0 + h))],
        out_specs=pl.BlockSpec((bq, DIFF_DV), lambda b, h, i: (b * nq + i, h)),
        out_shape=jax.ShapeDtypeStruct((T, H * DIFF_DV), BF16),
        scratch_shapes=[pltpu.VMEM((2, bq, 1), F32),
                        pltpu.VMEM((2, bq, 1), F32),
                        pltpu.VMEM((2, bq, DIFF_DV), F32)],
        compiler_params=_params("parallel", "parallel", "arbitrary"),
        name="diff_attn",
    )(lam4, diff_norm.reshape(1, DIFF_DV), proj, proj, proj, proj, proj)


def _post_kernel(ro_ref, do_ref, gr_ref, gd_ref, x_ref, mod_ref, wr_ref, wd_ref, wo_ref, nf_ref,
                 wrt_ref, brt_ref, x1_ref, h2_ref, route_ref, cnt_ref, carry_ref, *, tm):
    @pl.when(pl.program_id(0) == 0)
    def _():
        carry_ref[...] = jnp.zeros(carry_ref.shape, F32)

    mod = mod_ref[0]
    ret_out = _dot(ro_ref[...], wr_ref[...])
    diff_out = _dot(do_ref[...], wd_ref[...])
    merged = (_sigmoid(gr_ref[...].astype(F32)) * ret_out
              + _sigmoid(gd_ref[...].astype(F32)) * diff_out)
    x1 = x_ref[...] + mod[2:3] * _dot(merged.astype(BF16), wo_ref[...])
    x1_ref[...] = x1
    h2 = _rms(x1) * nf_ref[...] * (1.0 + mod[4:5]) + mod[3:4]
    h2_ref[...] = h2

    lg = _dot(h2.astype(BF16), wrt_ref[...]) + brt_ref[...]
    lane = lax.broadcasted_iota(jnp.int32, lg.shape, 1)
    far = jnp.int32(LANES)

    def top1(vals):
        best = jnp.max(vals, axis=1, keepdims=True)
        return best, jnp.min(jnp.where(vals == best, lane, far), axis=1, keepdims=True)

    is_group = (lane >= N_EXPERTS) & (lane < N_EXPERTS + N_GROUPS)
    g_best, g_lane = top1(jnp.where(is_group, lg, NEG_BIG))
    g_sum = jnp.sum(jnp.where(is_group, jnp.exp(lg - g_best), 0.0), axis=1, keepdims=True)
    g_top = 1.0 / g_sum
    first = (g_lane - N_EXPERTS) * EXPERTS_PER_GROUP
    in_group = (lane >= first) & (lane < first + EXPERTS_PER_GROUP)
    el = jnp.where(in_group, lg, NEG_BIG)
    e_a, i_a = top1(el)
    e_b, i_b = top1(jnp.where(lane == i_a, NEG_BIG, el))
    t = jnp.exp(e_b - e_a)
    w_a = g_top / (1.0 + t)
    w_b = g_top * t / (1.0 + t)

    hot_a = lane == i_a
    hot_b = lane == i_b
    hot = jnp.where(hot_a | hot_b, 1.0, 0.0)
    r_i = lax.broadcasted_iota(jnp.int32, (tm, tm), 0)
    c_i = lax.broadcasted_iota(jnp.int32, (tm, tm), 1)
    lower = jnp.where(c_i < r_i, 1.0, 0.0).astype(BF16)
    before = _dot(lower, hot.astype(BF16)) + carry_ref[...]
    rank_a = jnp.sum(jnp.where(hot_a, before, 0.0), axis=1, keepdims=True)
    rank_b = jnp.sum(jnp.where(hot_b, before, 0.0), axis=1, keepdims=True)
    total = carry_ref[...] + jnp.sum(hot, axis=0, keepdims=True)
    carry_ref[...] = total
    cnt_ref[...] = total

    fields = (i_a.astype(F32), i_b.astype(F32), w_a, w_b, rank_a, rank_b)
    route = jnp.zeros(lg.shape, F32)
    for n, f in enumerate(fields):
        route = jnp.where(lane == n, f, route)
    route_ref[...] = route


def _post_mixer(ro, do, proj, xf, mod, w_ret_o, w_diff_o, w_out, norm_ffn, w_rt, b_rt, seq):
    T, D = xf.shape
    tm = min(TM_POST, seq)
    per_batch = seq // tm
    row = lambda i: (i, 0)
    const = lambda i: (0, 0)
    return pl.pallas_call(
        functools.partial(_post_kernel, tm=tm),
        grid=(T // tm,),
        in_specs=[pl.BlockSpec((tm, D), row),
                  pl.BlockSpec((tm, D), row),
                  pl.BlockSpec((tm, D), lambda i: (i, OFF_GR // D)),
                  pl.BlockSpec((tm, D), lambda i: (i, OFF_GD // D)),
                  pl.BlockSpec((tm, D), row),
                  pl.BlockSpec((1, 6, D), lambda i: (i // per_batch, 0, 0)),
                  pl.BlockSpec((D, D), const),
                  pl.BlockSpec((D, D), const),
                  pl.BlockSpec((D, D), const),
                  pl.BlockSpec((1, D), const),
                  pl.BlockSpec((D, LANES), const),
                  pl.BlockSpec((1, LANES), const)],
        out_specs=[pl.BlockSpec((tm, D), row),
                   pl.BlockSpec((tm, D), row),
                   pl.BlockSpec((tm, LANES), row),
                   pl.BlockSpec((1, LANES), const)],
        out_shape=[jax.ShapeDtypeStruct((T, D), F32),
                   jax.ShapeDtypeStruct((T, D), F32),
                   jax.ShapeDtypeStruct((T, LANES), F32),
                   jax.ShapeDtypeStruct((1, LANES), F32)],
        scratch_shapes=[pltpu.VMEM((1, LANES), F32)],
        compiler_params=_params("arbitrary"),
        name="post_mixer",
    )(ro, do, proj, proj, xf, mod, w_ret_o, w_diff_o, w_out, norm_ffn.reshape(1, D), w_rt, b_rt)


def _row_copy(src, s, dst, d, sem):
    return pltpu.make_async_copy(src.at[pl.ds(s, 1), :], dst.at[pl.ds(d, 1), :], sem)


def _scatter_kernel(zb_ref, dest_ref, h_ref, o_hbm, zero_ref, sem, *, ts, tb):
    @pl.when(pl.program_id(0) == 0)
    def _():
        zero_ref[...] = jnp.zeros(zero_ref.shape, F32)

        def zero_copy(n):
            start = pl.multiple_of(zb_ref[n] * tb, tb)
            return pltpu.make_async_copy(zero_ref, o_hbm.at[pl.ds(start, tb), :], sem)

        for n in range(zb_ref.shape[0]):
            pl.when(zb_ref[n] >= 0)(lambda n=n: zero_copy(n).start())
        for n in range(zb_ref.shape[0]):
            pl.when(zb_ref[n] >= 0)(lambda n=n: zero_copy(n).wait())

    def issue(r, carry):
        for k in range(2):
            _row_copy(h_ref, r, o_hbm, dest_ref[0, 0, 2 * r + k], sem).start()
        return carry

    def wait(r, carry):
        for k in range(2):
            _row_copy(h_ref, r, o_hbm, dest_ref[0, 0, 2 * r + k], sem).wait()
        return carry

    lax.fori_loop(0, ts, issue, 0)
    lax.fori_loop(0, ts, wait, 0)


def _moe_scatter(h2, dest, zero_blocks, n_rows):
    T, D = h2.shape
    ts = min(TS_SCATTER, T)
    tb = TB_MOE
    grid_spec = pltpu.PrefetchScalarGridSpec(
        num_scalar_prefetch=1,
        grid=(T // ts,),
        in_specs=[pl.BlockSpec((1, 1, 2 * ts), lambda i, zb: (i, 0, 0), memory_space=pltpu.SMEM),
                  pl.BlockSpec((ts, D), lambda i, zb: (i, 0))],
        out_specs=pl.BlockSpec(memory_space=pl.ANY),
        scratch_shapes=[pltpu.VMEM((tb, D), F32), pltpu.SemaphoreType.DMA(())],
    )
    return pl.pallas_call(
        functools.partial(_scatter_kernel, ts=ts, tb=tb),
        grid_spec=grid_spec,
        out_shape=jax.ShapeDtypeStruct((n_rows, D), F32),
        compiler_params=_params("arbitrary"),
        name="moe_scatter",
    )(zero_blocks, dest.reshape(T // ts, 1, 2 * ts), h2)


def _expert_kernel(be_ref, nv_ref, x_ref, w1_ref, w3_ref, w2_ref, o_ref):
    n = pl.program_id(0)
    nv = nv_ref[n]

    @pl.when(nv > 0)
    def _():
        x = x_ref[...].astype(BF16)
        g = _dot(x, w1_ref[0])
        u = _dot(x, w3_ref[0])
        a = (g * _sigmoid(g) * u).astype(BF16)
        o_ref[...] = _dot(a, w2_ref[0])

    @pl.when(nv == 0)
    def _():
        o_ref[...] = jnp.zeros(o_ref.shape, F32)


def _moe_experts(h_pad, block_e, block_nv, w1, w3, w2):
    P, D = h_pad.shape
    tb = TB_MOE
    wmap = lambda n, be, nv: (be[n], 0, 0)
    grid_spec = pltpu.PrefetchScalarGridSpec(
        num_scalar_prefetch=2,
        grid=(P // tb,),
        in_specs=[pl.BlockSpec((tb, D), lambda n, be, nv: (n, 0)),
                  pl.BlockSpec((1, D, D_EXPERT), wmap),
                  pl.BlockSpec((1, D, D_EXPERT), wmap),
                  pl.BlockSpec((1, D_EXPERT, D), wmap)],
        out_specs=pl.BlockSpec((tb, D), lambda n, be, nv: (n, 0)),
    )
    return pl.pallas_call(
        _expert_kernel,
        grid_spec=grid_spec,
        out_shape=jax.ShapeDtypeStruct((P, D), F32),
        compiler_params=_params("arbitrary"),
        name="moe_experts",
    )(block_e, block_nv, h_pad, w1, w3, w2)


def _combine_kernel(dest_ref, x1_ref, route_ref, mod_ref, nw_ref, y_hbm, o_ref, buf, sem, *, tc, final):
    def issue(r, carry):
        for k in range(2):
            _row_copy(y_hbm, dest_ref[0, 0, 2 * r + k], buf.at[k], r, sem).start()
        return carry

    def wait(r, carry):
        for k in range(2):
            _row_copy(y_hbm, dest_ref[0, 0, 2 * r + k], buf.at[k], r, sem).wait()
        return carry

    lax.fori_loop(0, tc, issue, 0)
    lax.fori_loop(0, tc, wait, 0)
    route = route_ref[...]
    y = route[:, 2:3] * buf[0] + route[:, 3:4] * buf[1]
    x2 = x1_ref[...] + mod_ref[0][5:6] * y
    o_ref[...] = _rms(x2) * nw_ref[...] if final else x2


def _moe_combine(y_pad, dest, x1, route, mod, norm_w, seq, final):
    T, D = x1.shape
    tc = min(TC_COMBINE, seq)
    per_batch = seq // tc
    row = lambda i: (i, 0)
    return pl.pallas_call(
        functools.partial(_combine_kernel, tc=tc, final=final),
        grid=(T // tc,),
        in_specs=[pl.BlockSpec((1, 1, 2 * tc), lambda i: (i, 0, 0), memory_space=pltpu.SMEM),
                  pl.BlockSpec((tc, D), row),
                  pl.BlockSpec((tc, LANES), row),
                  pl.BlockSpec((1, 6, D), lambda i: (i // per_batch, 0, 0)),
                  pl.BlockSpec((1, D), lambda i: (0, 0)),
                  pl.BlockSpec(memory_space=pl.ANY)],
        out_specs=pl.BlockSpec((tc, D), row),
        out_shape=jax.ShapeDtypeStruct((T, D), F32),
        scratch_shapes=[pltpu.VMEM((2, tc, D), F32), pltpu.SemaphoreType.DMA(())],
        compiler_params=_params("arbitrary"),
        name="moe_combine",
    )(dest.reshape(T // tc, 1, 2 * tc), x1, route, mod, norm_w.reshape(1, D), y_pad)


def _routing_tables(route, counts, n_blocks):
    tb = TB_MOE
    expert = route[:, 0:2].astype(jnp.int32)
    rank = route[:, 4:6].astype(jnp.int32)
    cnt = counts[0, :N_EXPERTS].astype(jnp.int32)
    nblk = (cnt + tb - 1) // tb
    blk_end = jnp.cumsum(nblk)
    blk_start = blk_end - nblk
    hot = expert[:, :, None] == jnp.arange(N_EXPERTS, dtype=jnp.int32)
    dest = jnp.sum(jnp.where(hot, blk_start * tb, 0), axis=-1) + rank
    blocks = jnp.arange(n_blocks, dtype=jnp.int32)
    block_e = jnp.minimum(jnp.searchsorted(blk_end, blocks, side="right"), N_EXPERTS - 1).astype(jnp.int32)
    left = cnt[block_e] - (blocks - blk_start[block_e]) * tb
    block_nv = jnp.where(blocks < blk_end[-1], jnp.clip(left, 0, tb), 0).astype(jnp.int32)
    tail = blk_end[-1] + jnp.arange(N_EXPERTS, dtype=jnp.int32)
    zero_blocks = jnp.concatenate([jnp.where(nblk > 0, blk_end - 1, -1),
                                   jnp.where(tail < n_blocks, tail, -1)]).astype(jnp.int32)
    return dest.astype(jnp.int32), block_e, block_nv, zero_blocks


def kernel(x, c, positions, w_ada, b_ada, norm_mix, w_in, w_ret_o, w_diff_o, lam_q1, lam_k1, lam_q2, lam_k2,
           diff_norm, w_out, norm_ffn, w_router_group, b_router_group, w_router_expert, b_router_expert,
           w_exp_gate, w_exp_up, w_exp_down, norm_final):
    B, S, D = x.shape
    T = B * S
    depth = w_ada.shape[0]
    assert D_IN == w_in.shape[2] and S % RET_CHUNK == 0
    xf = x.reshape(T, D)
    pos = positions.reshape(T, 1)
    n_blocks = (2 * T) // TB_MOE + N_EXPERTS
    for l in range(depth):
        lambda_init = 0.8 - 0.6 * math.exp(-0.3 * l)
        mod = _adaln(c, w_ada[l], b_ada[l]).reshape(B, 6, D)
        proj = _inproj(xf, pos, mod, norm_mix[l], w_in[l].astype(BF16), S)
        ro = _retention(proj, B, S)
        lam4 = jnp.stack([lam_q1[l], lam_k1[l], lam_q2[l], lam_k2[l]]).astype(F32)
        do = _diff_attention(proj, lam4, diff_norm[l].astype(F32), lambda_init, B, S)

        pad = LANES - N_EXPERTS - N_GROUPS
        w_rt = jnp.concatenate([w_router_expert[l], w_router_group[l], jnp.zeros((D, pad), F32)], axis=1)
        b_rt = jnp.concatenate([b_router_expert[l], b_router_group[l], jnp.zeros((pad,), F32)]).reshape(1, LANES)
        x1, h2, route, counts = _post_mixer(
            ro, do, proj, xf, mod, w_ret_o[l].astype(BF16), w_diff_o[l].astype(BF16), w_out[l].astype(BF16),
            norm_ffn[l], w_rt.astype(BF16), b_rt, S)

        dest, block_e, block_nv, zero_blocks = _routing_tables(route, counts, n_blocks)
        h_pad = _moe_scatter(h2, dest.reshape(-1), zero_blocks, n_blocks * TB_MOE)
        y_pad = _moe_experts(h_pad, block_e, block_nv, w_exp_gate[l].astype(BF16), w_exp_up[l].astype(BF16),
                             w_exp_down[l].astype(BF16))
        last = l == depth - 1
        xf = _moe_combine(y_pad, dest.reshape(-1), x1, route, mod, norm_final if last else norm_final, S, last)
    return xf.reshape(B, S, D)
```

```python
import functools
import math

import jax
import jax.numpy as jnp
from jax import lax
from jax.experimental import pallas as pl
from jax.experimental.pallas import tpu as pltpu

F32 = jnp.float32
BF16 = jnp.bfloat16

EPS = 1e-6
ROPE_THETA = 10000.0
LANES = 128
RET_HEADS, RET_DK, RET_DV, RET_CHUNK = 4, 128, 256, 128
DIFF_HEADS, DIFF_DK, DIFF_DV = 4, 128, 256
N_GROUPS, EXPERTS_PER_GROUP, N_EXPERTS, D_EXPERT = 4, 8, 32, 512
OFF_RQ, OFF_RK, OFF_RV, OFF_RG, OFF_DQ, OFF_DK, OFF_DV, OFF_GR, OFF_GD, D_IN = (
    0, 512, 1024, 2048, 3072, 4096, 5120, 6144, 7168, 8192)
NEG_BIG = -1e30
VMEM_LIMIT_BYTES = 48 * 1024 * 1024

TM_INPROJ, TN_INPROJ = 1024, 1024
BQ_ATTN = 512
TM_POST = 512
TB_MOE = 256
TS_SCATTER = 256
TC_COMBINE = 256


def _params(*sem):
    return pltpu.CompilerParams(dimension_semantics=sem, vmem_limit_bytes=VMEM_LIMIT_BYTES)


def _sigmoid(v):
    return 1.0 / (1.0 + jnp.exp(-v))


def _rms(v):
    return v * lax.rsqrt(jnp.mean(v * v, axis=-1, keepdims=True) + EPS)


def _dot(a, b):
    return jnp.dot(a, b, preferred_element_type=F32)


def _dot_nt(a, b):
    return lax.dot_general(a, b, (((1,), (1,)), ((), ())), preferred_element_type=F32)


def _ada_kernel(c_ref, w_ref, b_ref, o_ref):
    c = c_ref[...]
    a = (c * _sigmoid(c)).astype(BF16)
    o_ref[...] = _dot(a, w_ref[...].astype(BF16)) + b_ref[...]


def _adaln(c, w, b):
    B, D = c.shape
    n = w.shape[1] // D
    return pl.pallas_call(
        _ada_kernel,
        grid=(n,),
        in_specs=[pl.BlockSpec((B, D), lambda j: (0, 0)),
                  pl.BlockSpec((D, D), lambda j: (0, j)),
                  pl.BlockSpec((1, D), lambda j: (0, j))],
        out_specs=pl.BlockSpec((B, D), lambda j: (0, j)),
        out_shape=jax.ShapeDtypeStruct((B, n * D), F32),
        compiler_params=_params("parallel"),
        name="adaln",
    )(c, w, b.reshape(1, -1))


def _inproj_kernel(x_ref, pos_ref, mod_ref, nw_ref, inv_ref, sgn_ref, cs_ref, w_ref, o_ref,
                   h_ref, cos_ref, sin_ref, *, tn):
    j = pl.program_id(1)

    @pl.when(j == 0)
    def _():
        mod = mod_ref[0]
        h = _rms(x_ref[...]) * nw_ref[...] * (1.0 + mod[1:2]) + mod[0:1]
        h_ref[...] = h.astype(BF16)
        ang = pos_ref[...].astype(F32) * inv_ref[...]
        cos_ref[...] = jnp.cos(ang)
        sin_ref[...] = jnp.sin(ang) * sgn_ref[...]

    acc = _dot(h_ref[...], w_ref[...])
    col = j * tn
    is_rope = (col < OFF_RV) | ((col >= OFF_DQ) & (col < OFF_DV))

    @pl.when(is_rope)
    def _():
        cos = cos_ref[...]
        sin = sin_ref[...]
        for k in range(tn // LANES):
            sl = slice(k * LANES, (k + 1) * LANES)
            a = acc[:, sl]
            r = a * cos + pltpu.roll(a, LANES // 2, 1) * sin
            o_ref[:, sl] = (r * cs_ref[:, sl]).astype(BF16)

    @pl.when(jnp.logical_not(is_rope))
    def _():
        o_ref[...] = acc.astype(BF16)


def _inproj(xf, pos, mod, norm_w, w_bf16, seq):
    T, D = xf.shape
    tm = min(TM_INPROJ, seq)
    tn = TN_INPROJ
    half = LANES // 2
    inv = ROPE_THETA ** (-jnp.arange(0, LANES, 2, dtype=F32) / LANES)
    inv = jnp.concatenate([inv, inv]).reshape(1, LANES)
    sgn = jnp.concatenate([-jnp.ones((half,), F32), jnp.ones((half,), F32)]).reshape(1, LANES)
    scale = RET_DK ** -0.5
    colscale = jnp.ones((D_IN,), F32)
    colscale = colscale.at[OFF_RK:OFF_RV].set(scale)
    colscale = colscale.at[OFF_DQ:OFF_DK].set(DIFF_DK ** -0.5)
    colscale = colscale.reshape(1, D_IN)
    per_batch = seq // tm
    return pl.pallas_call(
        functools.partial(_inproj_kernel, tn=tn),
        grid=(T // tm, D_IN // tn),
        in_specs=[pl.BlockSpec((tm, D), lambda i, j: (i, 0)),
                  pl.BlockSpec((tm, 1), lambda i, j: (i, 0)),
                  pl.BlockSpec((1, 6, D), lambda i, j: (i // per_batch, 0, 0)),
                  pl.BlockSpec((1, D), lambda i, j: (0, 0)),
                  pl.BlockSpec((1, LANES), lambda i, j: (0, 0)),
                  pl.BlockSpec((1, LANES), lambda i, j: (0, 0)),
                  pl.BlockSpec((1, tn), lambda i, j: (0, j)),
                  pl.BlockSpec((D, tn), lambda i, j: (0, j))],
        out_specs=pl.BlockSpec((tm, tn), lambda i, j: (i, j)),
        out_shape=jax.ShapeDtypeStruct((T, D_IN), BF16),
        scratch_shapes=[pltpu.VMEM((tm, D), BF16),
                        pltpu.VMEM((tm, LANES), F32),
                        pltpu.VMEM((tm, LANES), F32)],
        compiler_params=_params("parallel", "arbitrary"),
        name="inproj",
    )(xf, pos, mod, norm_w.reshape(1, D), inv, sgn, colscale, w_bf16)


def _ret_kernel(q_ref, k_ref, v_ref, g_ref, dmask_ref, xi_ref, zeta_ref, cd_ref, o_ref, state_ref):
    @pl.when(pl.program_id(1) == 0)
    def _():
        state_ref[...] = jnp.zeros(state_ref.shape, F32)

    for h in range(RET_HEADS):
        qk = slice(h * RET_DK, (h + 1) * RET_DK)
        vv = slice(h * RET_DV, (h + 1) * RET_DV)
        q = q_ref[:, qk]
        k = k_ref[:, qk]
        v = v_ref[:, vv]
        st = state_ref[h]
        s = _dot_nt(q, k) * dmask_ref[h]
        qx = (q.astype(F32) * xi_ref[h]).astype(BF16)
        o = _dot(s.astype(BF16), v) + _dot(qx, st.astype(BF16))
        kz_t = (k.astype(F32) * zeta_ref[h]).T.astype(BF16)
        state_ref[h] = cd_ref[h] * st + _dot(kz_t, v)
        g = g_ref[:, vv].astype(F32)
        o_ref[:, vv] = (_rms(o) * (g * _sigmoid(g))).astype(BF16)


def _retention(proj, batch, seq):
    T = proj.shape[0]
    C = RET_CHUNK
    H = RET_HEADS
    nc = seq // C
    gamma = 1.0 - jnp.exp2(-5.0 - jnp.arange(H, dtype=F32))
    log_g = jnp.log(gamma)
    idx = jnp.arange(C, dtype=F32)
    rel = idx[:, None] - idx[None, :]
    dmask = jnp.where(rel >= 0, jnp.exp(log_g[:, None, None] * jnp.maximum(rel, 0.0)), 0.0)
    zeta = jnp.exp(log_g[:, None] * (C - 1 - idx))
    xi = jnp.exp(log_g[:, None] * (idx + 1))
    cd = jnp.exp(log_g * C)
    zeta_b = jnp.broadcast_to(zeta[:, :, None], (H, C, RET_DK))
    xi_b = jnp.broadcast_to(xi[:, :, None], (H, C, RET_DK))
    cd_b = jnp.broadcast_to(cd[:, None, None], (H, 1, RET_DV))
    wq = H * RET_DK
    wv = H * RET_DV
    row = lambda b, n: b * nc + n
    const3 = lambda b, n: (0, 0, 0)
    return pl.pallas_call(
        _ret_kernel,
        grid=(batch, nc),
        in_specs=[pl.BlockSpec((C, wq), lambda b, n: (row(b, n), OFF_RQ // wq)),
                  pl.BlockSpec((C, wq), lambda b, n: (row(b, n), OFF_RK // wq)),
                  pl.BlockSpec((C, wv), lambda b, n: (row(b, n), OFF_RV // wv)),
                  pl.BlockSpec((C, wv), lambda b, n: (row(b, n), OFF_RG // wv)),
                  pl.BlockSpec((H, C, C), const3),
                  pl.BlockSpec((H, C, RET_DK), const3),
                  pl.BlockSpec((H, C, RET_DK), const3),
                  pl.BlockSpec((H, 1, RET_DV), const3)],
        out_specs=pl.BlockSpec((C, wv), lambda b, n: (row(b, n), 0)),
        out_shape=jax.ShapeDtypeStruct((T, wv), BF16),
        scratch_shapes=[pltpu.VMEM((H, RET_DK, RET_DV), F32)],
        compiler_params=_params("parallel", "arbitrary"),
        name="retention",
    )(proj, proj, proj, proj, dmask, xi_b, zeta_b, cd_b)


def _attn_kernel(lam_ref, dn_ref, q1_ref, q2_ref, k1_ref, k2_ref, v_ref, o_ref,
                 m_ref, l_ref, acc_ref, *, bq, lambda_init):
    i = pl.program_id(2)
    m_ref[...] = jnp.full(m_ref.shape, NEG_BIG, F32)
    l_ref[...] = jnp.zeros(l_ref.shape, F32)
    acc_ref[...] = jnp.zeros(acc_ref.shape, F32)
    qs = (q1_ref[...], q2_ref[...])
    k_refs = (k1_ref, k2_ref)

    def block(j, diagonal):
        start = pl.multiple_of(j * bq, bq)
        v = v_ref[pl.ds(start, bq), :]
        for c in range(2):
            k = k_refs[c][pl.ds(start, bq), :]
            s = _dot_nt(qs[c], k)
            if diagonal:
                row = lax.broadcasted_iota(jnp.int32, (bq, bq), 0)
                col = lax.broadcasted_iota(jnp.int32, (bq, bq), 1)
                s = jnp.where(col <= row, s, NEG_BIG)
            m_prev = m_ref[c]
            m_new = jnp.maximum(m_prev, jnp.max(s, axis=1, keepdims=True))
            p = jnp.exp(s - m_new)
            alpha = jnp.exp(m_prev - m_new)
            l_ref[c] = alpha * l_ref[c] + jnp.sum(p, axis=1, keepdims=True)
            acc_ref[c] = alpha * acc_ref[c] + _dot(p.astype(BF16), v)
            m_ref[c] = m_new

    def body(j, carry):
        block(j, False)
        return carry

    lax.fori_loop(0, i, body, 0)
    block(i, True)

    lam4 = lam_ref[...]
    lam = (jnp.exp(jnp.sum(lam4[0:1] * lam4[1:2], axis=1, keepdims=True))
           - jnp.exp(jnp.sum(lam4[2:3] * lam4[3:4], axis=1, keepdims=True)) + lambda_init)
    o = acc_ref[0] / l_ref[0] - lam * (acc_ref[1] / l_ref[1])
    o_ref[...] = (_rms(o) * dn_ref[...] * (1.0 - lambda_init)).astype(BF16)


def _diff_attention(proj, lam4, diff_norm, lambda_init, batch, seq):
    T = proj.shape[0]
    H = DIFF_HEADS
    bq = min(BQ_ATTN, seq)
    nq = seq // bq
    q0 = OFF_DQ // DIFF_DK
    k0 = OFF_DK // DIFF_DK
    v0 = OFF_DV // DIFF_DV
    return pl.pallas_call(
        functools.partial(_attn_kernel, bq=bq, lambda_init=lambda_init),
        grid=(batch, H, nq),
        in_specs=[pl.BlockSpec((4, DIFF_DK), lambda b, h, i: (0, 0)),
                  pl.BlockSpec((1, DIFF_DV), lambda b, h, i: (0, 0)),
                  pl.BlockSpec((bq, DIFF_DK), lambda b, h, i: (b * nq + i, q0 + 2 * h)),
                  pl.BlockSpec((bq, DIFF_DK), lambda b, h, i: (b * nq + i, q0 + 2 * h + 1)),
                  pl.BlockSpec((seq, DIFF_DK), lambda b, h, i: (b, k0 + 2 * h)),
                  pl.BlockSpec((seq, DIFF_DK), lambda b, h, i: (b, k0 + 2 * h + 1)),
                  pl.BlockSpec((seq, DIFF_DV), lambda b, h, i: (b, v0 + h))],
        out_specs=pl.BlockSpec((bq, DIFF_DV), lambda b, h, i: (b * nq + i, h)),
        out_shape=jax.ShapeDtypeStruct((T, H * DIFF_DV), BF16),
        scratch_shapes=[pltpu.VMEM((2, bq, 1), F32),
                        pltpu.VMEM((2, bq, 1), F32),
                        pltpu.VMEM((2, bq, DIFF_DV), F32)],
        compiler_params=_params("parallel", "parallel", "arbitrary"),
        name="diff_attn",
    )(lam4, diff_norm.reshape(1, DIFF_DV), proj, proj, proj, proj, proj)


def _post_kernel(ro_ref, do_ref, gr_ref, gd_ref, x_ref, mod_ref, wr_ref, wd_ref, wo_ref, nf_ref,
                 wrt_ref, brt_ref, x1_ref, h2_ref, route_ref, cnt_ref, carry_ref, *, tm):
    @pl.when(pl.program_id(0) == 0)
    def _():
        carry_ref[...] = jnp.zeros(carry_ref.shape, F32)

    mod = mod_ref[0]
    ret_out = _dot(ro_ref[...], wr_ref[...])
    diff_out = _dot(do_ref[...], wd_ref[...])
    merged = (_sigmoid(gr_ref[...].astype(F32)) * ret_out
              + _sigmoid(gd_ref[...].astype(F32)) * diff_out)
    x1 = x_ref[...] + mod[2:3] * _dot(merged.astype(BF16), wo_ref[...])
    x1_ref[...] = x1
    h2 = _rms(x1) * nf_ref[...] * (1.0 + mod[4:5]) + mod[3:4]
    h2_ref[...] = h2

    lg = _dot(h2.astype(BF16), wrt_ref[...]) + brt_ref[...]
    lane = lax.broadcasted_iota(jnp.int32, lg.shape, 1)
    far = jnp.int32(LANES)

    def top1(vals):
        best = jnp.max(vals, axis=1, keepdims=True)
        return best, jnp.min(jnp.where(vals == best, lane, far), axis=1, keepdims=True)

    is_group = (lane >= N_EXPERTS) & (lane < N_EXPERTS + N_GROUPS)
    g_best, g_lane = top1(jnp.where(is_group, lg, NEG_BIG))
    g_sum = jnp.sum(jnp.where(is_group, jnp.exp(lg - g_best), 0.0), axis=1, keepdims=True)
    g_top = 1.0 / g_sum
    first = (g_lane - N_EXPERTS) * EXPERTS_PER_GROUP
    in_group = (lane >= first) & (lane < first + EXPERTS_PER_GROUP)
    el = jnp.where(in_group, lg, NEG_BIG)
    e_a, i_a = top1(el)
    e_b, i_b = top1(jnp.where(lane == i_a, NEG_BIG, el))
    t = jnp.exp(e_b - e_a)
    w_a = g_top / (1.0 + t)
    w_b = g_top * t / (1.0 + t)

    hot_a = lane == i_a
    hot_b = lane == i_b
    hot = jnp.where(hot_a | hot_b, 1.0, 0.0)
    r_i = lax.broadcasted_iota(jnp.int32, (tm, tm), 0)
    c_i = lax.broadcasted_iota(jnp.int32, (tm, tm), 1)
    lower = jnp.where(c_i < r_i, 1.0, 0.0).astype(BF16)
    before = _dot(lower, hot.astype(BF16)) + carry_ref[...]
    rank_a = jnp.sum(jnp.where(hot_a, before, 0.0), axis=1, keepdims=True)
    rank_b = jnp.sum(jnp.where(hot_b, before, 0.0), axis=1, keepdims=True)
    total = carry_ref[...] + jnp.sum(hot, axis=0, keepdims=True)
    carry_ref[...] = total
    cnt_ref[...] = total

    fields = (i_a.astype(F32), i_b.astype(F32), w_a, w_b, rank_a, rank_b)
    route = jnp.zeros(lg.shape, F32)
    for n, f in enumerate(fields):
        route = jnp.where(lane == n, f, route)
    route_ref[...] = route


def _post_mixer(ro, do, proj, xf, mod, w_ret_o, w_diff_o, w_out, norm_ffn, w_rt, b_rt, seq):
    T, D = xf.shape
    tm = min(TM_POST, seq)
    per_batch = seq // tm
    row = lambda i: (i, 0)
    const = lambda i: (0, 0)
    return pl.pallas_call(
        functools.partial(_post_kernel, tm=tm),
        grid=(T // tm,),
        in_specs=[pl.BlockSpec((tm, D), row),
                  pl.BlockSpec((tm, D), row),
                  pl.BlockSpec((tm, D), lambda i: (i, OFF_GR // D)),
                  pl.BlockSpec((tm, D), lambda i: (i, OFF_GD // D)),
                  pl.BlockSpec((tm, D), row),
                  pl.BlockSpec((1, 6, D), lambda i: (i // per_batch, 0, 0)),
                  pl.BlockSpec((D, D), const),
                  pl.BlockSpec((D, D), const),
                  pl.BlockSpec((D, D), const),
                  pl.BlockSpec((1, D), const),
                  pl.BlockSpec((D, LANES), const),
                  pl.BlockSpec((1, LANES), const)],
        out_specs=[pl.BlockSpec((tm, D), row),
                   pl.BlockSpec((tm, D), row),
                   pl.BlockSpec((tm, LANES), row),
                   pl.BlockSpec((1, LANES), const)],
        out_shape=[jax.ShapeDtypeStruct((T, D), F32),
                   jax.ShapeDtypeStruct((T, D), F32),
                   jax.ShapeDtypeStruct((T, LANES), F32),
                   jax.ShapeDtypeStruct((1, LANES), F32)],
        scratch_shapes=[pltpu.VMEM((1, LANES), F32)],
        compiler_params=_params("arbitrary"),
        name="post_mixer",
    )(ro, do, proj, proj, xf, mod, w_ret_o, w_diff_o, w_out, norm_ffn.reshape(1, D), w_rt, b_rt)


def _row_copy(src, s, dst, d, sem):
    return pltpu.make_async_copy(src.at[pl.ds(s, 1), :], dst.at[pl.ds(d, 1), :], sem)


def _scatter_kernel(zb_ref, dest_ref, h_hbm, o_hbm, zero_ref, zsem, sem, *, ts, tb):
    i = pl.program_id(0)

    @pl.when(i == 0)
    def _():
        zero_ref[...] = jnp.zeros(zero_ref.shape, F32)

        def zero_copy(n):
            start = pl.multiple_of(zb_ref[n] * tb, tb)
            return pltpu.make_async_copy(zero_ref, o_hbm.at[pl.ds(start, tb), :], zsem)

        for n in range(zb_ref.shape[0]):
            pl.when(zb_ref[n] >= 0)(lambda n=n: zero_copy(n).start())
        for n in range(zb_ref.shape[0]):
            pl.when(zb_ref[n] >= 0)(lambda n=n: zero_copy(n).wait())

    def issue(r, carry):
        for k in range(2):
            _row_copy(h_hbm, i * ts + r, o_hbm, dest_ref[0, 0, 2 * r + k], sem).start()
        return carry

    def drain_one_step():
        pltpu.make_async_copy(h_hbm.at[pl.ds(0, 2 * ts), :], o_hbm.at[pl.ds(0, 2 * ts), :], sem).wait()

    lax.fori_loop(0, ts, issue, 0, unroll=8)
    pl.when(i > 0)(drain_one_step)
    pl.when(i == pl.num_programs(0) - 1)(drain_one_step)


def _moe_scatter(h2, dest, zero_blocks, n_rows):
    T, D = h2.shape
    ts = min(TS_SCATTER, T)
    tb = TB_MOE
    grid_spec = pltpu.PrefetchScalarGridSpec(
        num_scalar_prefetch=1,
        grid=(T // ts,),
        in_specs=[pl.BlockSpec((1, 1, 2 * ts), lambda i, zb: (i, 0, 0), memory_space=pltpu.SMEM),
                  pl.BlockSpec(memory_space=pl.ANY)],
        out_specs=pl.BlockSpec(memory_space=pl.ANY),
        scratch_shapes=[pltpu.VMEM((tb, D), F32), pltpu.SemaphoreType.DMA(()), pltpu.SemaphoreType.DMA(())],
    )
    return pl.pallas_call(
        functools.partial(_scatter_kernel, ts=ts, tb=tb),
        grid_spec=grid_spec,
        out_shape=jax.ShapeDtypeStruct((n_rows, D), F32),
        compiler_params=_params("arbitrary"),
        name="moe_scatter",
    )(zero_blocks, dest.reshape(T // ts, 1, 2 * ts), h2)


def _expert_kernel(be_ref, nv_ref, x_ref, w1_ref, w3_ref, w2_ref, o_ref, w1b_ref, w3b_ref, w2b_ref):
    n = pl.program_id(0)
    nv = nv_ref[n]
    new_expert = (n == 0) | (be_ref[n] != be_ref[jnp.maximum(n - 1, 0)])

    @pl.when(new_expert & (nv > 0))
    def _():
        w1b_ref[...] = w1_ref[0].astype(BF16)
        w3b_ref[...] = w3_ref[0].astype(BF16)
        w2b_ref[...] = w2_ref[0].astype(BF16)

    @pl.when(nv > 0)
    def _():
        x = x_ref[...].astype(BF16)
        g = _dot(x, w1b_ref[...])
        u = _dot(x, w3b_ref[...])
        a = (g * _sigmoid(g) * u).astype(BF16)
        o_ref[...] = _dot(a, w2b_ref[...])

    @pl.when(nv == 0)
    def _():
        o_ref[...] = jnp.zeros(o_ref.shape, F32)


def _moe_experts(h_pad, block_e, block_nv, w1, w3, w2):
    P, D = h_pad.shape
    tb = TB_MOE
    wmap = lambda n, be, nv: (be[n], 0, 0)
    grid_spec = pltpu.PrefetchScalarGridSpec(
        num_scalar_prefetch=2,
        grid=(P // tb,),
        in_specs=[pl.BlockSpec((tb, D), lambda n, be, nv: (n, 0)),
                  pl.BlockSpec((1, D, D_EXPERT), wmap),
                  pl.BlockSpec((1, D, D_EXPERT), wmap),
                  pl.BlockSpec((1, D_EXPERT, D), wmap)],
        out_specs=pl.BlockSpec((tb, D), lambda n, be, nv: (n, 0)),
        scratch_shapes=[pltpu.VMEM((D, D_EXPERT), BF16), pltpu.VMEM((D, D_EXPERT), BF16),
                        pltpu.VMEM((D_EXPERT, D), BF16)],
    )
    return pl.pallas_call(
        _expert_kernel,
        grid_spec=grid_spec,
        out_shape=jax.ShapeDtypeStruct((P, D), F32),
        compiler_params=_params("arbitrary"),
        name="moe_experts",
    )(block_e, block_nv, h_pad, w1, w3, w2)


def _combine_kernel(dcur_ref, dnext_ref, x1_ref, route_ref, mod_ref, nw_ref, y_hbm, o_ref, buf, sem, *, tc, final):
    i = pl.program_id(0)
    slot = i % 2

    def gather(d_ref, s):
        def issue(r, carry):
            for k in range(2):
                _row_copy(y_hbm, d_ref[0, 0, 2 * r + k], buf.at[s], k * tc + r, sem.at[s]).start()
            return carry

        lax.fori_loop(0, tc, issue, 0, unroll=8)

    pl.when(i == 0)(lambda: gather(dcur_ref, 0))
    pl.when(i + 1 < pl.num_programs(0))(lambda: gather(dnext_ref, 1 - slot))
    pltpu.make_async_copy(y_hbm.at[pl.ds(0, 2 * tc), :], buf.at[slot], sem.at[slot]).wait()
    route = route_ref[...]
    y = route[:, 2:3] * buf[slot, 0:tc, :] + route[:, 3:4] * buf[slot, tc:2 * tc, :]
    x2 = x1_ref[...] + mod_ref[0][5:6] * y
    o_ref[...] = _rms(x2) * nw_ref[...] if final else x2


def _moe_combine(y_pad, dest, x1, route, mod, norm_w, seq, final):
    T, D = x1.shape
    tc = min(TC_COMBINE, seq)
    per_batch = seq // tc
    row = lambda i: (i, 0)
    n = T // tc
    dest3 = dest.reshape(n, 1, 2 * tc)
    return pl.pallas_call(
        functools.partial(_combine_kernel, tc=tc, final=final),
        grid=(n,),
        in_specs=[pl.BlockSpec((1, 1, 2 * tc), lambda i: (i, 0, 0), memory_space=pltpu.SMEM),
                  pl.BlockSpec((1, 1, 2 * tc), lambda i: (jnp.minimum(i + 1, n - 1), 0, 0),
                               memory_space=pltpu.SMEM),
                  pl.BlockSpec((tc, D), row),
                  pl.BlockSpec((tc, LANES), row),
                  pl.BlockSpec((1, 6, D), lambda i: (i // per_batch, 0, 0)),
                  pl.BlockSpec((1, D), lambda i: (0, 0)),
                  pl.BlockSpec(memory_space=pl.ANY)],
        out_specs=pl.BlockSpec((tc, D), row),
        out_shape=jax.ShapeDtypeStruct((T, D), F32),
        scratch_shapes=[pltpu.VMEM((2, 2 * tc, D), F32), pltpu.SemaphoreType.DMA((2,))],
        compiler_params=_params("arbitrary"),
        name="moe_combine",
    )(dest3, dest3, x1, route, mod, norm_w.reshape(1, D), y_pad)


def _routing_tables(route, counts, n_blocks):
    tb = TB_MOE
    expert = route[:, 0:2].astype(jnp.int32)
    rank = route[:, 4:6].astype(jnp.int32)
    cnt = counts[0, :N_EXPERTS].astype(jnp.int32)
    nblk = (cnt + tb - 1) // tb
    blk_end = jnp.cumsum(nblk)
    blk_start = blk_end - nblk
    hot = expert[:, :, None] == jnp.arange(N_EXPERTS, dtype=jnp.int32)
    dest = jnp.sum(jnp.where(hot, blk_start * tb, 0), axis=-1) + rank
    blocks = jnp.arange(n_blocks, dtype=jnp.int32)
    block_e = jnp.minimum(jnp.sum(blocks[:, None] >= blk_end[None, :], axis=1), N_EXPERTS - 1).astype(jnp.int32)
    left = cnt[block_e] - (blocks - blk_start[block_e]) * tb
    block_nv = jnp.where(blocks < blk_end[-1], jnp.clip(left, 0, tb), 0).astype(jnp.int32)
    tail = blk_end[-1] + jnp.arange(N_EXPERTS, dtype=jnp.int32)
    zero_blocks = jnp.concatenate([jnp.where(nblk > 0, blk_end - 1, -1),
                                   jnp.where(tail < n_blocks, tail, -1)]).astype(jnp.int32)
    return dest.astype(jnp.int32), block_e, block_nv, zero_blocks


def kernel(x, c, positions, w_ada, b_ada, norm_mix, w_in, w_ret_o, w_diff_o, lam_q1, lam_k1, lam_q2, lam_k2,
           diff_norm, w_out, norm_ffn, w_router_group, b_router_group, w_router_expert, b_router_expert,
           w_exp_gate, w_exp_up, w_exp_down, norm_final):
    B, S, D = x.shape
    T = B * S
    depth = w_ada.shape[0]
    assert D_IN == w_in.shape[2] and S % RET_CHUNK == 0
    xf = x.reshape(T, D)
    pos = positions.reshape(T, 1)
    n_blocks = (2 * T) // TB_MOE + N_EXPERTS
    for l in range(depth):
        lambda_init = 0.8 - 0.6 * math.exp(-0.3 * l)
        mod = _adaln(c, w_ada[l], b_ada[l]).reshape(B, 6, D)
        proj = _inproj(xf, pos, mod, norm_mix[l], w_in[l].astype(BF16), S)
        ro = _retention(proj, B, S)
        lam4 = jnp.stack([lam_q1[l], lam_k1[l], lam_q2[l], lam_k2[l]]).astype(F32)
        do = _diff_attention(proj, lam4, diff_norm[l].astype(F32), lambda_init, B, S)

        pad = LANES - N_EXPERTS - N_GROUPS
        w_rt = jnp.concatenate([w_router_expert[l], w_router_group[l], jnp.zeros((D, pad), F32)], axis=1)
        b_rt = jnp.concatenate([b_router_expert[l], b_router_group[l], jnp.zeros((pad,), F32)]).reshape(1, LANES)
        x1, h2, route, counts = _post_mixer(
            ro, do, proj, xf, mod, w_ret_o[l].astype(BF16), w_diff_o[l].astype(BF16), w_out[l].astype(BF16),
            norm_ffn[l], w_rt.astype(BF16), b_rt, S)

        dest, block_e, block_nv, zero_blocks = _routing_tables(route, counts, n_blocks)
        h_pad = _moe_scatter(h2, dest.reshape(-1), zero_blocks, n_blocks * TB_MOE)
        y_pad = _moe_experts(h_pad, block_e, block_nv, w_exp_gate[l], w_exp_up[l], w_exp_down[l])
        xf = _moe_combine(y_pad, dest.reshape(-1), x1, route, mod, norm_final, S, l == depth - 1)
    return xf.reshape(B, S, D)
```

```python
import functools
import math

import jax
import jax.numpy as jnp
from jax import lax
from jax.experimental import pallas as pl
from jax.experimental.pallas import tpu as pltpu

F32 = jnp.float32
BF16 = jnp.bfloat16

EPS = 1e-6
ROPE_THETA = 10000.0
LANES = 128
RET_HEADS, RET_DK, RET_DV, RET_CHUNK = 4, 128, 256, 128
DIFF_HEADS, DIFF_DK, DIFF_DV = 4, 128, 256
N_GROUPS, EXPERTS_PER_GROUP, N_EXPERTS, D_EXPERT = 4, 8, 32, 512
OFF_RQ, OFF_RK, OFF_RV, OFF_RG, OFF_DQ, OFF_DK, OFF_DV, OFF_GR, OFF_GD, D_IN = (
    0, 512, 1024, 2048, 3072, 4096, 5120, 6144, 7168, 8192)
NEG_BIG = -1e30
VMEM_LIMIT_BYTES = 48 * 1024 * 1024

TM_INPROJ, TN_INPROJ = 1024, 1024
BQ_ATTN = 512
TM_POST = 512
TB_MOE = 256
TS_SCATTER = 256
TC_COMBINE = 256


def _params(*sem):
    return pltpu.CompilerParams(dimension_semantics=sem, vmem_limit_bytes=VMEM_LIMIT_BYTES)


def _sigmoid(v):
    return 1.0 / (1.0 + jnp.exp(-v))


def _rms(v):
    return v * lax.rsqrt(jnp.mean(v * v, axis=-1, keepdims=True) + EPS)


def _dot(a, b):
    return jnp.dot(a, b, preferred_element_type=F32)


def _dot_nt(a, b):
    return lax.dot_general(a, b, (((1,), (1,)), ((), ())), preferred_element_type=F32)


def _ada_kernel(c_ref, w_ref, b_ref, o_ref):
    c = c_ref[...]
    a = (c * _sigmoid(c)).astype(BF16)
    o_ref[...] = _dot(a, w_ref[...].astype(BF16)) + b_ref[...]


def _adaln(c, w, b):
    B, D = c.shape
    n = w.shape[1] // D
    return pl.pallas_call(
        _ada_kernel,
        grid=(n,),
        in_specs=[pl.BlockSpec((B, D), lambda j: (0, 0)),
                  pl.BlockSpec((D, D), lambda j: (0, j)),
                  pl.BlockSpec((1, D), lambda j: (0, j))],
        out_specs=pl.BlockSpec((B, D), lambda j: (0, j)),
        out_shape=jax.ShapeDtypeStruct((B, n * D), F32),
        compiler_params=_params("parallel"),
        name="adaln",
    )(c, w, b.reshape(1, -1))


def _inproj_kernel(x_ref, pos_ref, mod_ref, nw_ref, inv_ref, sgn_ref, cs_ref, w_ref, o_ref,
                   h_ref, cos_ref, sin_ref, *, tn):
    j = pl.program_id(1)

    @pl.when(j == 0)
    def _():
        mod = mod_ref[0]
        h = _rms(x_ref[...]) * nw_ref[...] * (1.0 + mod[1:2]) + mod[0:1]
        h_ref[...] = h.astype(BF16)
        ang = pos_ref[...].astype(F32) * inv_ref[...]
        cos_ref[...] = jnp.cos(ang)
        sin_ref[...] = jnp.sin(ang) * sgn_ref[...]

    acc = _dot(h_ref[...], w_ref[...])
    col = j * tn
    is_rope = (col < OFF_RV) | ((col >= OFF_DQ) & (col < OFF_DV))

    @pl.when(is_rope)
    def _():
        cos = cos_ref[...]
        sin = sin_ref[...]
        for k in range(tn // LANES):
            sl = slice(k * LANES, (k + 1) * LANES)
            a = acc[:, sl]
            r = a * cos + pltpu.roll(a, LANES // 2, 1) * sin
            o_ref[:, sl] = (r * cs_ref[:, sl]).astype(BF16)

    @pl.when(jnp.logical_not(is_rope))
    def _():
        o_ref[...] = acc.astype(BF16)


def _inproj(xf, pos, mod, norm_w, w_bf16, seq):
    T, D = xf.shape
    tm = min(TM_INPROJ, seq)
    tn = TN_INPROJ
    half = LANES // 2
    inv = ROPE_THETA ** (-jnp.arange(0, LANES, 2, dtype=F32) / LANES)
    inv = jnp.concatenate([inv, inv]).reshape(1, LANES)
    sgn = jnp.concatenate([-jnp.ones((half,), F32), jnp.ones((half,), F32)]).reshape(1, LANES)
    scale = RET_DK ** -0.5
    colscale = jnp.ones((D_IN,), F32)
    colscale = colscale.at[OFF_RK:OFF_RV].set(scale)
    colscale = colscale.at[OFF_DQ:OFF_DK].set(DIFF_DK ** -0.5 * math.log2(math.e))
    colscale = colscale.reshape(1, D_IN)
    per_batch = seq // tm
    return pl.pallas_call(
        functools.partial(_inproj_kernel, tn=tn),
        grid=(T // tm, D_IN // tn),
        in_specs=[pl.BlockSpec((tm, D), lambda i, j: (i, 0)),
                  pl.BlockSpec((tm, 1), lambda i, j: (i, 0)),
                  pl.BlockSpec((1, 6, D), lambda i, j: (i // per_batch, 0, 0)),
                  pl.BlockSpec((1, D), lambda i, j: (0, 0)),
                  pl.BlockSpec((1, LANES), lambda i, j: (0, 0)),
                  pl.BlockSpec((1, LANES), lambda i, j: (0, 0)),
                  pl.BlockSpec((1, tn), lambda i, j: (0, j)),
                  pl.BlockSpec((D, tn), lambda i, j: (0, j))],
        out_specs=pl.BlockSpec((tm, tn), lambda i, j: (i, j)),
        out_shape=jax.ShapeDtypeStruct((T, D_IN), BF16),
        scratch_shapes=[pltpu.VMEM((tm, D), BF16),
                        pltpu.VMEM((tm, LANES), F32),
                        pltpu.VMEM((tm, LANES), F32)],
        compiler_params=_params("parallel", "arbitrary"),
        name="inproj",
    )(xf, pos, mod, norm_w.reshape(1, D), inv, sgn, colscale, w_bf16)


def _ret_kernel(q_ref, k_ref, v_ref, g_ref, dmask_ref, xi_ref, zeta_ref, cd_ref, o_ref, state_ref):
    @pl.when(pl.program_id(1) == 0)
    def _():
        state_ref[...] = jnp.zeros(state_ref.shape, F32)

    for h in range(RET_HEADS):
        qk = slice(h * RET_DK, (h + 1) * RET_DK)
        vv = slice(h * RET_DV, (h + 1) * RET_DV)
        q = q_ref[:, qk]
        k = k_ref[:, qk]
        v = v_ref[:, vv]
        st = state_ref[h]
        s = _dot_nt(q, k) * dmask_ref[h]
        qx = (q.astype(F32) * xi_ref[h]).astype(BF16)
        o = _dot(s.astype(BF16), v) + _dot(qx, st.astype(BF16))
        kz_t = (k.astype(F32) * zeta_ref[h]).T.astype(BF16)
        state_ref[h] = cd_ref[h] * st + _dot(kz_t, v)
        g = g_ref[:, vv].astype(F32)
        o_ref[:, vv] = (_rms(o) * (g * _sigmoid(g))).astype(BF16)


def _retention(proj, batch, seq):
    T = proj.shape[0]
    C = RET_CHUNK
    H = RET_HEADS
    nc = seq // C
    gamma = 1.0 - jnp.exp2(-5.0 - jnp.arange(H, dtype=F32))
    log_g = jnp.log(gamma)
    idx = jnp.arange(C, dtype=F32)
    rel = idx[:, None] - idx[None, :]
    dmask = jnp.where(rel >= 0, jnp.exp(log_g[:, None, None] * jnp.maximum(rel, 0.0)), 0.0)
    zeta = jnp.exp(log_g[:, None] * (C - 1 - idx))
    xi = jnp.exp(log_g[:, None] * (idx + 1))
    cd = jnp.exp(log_g * C)
    zeta_b = jnp.broadcast_to(zeta[:, :, None], (H, C, RET_DK))
    xi_b = jnp.broadcast_to(xi[:, :, None], (H, C, RET_DK))
    cd_b = jnp.broadcast_to(cd[:, None, None], (H, 1, RET_DV))
    wq = H * RET_DK
    wv = H * RET_DV
    row = lambda b, n: b * nc + n
    const3 = lambda b, n: (0, 0, 0)
    return pl.pallas_call(
        _ret_kernel,
        grid=(batch, nc),
        in_specs=[pl.BlockSpec((C, wq), lambda b, n: (row(b, n), OFF_RQ // wq)),
                  pl.BlockSpec((C, wq), lambda b, n: (row(b, n), OFF_RK // wq)),
                  pl.BlockSpec((C, wv), lambda b, n: (row(b, n), OFF_RV // wv)),
                  pl.BlockSpec((C, wv), lambda b, n: (row(b, n), OFF_RG // wv)),
                  pl.BlockSpec((H, C, C), const3),
                  pl.BlockSpec((H, C, RET_DK), const3),
                  pl.BlockSpec((H, C, RET_DK), const3),
                  pl.BlockSpec((H, 1, RET_DV), const3)],
        out_specs=pl.BlockSpec((C, wv), lambda b, n: (row(b, n), 0)),
        out_shape=jax.ShapeDtypeStruct((T, wv), BF16),
        scratch_shapes=[pltpu.VMEM((H, RET_DK, RET_DV), F32)],
        compiler_params=_params("parallel", "arbitrary"),
        name="retention",
    )(proj, proj, proj, proj, dmask, xi_b, zeta_b, cd_b)


def _transpose_bf16(a):
    return a.astype(F32).T.astype(BF16)


def _attn_kernel(lam_ref, dn_ref, q1_ref, q2_ref, k1_ref, k2_ref, v_ref, o_ref,
                 vt_ref, m_ref, l_ref, acc_ref, *, bq, lambda_init):
    i = pl.program_id(2)
    nq = vt_ref.shape[0]

    @pl.when(i == 0)
    def _():
        for c in range(nq):
            vt_ref[c] = _transpose_bf16(v_ref[c * bq:(c + 1) * bq, :])

    m_ref[...] = jnp.full(m_ref.shape, NEG_BIG, F32)
    l_ref[...] = jnp.zeros(l_ref.shape, F32)
    acc_ref[...] = jnp.zeros(acc_ref.shape, F32)
    qts = (_transpose_bf16(q1_ref[...]), _transpose_bf16(q2_ref[...]))
    k_refs = (k1_ref, k2_ref)

    def block(j, diagonal):
        start = pl.multiple_of(j * bq, bq)
        vt = vt_ref[j]
        for c in range(2):
            k = k_refs[c][pl.ds(start, bq), :]
            st = _dot(k, qts[c])
            if diagonal:
                key = lax.broadcasted_iota(jnp.int32, (bq, bq), 0)
                qry = lax.broadcasted_iota(jnp.int32, (bq, bq), 1)
                st = jnp.where(key <= qry, st, NEG_BIG)
            m_prev = m_ref[c]
            m_new = jnp.maximum(m_prev, jnp.max(st, axis=0, keepdims=True))
            p = jnp.exp2(st - m_new)
            alpha = jnp.exp2(m_prev - m_new)
            l_ref[c] = alpha * l_ref[c] + jnp.sum(p, axis=0, keepdims=True)
            acc_ref[c] = alpha * acc_ref[c] + _dot(vt, p.astype(BF16))
            m_ref[c] = m_new

    def body(j, carry):
        block(j, False)
        return carry

    lax.fori_loop(0, i, body, 0)
    block(i, True)

    lam4 = lam_ref[...]
    lam = (jnp.exp(jnp.sum(lam4[0:1] * lam4[1:2], axis=1, keepdims=True))
           - jnp.exp(jnp.sum(lam4[2:3] * lam4[3:4], axis=1, keepdims=True)) + lambda_init)
    ot = acc_ref[0] * (1.0 / l_ref[0]) - lam * (acc_ref[1] * (1.0 / l_ref[1]))
    ot = ot * lax.rsqrt(jnp.mean(ot * ot, axis=0, keepdims=True) + EPS)
    o_ref[...] = (ot.T * dn_ref[...] * (1.0 - lambda_init)).astype(BF16)


def _diff_attention(proj, lam4, diff_norm, lambda_init, batch, seq):
    T = proj.shape[0]
    H = DIFF_HEADS
    bq = min(BQ_ATTN, seq)
    nq = seq // bq
    q0 = OFF_DQ // DIFF_DK
    k0 = OFF_DK // DIFF_DK
    v0 = OFF_DV // DIFF_DV
    return pl.pallas_call(
        functools.partial(_attn_kernel, bq=bq, lambda_init=lambda_init),
        grid=(batch, H, nq),
        in_specs=[pl.BlockSpec((4, DIFF_DK), lambda b, h, i: (0, 0)),
                  pl.BlockSpec((1, DIFF_DV), lambda b, h, i: (0, 0)),
                  pl.BlockSpec((bq, DIFF_DK), lambda b, h, i: (b * nq + i, q0 + 2 * h)),
                  pl.BlockSpec((bq, DIFF_DK), lambda b, h, i: (b * nq + i, q0 + 2 * h + 1)),
                  pl.BlockSpec((seq, DIFF_DK), lambda b, h, i: (b, k0 + 2 * h)),
                  pl.BlockSpec((seq, DIFF_DK), lambda b, h, i: (b, k0 + 2 * h + 1)),
                  pl.BlockSpec((seq, DIFF_DV), lambda b, h, i: (b, v0 + h))],
        out_specs=pl.BlockSpec((bq, DIFF_DV), lambda b, h, i: (b * nq + i, h)),
        out_shape=jax.ShapeDtypeStruct((T, H * DIFF_DV), BF16),
        scratch_shapes=[pltpu.VMEM((nq, DIFF_DV, bq), BF16),
                        pltpu.VMEM((2, 1, bq), F32),
                        pltpu.VMEM((2, 1, bq), F32),
                        pltpu.VMEM((2, DIFF_DV, bq), F32)],
        compiler_params=_params("parallel", "parallel", "arbitrary"),
        name="diff_attn",
    )(lam4, diff_norm.reshape(1, DIFF_DV), proj, proj, proj, proj, proj)


def _post_kernel(ro_ref, do_ref, gr_ref, gd_ref, x_ref, mod_ref, wr_ref, wd_ref, wo_ref, nf_ref,
                 wrt_ref, brt_ref, x1_ref, h2_ref, route_ref, cnt_ref, carry_ref, *, tm):
    @pl.when(pl.program_id(0) == 0)
    def _():
        carry_ref[...] = jnp.zeros(carry_ref.shape, F32)

    mod = mod_ref[0]
    ret_out = _dot(ro_ref[...], wr_ref[...])
    diff_out = _dot(do_ref[...], wd_ref[...])
    merged = (_sigmoid(gr_ref[...].astype(F32)) * ret_out
              + _sigmoid(gd_ref[...].astype(F32)) * diff_out)
    x1 = x_ref[...] + mod[2:3] * _dot(merged.astype(BF16), wo_ref[...])
    x1_ref[...] = x1
    h2 = _rms(x1) * nf_ref[...] * (1.0 + mod[4:5]) + mod[3:4]
    h2_ref[...] = h2

    lg = _dot(h2.astype(BF16), wrt_ref[...]) + brt_ref[...]
    lane = lax.broadcasted_iota(jnp.int32, lg.shape, 1)
    far = jnp.int32(LANES)

    def top1(vals):
        best = jnp.max(vals, axis=1, keepdims=True)
        return best, jnp.min(jnp.where(vals == best, lane, far), axis=1, keepdims=True)

    is_group = (lane >= N_EXPERTS) & (lane < N_EXPERTS + N_GROUPS)
    g_best, g_lane = top1(jnp.where(is_group, lg, NEG_BIG))
    g_sum = jnp.sum(jnp.where(is_group, jnp.exp(lg - g_best), 0.0), axis=1, keepdims=True)
    g_top = 1.0 / g_sum
    first = (g_lane - N_EXPERTS) * EXPERTS_PER_GROUP
    in_group = (lane >= first) & (lane < first + EXPERTS_PER_GROUP)
    el = jnp.where(in_group, lg, NEG_BIG)
    e_a, i_a = top1(el)
    e_b, i_b = top1(jnp.where(lane == i_a, NEG_BIG, el))
    t = jnp.exp(e_b - e_a)
    w_a = g_top / (1.0 + t)
    w_b = g_top * t / (1.0 + t)

    hot_a = lane == i_a
    hot_b = lane == i_b
    hot = jnp.where(hot_a | hot_b, 1.0, 0.0)
    r_i = lax.broadcasted_iota(jnp.int32, (tm, tm), 0)
    c_i = lax.broadcasted_iota(jnp.int32, (tm, tm), 1)
    lower = jnp.where(c_i < r_i, 1.0, 0.0).astype(BF16)
    before = _dot(lower, hot.astype(BF16)) + carry_ref[...]
    rank_a = jnp.sum(jnp.where(hot_a, before, 0.0), axis=1, keepdims=True)
    rank_b = jnp.sum(jnp.where(hot_b, before, 0.0), axis=1, keepdims=True)
    total = carry_ref[...] + jnp.sum(hot, axis=0, keepdims=True)
    carry_ref[...] = total
    cnt_ref[...] = total

    fields = (i_a.astype(F32), i_b.astype(F32), w_a, w_b, rank_a, rank_b)
    route = jnp.zeros(lg.shape, F32)
    for n, f in enumerate(fields):
        route = jnp.where(lane == n, f, route)
    route_ref[...] = route


def _post_mixer(ro, do, proj, xf, mod, w_ret_o, w_diff_o, w_out, norm_ffn, w_rt, b_rt, seq):
    T, D = xf.shape
    tm = min(TM_POST, seq)
    per_batch = seq // tm
    row = lambda i: (i, 0)
    const = lambda i: (0, 0)
    return pl.pallas_call(
        functools.partial(_post_kernel, tm=tm),
        grid=(T // tm,),
        in_specs=[pl.BlockSpec((tm, D), row),
                  pl.BlockSpec((tm, D), row),
                  pl.BlockSpec((tm, D), lambda i: (i, OFF_GR // D)),
                  pl.BlockSpec((tm, D), lambda i: (i, OFF_GD // D)),
                  pl.BlockSpec((tm, D), row),
                  pl.BlockSpec((1, 6, D), lambda i: (i // per_batch, 0, 0)),
                  pl.BlockSpec((D, D), const),
                  pl.BlockSpec((D, D), const),
                  pl.BlockSpec((D, D), const),
                  pl.BlockSpec((1, D), const),
                  pl.BlockSpec((D, LANES), const),
                  pl.BlockSpec((1, LANES), const)],
        out_specs=[pl.BlockSpec((tm, D), row),
                   pl.BlockSpec((tm, D), row),
                   pl.BlockSpec((tm, LANES), row),
                   pl.BlockSpec((1, LANES), const)],
        out_shape=[jax.ShapeDtypeStruct((T, D), F32),
                   jax.ShapeDtypeStruct((T, D), F32),
                   jax.ShapeDtypeStruct((T, LANES), F32),
                   jax.ShapeDtypeStruct((1, LANES), F32)],
        scratch_shapes=[pltpu.VMEM((1, LANES), F32)],
        compiler_params=_params("arbitrary"),
        name="post_mixer",
    )(ro, do, proj, proj, xf, mod, w_ret_o, w_diff_o, w_out, norm_ffn.reshape(1, D), w_rt, b_rt)


def _row_copy(src, s, dst, d, sem):
    return pltpu.make_async_copy(src.at[pl.ds(s, 1), :], dst.at[pl.ds(d, 1), :], sem)


def _scatter_kernel(zb_ref, dest_ref, h_ref, o_hbm, zero_ref, zsem, sem, *, ts, tb):
    i = pl.program_id(0)

    @pl.when(i == 0)
    def _():
        zero_ref[...] = jnp.zeros(zero_ref.shape, F32)

        def zero_copy(n):
            start = pl.multiple_of(zb_ref[n] * tb, tb)
            return pltpu.make_async_copy(zero_ref, o_hbm.at[pl.ds(start, tb), :], zsem)

        for n in range(zb_ref.shape[0]):
            pl.when(zb_ref[n] >= 0)(lambda n=n: zero_copy(n).start())
        for n in range(zb_ref.shape[0]):
            pl.when(zb_ref[n] >= 0)(lambda n=n: zero_copy(n).wait())

    def issue(r, carry):
        for k in range(2):
            _row_copy(h_ref, r, o_hbm, dest_ref[0, 0, 2 * r + k], sem).start()
        return carry

    lax.fori_loop(0, ts, issue, 0, unroll=8)
    pltpu.make_async_copy(o_hbm.at[pl.ds(0, 2 * ts), :], o_hbm.at[pl.ds(0, 2 * ts), :], sem).wait()


def _moe_scatter(h2, dest, zero_blocks, n_rows):
    T, D = h2.shape
    ts = min(TS_SCATTER, T)
    tb = TB_MOE
    grid_spec = pltpu.PrefetchScalarGridSpec(
        num_scalar_prefetch=1,
        grid=(T // ts,),
        in_specs=[pl.BlockSpec((1, 1, 2 * ts), lambda i, zb: (i, 0, 0), memory_space=pltpu.SMEM),
                  pl.BlockSpec((ts, D), lambda i, zb: (i, 0))],
        out_specs=pl.BlockSpec(memory_space=pl.ANY),
        scratch_shapes=[pltpu.VMEM((tb, D), F32), pltpu.SemaphoreType.DMA(()), pltpu.SemaphoreType.DMA(())],
    )
    return pl.pallas_call(
        functools.partial(_scatter_kernel, ts=ts, tb=tb),
        grid_spec=grid_spec,
        out_shape=jax.ShapeDtypeStruct((n_rows, D), F32),
        compiler_params=_params("arbitrary"),
        name="moe_scatter",
    )(zero_blocks, dest.reshape(T // ts, 1, 2 * ts), h2)


def _expert_kernel(be_ref, nv_ref, x_ref, w1_ref, w3_ref, w2_ref, o_ref, w1b_ref, w3b_ref, w2b_ref):
    n = pl.program_id(0)
    nv = nv_ref[n]
    new_expert = (n == 0) | (be_ref[n] != be_ref[jnp.maximum(n - 1, 0)])

    @pl.when(new_expert & (nv > 0))
    def _():
        w1b_ref[...] = w1_ref[0].astype(BF16)
        w3b_ref[...] = w3_ref[0].astype(BF16)
        w2b_ref[...] = w2_ref[0].astype(BF16)

    @pl.when(nv > 0)
    def _():
        x = x_ref[...].astype(BF16)
        g = _dot(x, w1b_ref[...])
        u = _dot(x, w3b_ref[...])
        a = (g * _sigmoid(g) * u).astype(BF16)
        o_ref[...] = _dot(a, w2b_ref[...])

    @pl.when(nv == 0)
    def _():
        o_ref[...] = jnp.zeros(o_ref.shape, F32)


def _moe_experts(h_pad, block_e, block_nv, w1, w3, w2):
    P, D = h_pad.shape
    tb = TB_MOE
    wmap = lambda n, be, nv: (be[n], 0, 0)
    grid_spec = pltpu.PrefetchScalarGridSpec(
        num_scalar_prefetch=2,
        grid=(P // tb,),
        in_specs=[pl.BlockSpec((tb, D), lambda n, be, nv: (n, 0)),
                  pl.BlockSpec((1, D, D_EXPERT), wmap),
                  pl.BlockSpec((1, D, D_EXPERT), wmap),
                  pl.BlockSpec((1, D_EXPERT, D), wmap)],
        out_specs=pl.BlockSpec((tb, D), lambda n, be, nv: (n, 0)),
        scratch_shapes=[pltpu.VMEM((D, D_EXPERT), BF16), pltpu.VMEM((D, D_EXPERT), BF16),
                        pltpu.VMEM((D_EXPERT, D), BF16)],
    )
    return pl.pallas_call(
        _expert_kernel,
        grid_spec=grid_spec,
        out_shape=jax.ShapeDtypeStruct((P, D), F32),
        compiler_params=_params("arbitrary"),
        name="moe_experts",
    )(block_e, block_nv, h_pad, w1, w3, w2)


def _combine_kernel(dcur_ref, dnext_ref, x1_ref, route_ref, mod_ref, nw_ref, y_hbm, o_ref, buf, sem, *, tc, final):
    i = pl.program_id(0)
    slot = i % 2

    def gather(d_ref, s):
        def issue(r, carry):
            for k in range(2):
                _row_copy(y_hbm, d_ref[0, 0, 2 * r + k], buf.at[s], k * tc + r, sem.at[s]).start()
            return carry

        lax.fori_loop(0, tc, issue, 0, unroll=8)

    pl.when(i == 0)(lambda: gather(dcur_ref, 0))
    pl.when(i + 1 < pl.num_programs(0))(lambda: gather(dnext_ref, 1 - slot))
    pltpu.make_async_copy(y_hbm.at[pl.ds(0, 2 * tc), :], buf.at[slot], sem.at[slot]).wait()
    route = route_ref[...]
    y = route[:, 2:3] * buf[slot, 0:tc, :] + route[:, 3:4] * buf[slot, tc:2 * tc, :]
    x2 = x1_ref[...] + mod_ref[0][5:6] * y
    o_ref[...] = _rms(x2) * nw_ref[...] if final else x2


def _moe_combine(y_pad, dest, x1, route, mod, norm_w, seq, final):
    T, D = x1.shape
    tc = min(TC_COMBINE, seq)
    per_batch = seq // tc
    row = lambda i: (i, 0)
    n = T // tc
    dest3 = dest.reshape(n, 1, 2 * tc)
    return pl.pallas_call(
        functools.partial(_combine_kernel, tc=tc, final=final),
        grid=(n,),
        in_specs=[pl.BlockSpec((1, 1, 2 * tc), lambda i: (i, 0, 0), memory_space=pltpu.SMEM),
                  pl.BlockSpec((1, 1, 2 * tc), lambda i: (jnp.minimum(i + 1, n - 1), 0, 0),
                               memory_space=pltpu.SMEM),
                  pl.BlockSpec((tc, D), row),
                  pl.BlockSpec((tc, LANES), row),
                  pl.BlockSpec((1, 6, D), lambda i: (i // per_batch, 0, 0)),
                  pl.BlockSpec((1, D), lambda i: (0, 0)),
                  pl.BlockSpec(memory_space=pl.ANY)],
        out_specs=pl.BlockSpec((tc, D), row),
        out_shape=jax.ShapeDtypeStruct((T, D), F32),
        scratch_shapes=[pltpu.VMEM((2, 2 * tc, D), F32), pltpu.SemaphoreType.DMA((2,))],
        compiler_params=_params("arbitrary"),
        name="moe_combine",
    )(dest3, dest3, x1, route, mod, norm_w.reshape(1, D), y_pad)


def _routing_tables(route, counts, n_blocks):
    tb = TB_MOE
    expert = route[:, 0:2].astype(jnp.int32)
    rank = route[:, 4:6].astype(jnp.int32)
    cnt = counts[0, :N_EXPERTS].astype(jnp.int32)
    nblk = (cnt + tb - 1) // tb
    blk_end = jnp.cumsum(nblk)
    blk_start = blk_end - nblk
    hot = expert[:, :, None] == jnp.arange(N_EXPERTS, dtype=jnp.int32)
    dest = jnp.sum(jnp.where(hot, blk_start * tb, 0), axis=-1) + rank
    blocks = jnp.arange(n_blocks, dtype=jnp.int32)
    block_e = jnp.minimum(jnp.sum(blocks[:, None] >= blk_end[None, :], axis=1), N_EXPERTS - 1).astype(jnp.int32)
    left = cnt[block_e] - (blocks - blk_start[block_e]) * tb
    block_nv = jnp.where(blocks < blk_end[-1], jnp.clip(left, 0, tb), 0).astype(jnp.int32)
    tail = blk_end[-1] + jnp.arange(N_EXPERTS, dtype=jnp.int32)
    zero_blocks = jnp.concatenate([jnp.where(nblk > 0, blk_end - 1, -1),
                                   jnp.where(tail < n_blocks, tail, -1)]).astype(jnp.int32)
    return dest.astype(jnp.int32), block_e, block_nv, zero_blocks


def kernel(x, c, positions, w_ada, b_ada, norm_mix, w_in, w_ret_o, w_diff_o, lam_q1, lam_k1, lam_q2, lam_k2,
           diff_norm, w_out, norm_ffn, w_router_group, b_router_group, w_router_expert, b_router_expert,
           w_exp_gate, w_exp_up, w_exp_down, norm_final):
    B, S, D = x.shape
    T = B * S
    depth = w_ada.shape[0]
    assert D_IN == w_in.shape[2] and S % RET_CHUNK == 0
    xf = x.reshape(T, D)
    pos = positions.reshape(T, 1)
    n_blocks = (2 * T) // TB_MOE + N_EXPERTS
    for l in range(depth):
        lambda_init = 0.8 - 0.6 * math.exp(-0.3 * l)
        mod = _adaln(c, w_ada[l], b_ada[l]).reshape(B, 6, D)
        proj = _inproj(xf, pos, mod, norm_mix[l], w_in[l].astype(BF16), S)
        ro = _retention(proj, B, S)
        lam4 = jnp.stack([lam_q1[l], lam_k1[l], lam_q2[l], lam_k2[l]]).astype(F32)
        do = _diff_attention(proj, lam4, diff_norm[l].astype(F32), lambda_init, B, S)

        pad = LANES - N_EXPERTS - N_GROUPS
        w_rt = jnp.concatenate([w_router_expert[l], w_router_group[l], jnp.zeros((D, pad), F32)], axis=1)
        b_rt = jnp.concatenate([b_router_expert[l], b_router_group[l], jnp.zeros((pad,), F32)]).reshape(1, LANES)
        x1, h2, route, counts = _post_mixer(
            ro, do, proj, xf, mod, w_ret_o[l].astype(BF16), w_diff_o[l].astype(BF16), w_out[l].astype(BF16),
            norm_ffn[l], w_rt.astype(BF16), b_rt, S)

        dest, block_e, block_nv, zero_blocks = _routing_tables(route, counts, n_blocks)
        h_pad = _moe_scatter(h2, dest.reshape(-1), zero_blocks, n_blocks * TB_MOE)
        y_pad = _moe_experts(h_pad, block_e, block_nv, w_exp_gate[l], w_exp_up[l], w_exp_down[l])
        xf = _moe_combine(y_pad, dest.reshape(-1), x1, route, mod, norm_final, S, l == depth - 1)
    return xf.reshape(B, S, D)
```

```python
import functools
import math

import jax
import jax.numpy as jnp
from jax import lax
from jax.experimental import pallas as pl
from jax.experimental.pallas import tpu as pltpu

F32 = jnp.float32
BF16 = jnp.bfloat16

EPS = 1e-6
ROPE_THETA = 10000.0
LANES = 128
RET_HEADS, RET_DK, RET_DV, RET_CHUNK = 4, 128, 256, 128
DIFF_HEADS, DIFF_DK, DIFF_DV = 4, 128, 256
N_GROUPS, EXPERTS_PER_GROUP, N_EXPERTS, D_EXPERT = 4, 8, 32, 512
OFF_RQ, OFF_RK, OFF_RV, OFF_RG, OFF_DQ, OFF_DK, OFF_DV, OFF_GR, OFF_GD, D_IN = (
    0, 512, 1024, 2048, 3072, 4096, 5120, 6144, 7168, 8192)
NEG_BIG = -1e30
VMEM_LIMIT_BYTES = 48 * 1024 * 1024
VMEM_LIMIT_INPROJ_BYTES = 56 * 1024 * 1024

TM_INPROJ, TN_INPROJ = 512, 1024
BQ_ATTN = 512
TM_POST = 512
TB_MOE = 256
TS_SCATTER = 256
TC_COMBINE = 256


def _params(*sem):
    return pltpu.CompilerParams(dimension_semantics=sem, vmem_limit_bytes=VMEM_LIMIT_BYTES)


def _sigmoid(v):
    return 1.0 / (1.0 + jnp.exp(-v))


def _rms(v):
    return v * lax.rsqrt(jnp.mean(v * v, axis=-1, keepdims=True) + EPS)


def _dot(a, b):
    return jnp.dot(a, b, preferred_element_type=F32)


def _dot_nt(a, b):
    return lax.dot_general(a, b, (((1,), (1,)), ((), ())), preferred_element_type=F32)


def _ada_kernel(c_ref, w_ref, b_ref, o_ref):
    c = c_ref[...]
    a = (c * _sigmoid(c)).astype(BF16)
    o_ref[...] = _dot(a, w_ref[...].astype(BF16)) + b_ref[...]


def _adaln(c, w, b):
    B, D = c.shape
    n = w.shape[1] // D
    return pl.pallas_call(
        _ada_kernel,
        grid=(n,),
        in_specs=[pl.BlockSpec((B, D), lambda j: (0, 0)),
                  pl.BlockSpec((D, D), lambda j: (0, j)),
                  pl.BlockSpec((1, D), lambda j: (0, j))],
        out_specs=pl.BlockSpec((B, D), lambda j: (0, j)),
        out_shape=jax.ShapeDtypeStruct((B, n * D), F32),
        compiler_params=_params("parallel"),
        name="adaln",
    )(c, w, b.reshape(1, -1))


ROPE_GROUPS = ((OFF_RQ, OFF_RK, 1.0), (OFF_RK, OFF_RV, RET_DK ** -0.5),
               (OFF_DQ, OFF_DK, DIFF_DK ** -0.5 * math.log2(math.e)), (OFF_DK, OFF_DV, 1.0))


def _rope_scale(col):
    for lo, hi, scale in ROPE_GROUPS:
        if lo <= col < hi:
            return scale
    return None


def _inproj_kernel(x0_ref, pos0_ref, mod0_ref, xn_ref, posn_ref, modn_ref, nw_ref, inv_ref, sgn_ref,
                   w_ref, o_ref, h_ref, cos_ref, sin_ref, *, tn):
    i = pl.program_id(0)

    def prepare(x, pos, mod, slot):
        h = _rms(x) * nw_ref[...] * (1.0 + mod[1:2]) + mod[0:1]
        h_ref[slot] = h.astype(BF16)
        ang = pos.astype(F32) * inv_ref[...]
        cos_ref[slot] = jnp.cos(ang)
        sin_ref[slot] = jnp.sin(ang) * sgn_ref[...]

    @pl.when(i == 0)
    def _():
        prepare(x0_ref[...], pos0_ref[...], mod0_ref[0], 0)

    cur = i % 2
    for g in range(o_ref.shape[1] // tn):
        acc = _dot(h_ref[cur], w_ref[:, g * tn:(g + 1) * tn])
        for k in range(tn // LANES):
            first = g * tn + k * LANES
            a = acc[:, k * LANES:(k + 1) * LANES]
            scale = _rope_scale(first)
            if scale is not None:
                a = a * cos_ref[cur] + pltpu.roll(a, LANES // 2, 1) * sin_ref[cur]
                if scale != 1.0:
                    a = a * scale
            o_ref[:, first:first + LANES] = a.astype(BF16)

    prepare(xn_ref[...], posn_ref[...], modn_ref[0], 1 - cur)


def _inproj(xf, pos, mod, norm_w, w_bf16, seq):
    T, D = xf.shape
    tm = min(TM_INPROJ, seq)
    tn = TN_INPROJ
    half = LANES // 2
    inv = ROPE_THETA ** (-jnp.arange(0, LANES, 2, dtype=F32) / LANES)
    inv = jnp.concatenate([inv, inv]).reshape(1, LANES)
    sgn = jnp.concatenate([-jnp.ones((half,), F32), jnp.ones((half,), F32)]).reshape(1, LANES)
    per_batch = seq // tm
    n_m = T // tm
    nxt = lambda i: jnp.minimum(i + 1, n_m - 1)
    once = dict(pipeline_mode=pl.Buffered(1))
    return pl.pallas_call(
        functools.partial(_inproj_kernel, tn=tn),
        grid=(n_m,),
        in_specs=[pl.BlockSpec((tm, D), lambda i: (0, 0), **once),
                  pl.BlockSpec((tm, 1), lambda i: (0, 0), **once),
                  pl.BlockSpec((1, 6, D), lambda i: (0, 0, 0), **once),
                  pl.BlockSpec((tm, D), lambda i: (nxt(i), 0)),
                  pl.BlockSpec((tm, 1), lambda i: (nxt(i), 0)),
                  pl.BlockSpec((1, 6, D), lambda i: (nxt(i) // per_batch, 0, 0)),
                  pl.BlockSpec((1, D), lambda i: (0, 0), **once),
                  pl.BlockSpec((1, LANES), lambda i: (0, 0), **once),
                  pl.BlockSpec((1, LANES), lambda i: (0, 0), **once),
                  pl.BlockSpec((D, D_IN), lambda i: (0, 0), **once)],
        out_specs=pl.BlockSpec((tm, D_IN), lambda i: (i, 0)),
        out_shape=jax.ShapeDtypeStruct((T, D_IN), BF16),
        scratch_shapes=[pltpu.VMEM((2, tm, D), BF16),
                        pltpu.VMEM((2, tm, LANES), F32),
                        pltpu.VMEM((2, tm, LANES), F32)],
        compiler_params=pltpu.CompilerParams(dimension_semantics=("arbitrary",),
                                             vmem_limit_bytes=VMEM_LIMIT_INPROJ_BYTES),
        name="inproj",
    )(xf, pos, mod, xf, pos, mod, norm_w.reshape(1, D), inv, sgn, w_bf16)


def _ret_kernel(q_ref, k_ref, v_ref, g_ref, dmask_ref, xi_ref, zeta_ref, cd_ref, o_ref, state_ref):
    @pl.when(pl.program_id(1) == 0)
    def _():
        state_ref[...] = jnp.zeros(state_ref.shape, F32)

    for h in range(RET_HEADS):
        qk = slice(h * RET_DK, (h + 1) * RET_DK)
        vv = slice(h * RET_DV, (h + 1) * RET_DV)
        q = q_ref[:, qk]
        k = k_ref[:, qk]
        v = v_ref[:, vv]
        st = state_ref[h]
        s = _dot_nt(q, k) * dmask_ref[h]
        qx = (q.astype(F32) * xi_ref[h]).astype(BF16)
        o = _dot(s.astype(BF16), v) + _dot(qx, st.astype(BF16))
        kz_t = (k.astype(F32) * zeta_ref[h]).T.astype(BF16)
        state_ref[h] = cd_ref[h] * st + _dot(kz_t, v)
        g = g_ref[:, vv].astype(F32)
        o_ref[:, vv] = (_rms(o) * (g * _sigmoid(g))).astype(BF16)


def _retention(proj, batch, seq):
    T = proj.shape[0]
    C = RET_CHUNK
    H = RET_HEADS
    nc = seq // C
    gamma = 1.0 - jnp.exp2(-5.0 - jnp.arange(H, dtype=F32))
    log_g = jnp.log(gamma)
    idx = jnp.arange(C, dtype=F32)
    rel = idx[:, None] - idx[None, :]
    dmask = jnp.where(rel >= 0, jnp.exp(log_g[:, None, None] * jnp.maximum(rel, 0.0)), 0.0)
    zeta = jnp.exp(log_g[:, None] * (C - 1 - idx))
    xi = jnp.exp(log_g[:, None] * (idx + 1))
    cd = jnp.exp(log_g * C)
    zeta_b = jnp.broadcast_to(zeta[:, :, None], (H, C, RET_DK))
    xi_b = jnp.broadcast_to(xi[:, :, None], (H, C, RET_DK))
    cd_b = jnp.broadcast_to(cd[:, None, None], (H, 1, RET_DV))
    wq = H * RET_DK
    wv = H * RET_DV
    row = lambda b, n: b * nc + n
    const3 = lambda b, n: (0, 0, 0)
    return pl.pallas_call(
        _ret_kernel,
        grid=(batch, nc),
        in_specs=[pl.BlockSpec((C, wq), lambda b, n: (row(b, n), OFF_RQ // wq)),
                  pl.BlockSpec((C, wq), lambda b, n: (row(b, n), OFF_RK // wq)),
                  pl.BlockSpec((C, wv), lambda b, n: (row(b, n), OFF_RV // wv)),
                  pl.BlockSpec((C, wv), lambda b, n: (row(b, n), OFF_RG // wv)),
                  pl.BlockSpec((H, C, C), const3),
                  pl.BlockSpec((H, C, RET_DK), const3),
                  pl.BlockSpec((H, C, RET_DK), const3),
                  pl.BlockSpec((H, 1, RET_DV), const3)],
        out_specs=pl.BlockSpec((C, wv), lambda b, n: (row(b, n), 0)),
        out_shape=jax.ShapeDtypeStruct((T, wv), BF16),
        scratch_shapes=[pltpu.VMEM((H, RET_DK, RET_DV), F32)],
        compiler_params=_params("parallel", "arbitrary"),
        name="retention",
    )(proj, proj, proj, proj, dmask, xi_b, zeta_b, cd_b)


def _transpose_bf16(a):
    return a.astype(F32).T.astype(BF16)


def _attn_kernel(lam_ref, dn_ref, q1_ref, q2_ref, k1_ref, k2_ref, v_ref, o_ref,
                 vt_ref, qt_ref, s_ref, pm_ref, m_ref, l_ref, acc_ref, *, bq, bk, lambda_init):
    i = pl.program_id(2)
    n_sub = vt_ref.shape[0]

    @pl.when(i == 0)
    def _():
        for t in range(n_sub):
            vt_ref[t] = _transpose_bf16(v_ref[t * bk:(t + 1) * bk, :])

    m_ref[...] = jnp.full(m_ref.shape, NEG_BIG, F32)
    l_ref[...] = jnp.zeros(l_ref.shape, F32)
    acc_ref[...] = jnp.zeros(acc_ref.shape, F32)
    qt_ref[0] = _transpose_bf16(q1_ref[...])
    qt_ref[1] = _transpose_bf16(q2_ref[...])
    k_refs = (k1_ref, k2_ref)

    def score(t, slot, diagonal_offset=None):
        start = pl.multiple_of(t * bk, bk)
        for c in range(2):
            st = _dot(k_refs[c][pl.ds(start, bk), :], qt_ref[c])
            if diagonal_offset is not None:
                key = lax.broadcasted_iota(jnp.int32, (bk, bq), 0) + diagonal_offset
                qry = lax.broadcasted_iota(jnp.int32, (bk, bq), 1)
                st = jnp.where(key <= qry, st, NEG_BIG)
            s_ref[slot, c] = st
            pm_ref[slot, c] = jnp.max(st, axis=0, keepdims=True)

    def accumulate(t, slot):
        vt = vt_ref[t]
        for c in range(2):
            m_prev = m_ref[c]
            m_new = jnp.maximum(m_prev, pm_ref[slot, c])
            p = jnp.exp2(s_ref[slot, c] - m_new)
            alpha = jnp.exp2(m_prev - m_new)
            l_ref[c] = alpha * l_ref[c] + jnp.sum(p, axis=0, keepdims=True)
            acc_ref[c] = alpha * acc_ref[c] + _dot(vt, p.astype(BF16))
            m_ref[c] = m_new

    def pair(jj, next_is_diagonal):
        t = 2 * jj
        score(t + 1, 1)
        accumulate(t, 0)
        score(t + 2, 0, 0 if next_is_diagonal else None)
        accumulate(t + 1, 1)

    pl.when(i == 0)(lambda: score(0, 0, 0))
    pl.when(i > 0)(lambda: score(0, 0))

    def body(jj, carry):
        pair(jj, False)
        return carry

    lax.fori_loop(0, i - 1, body, 0)
    pl.when(i > 0)(lambda: pair(i - 1, True))
    score(2 * i + 1, 1, bk)
    accumulate(2 * i, 0)
    accumulate(2 * i + 1, 1)

    lam4 = lam_ref[...]
    lam = (jnp.exp(jnp.sum(lam4[0:1] * lam4[1:2], axis=1, keepdims=True))
           - jnp.exp(jnp.sum(lam4[2:3] * lam4[3:4], axis=1, keepdims=True)) + lambda_init)
    ot = acc_ref[0] * (1.0 / l_ref[0]) - lam * (acc_ref[1] * (1.0 / l_ref[1]))
    ot = ot * lax.rsqrt(jnp.mean(ot * ot, axis=0, keepdims=True) + EPS)
    o_ref[...] = (ot.T * dn_ref[...] * (1.0 - lambda_init)).astype(BF16)


def _diff_attention(proj, lam4, diff_norm, lambda_init, batch, seq):
    T = proj.shape[0]
    H = DIFF_HEADS
    bq = min(BQ_ATTN, seq)
    bk = bq // 2
    nq = seq // bq
    q0 = OFF_DQ // DIFF_DK
    k0 = OFF_DK // DIFF_DK
    v0 = OFF_DV // DIFF_DV
    return pl.pallas_call(
        functools.partial(_attn_kernel, bq=bq, bk=bk, lambda_init=lambda_init),
        grid=(batch, H, nq),
        in_specs=[pl.BlockSpec((4, DIFF_DK), lambda b, h, i: (0, 0)),
                  pl.BlockSpec((1, DIFF_DV), lambda b, h, i: (0, 0)),
                  pl.BlockSpec((bq, DIFF_DK), lambda b, h, i: (b * nq + i, q0 + 2 * h)),
                  pl.BlockSpec((bq, DIFF_DK), lambda b, h, i: (b * nq + i, q0 + 2 * h + 1)),
                  pl.BlockSpec((seq, DIFF_DK), lambda b, h, i: (b, k0 + 2 * h)),
                  pl.BlockSpec((seq, DIFF_DK), lambda b, h, i: (b, k0 + 2 * h + 1)),
                  pl.BlockSpec((seq, DIFF_DV), lambda b, h, i: (b, v0 + h))],
        out_specs=pl.BlockSpec((bq, DIFF_DV), lambda b, h, i: (b * nq + i, h)),
        out_shape=jax.ShapeDtypeStruct((T, H * DIFF_DV), BF16),
        scratch_shapes=[pltpu.VMEM((seq // bk, DIFF_DV, bk), BF16),
                        pltpu.VMEM((2, DIFF_DK, bq), BF16),
                        pltpu.VMEM((2, 2, bk, bq), F32),
                        pltpu.VMEM((2, 2, 1, bq), F32),
                        pltpu.VMEM((2, 1, bq), F32),
                        pltpu.VMEM((2, 1, bq), F32),
                        pltpu.VMEM((2, DIFF_DV, bq), F32)],
        compiler_params=_params("parallel", "parallel", "arbitrary"),
        name="diff_attn",
    )(lam4, diff_norm.reshape(1, DIFF_DV), proj, proj, proj, proj, proj)


def _post_kernel(ro_ref, do_ref, gr_ref, gd_ref, x_ref, mod_ref, wr_ref, wd_ref, wo_ref, nf_ref,
                 wrt_ref, brt_ref, x1_ref, h2_ref, route_ref, cnt_ref, carry_ref, *, tm):
    @pl.when(pl.program_id(0) == 0)
    def _():
        carry_ref[...] = jnp.zeros(carry_ref.shape, F32)

    mod = mod_ref[0]
    ret_out = _dot(ro_ref[...], wr_ref[...])
    diff_out = _dot(do_ref[...], wd_ref[...])
    merged = (_sigmoid(gr_ref[...].astype(F32)) * ret_out
              + _sigmoid(gd_ref[...].astype(F32)) * diff_out)
    x1 = x_ref[...] + mod[2:3] * _dot(merged.astype(BF16), wo_ref[...])
    x1_ref[...] = x1
    h2 = _rms(x1) * nf_ref[...] * (1.0 + mod[4:5]) + mod[3:4]
    h2_ref[...] = h2

    lg = _dot(h2.astype(BF16), wrt_ref[...]) + brt_ref[...]
    lane = lax.broadcasted_iota(jnp.int32, lg.shape, 1)
    far = jnp.int32(LANES)

    def top1(vals):
        best = jnp.max(vals, axis=1, keepdims=True)
        return best, jnp.min(jnp.where(vals == best, lane, far), axis=1, keepdims=True)

    is_group = (lane >= N_EXPERTS) & (lane < N_EXPERTS + N_GROUPS)
    g_best, g_lane = top1(jnp.where(is_group, lg, NEG_BIG))
    g_sum = jnp.sum(jnp.where(is_group, jnp.exp(lg - g_best), 0.0), axis=1, keepdims=True)
    g_top = 1.0 / g_sum
    first = (g_lane - N_EXPERTS) * EXPERTS_PER_GROUP
    in_group = (lane >= first) & (lane < first + EXPERTS_PER_GROUP)
    el = jnp.where(in_group, lg, NEG_BIG)
    e_a, i_a = top1(el)
    e_b, i_b = top1(jnp.where(lane == i_a, NEG_BIG, el))
    t = jnp.exp(e_b - e_a)
    w_a = g_top / (1.0 + t)
    w_b = g_top * t / (1.0 + t)

    hot_a = lane == i_a
    hot_b = lane == i_b
    hot = jnp.where(hot_a | hot_b, 1.0, 0.0)
    r_i = lax.broadcasted_iota(jnp.int32, (tm, tm), 0)
    c_i = lax.broadcasted_iota(jnp.int32, (tm, tm), 1)
    lower = jnp.where(c_i < r_i, 1.0, 0.0).astype(BF16)
    before = _dot(lower, hot.astype(BF16)) + carry_ref[...]
    rank_a = jnp.sum(jnp.where(hot_a, before, 0.0), axis=1, keepdims=True)
    rank_b = jnp.sum(jnp.where(hot_b, before, 0.0), axis=1, keepdims=True)
    total = carry_ref[...] + jnp.sum(hot, axis=0, keepdims=True)
    carry_ref[...] = total
    cnt_ref[...] = total

    fields = (i_a.astype(F32), i_b.astype(F32), w_a, w_b, rank_a, rank_b)
    route = jnp.zeros(lg.shape, F32)
    for n, f in enumerate(fields):
        route = jnp.where(lane == n, f, route)
    route_ref[...] = route


def _post_mixer(ro, do, proj, xf, mod, w_ret_o, w_diff_o, w_out, norm_ffn, w_rt, b_rt, seq):
    T, D = xf.shape
    tm = min(TM_POST, seq)
    per_batch = seq // tm
    row = lambda i: (i, 0)
    const = lambda i: (0, 0)
    return pl.pallas_call(
        functools.partial(_post_kernel, tm=tm),
        grid=(T // tm,),
        in_specs=[pl.BlockSpec((tm, D), row),
                  pl.BlockSpec((tm, D), row),
                  pl.BlockSpec((tm, D), lambda i: (i, OFF_GR // D)),
                  pl.BlockSpec((tm, D), lambda i: (i, OFF_GD // D)),
                  pl.BlockSpec((tm, D), row),
                  pl.BlockSpec((1, 6, D), lambda i: (i // per_batch, 0, 0)),
                  pl.BlockSpec((D, D), const),
                  pl.BlockSpec((D, D), const),
                  pl.BlockSpec((D, D), const),
                  pl.BlockSpec((1, D), const),
                  pl.BlockSpec((D, LANES), const),
                  pl.BlockSpec((1, LANES), const)],
        out_specs=[pl.BlockSpec((tm, D), row),
                   pl.BlockSpec((tm, D), row),
                   pl.BlockSpec((tm, LANES), row),
                   pl.BlockSpec((1, LANES), const)],
        out_shape=[jax.ShapeDtypeStruct((T, D), F32),
                   jax.ShapeDtypeStruct((T, D), F32),
                   jax.ShapeDtypeStruct((T, LANES), F32),
                   jax.ShapeDtypeStruct((1, LANES), F32)],
        scratch_shapes=[pltpu.VMEM((1, LANES), F32)],
        compiler_params=_params("arbitrary"),
        name="post_mixer",
    )(ro, do, proj, proj, xf, mod, w_ret_o, w_diff_o, w_out, norm_ffn.reshape(1, D), w_rt, b_rt)


def _row_copy(src, s, dst, d, sem):
    return pltpu.make_async_copy(src.at[pl.ds(s, 1), :], dst.at[pl.ds(d, 1), :], sem)


def _scatter_kernel(zb_ref, dest_ref, h_ref, o_hbm, zero_ref, zsem, sem, *, ts, tb):
    i = pl.program_id(0)

    @pl.when(i == 0)
    def _():
        zero_ref[...] = jnp.zeros(zero_ref.shape, F32)

        def zero_copy(n):
            start = pl.multiple_of(zb_ref[n] * tb, tb)
            return pltpu.make_async_copy(zero_ref, o_hbm.at[pl.ds(start, tb), :], zsem)

        for n in range(zb_ref.shape[0]):
            pl.when(zb_ref[n] >= 0)(lambda n=n: zero_copy(n).start())
        for n in range(zb_ref.shape[0]):
            pl.when(zb_ref[n] >= 0)(lambda n=n: zero_copy(n).wait())

    def issue(r, carry):
        for k in range(2):
            _row_copy(h_ref, r, o_hbm, dest_ref[0, 0, 2 * r + k], sem).start()
        return carry

    lax.fori_loop(0, ts, issue, 0, unroll=8)
    pltpu.make_async_copy(o_hbm.at[pl.ds(0, 2 * ts), :], o_hbm.at[pl.ds(0, 2 * ts), :], sem).wait()


def _moe_scatter(h2, dest, zero_blocks, n_rows):
    T, D = h2.shape
    ts = min(TS_SCATTER, T)
    tb = TB_MOE
    grid_spec = pltpu.PrefetchScalarGridSpec(
        num_scalar_prefetch=1,
        grid=(T // ts,),
        in_specs=[pl.BlockSpec((1, 1, 2 * ts), lambda i, zb: (i, 0, 0), memory_space=pltpu.SMEM),
                  pl.BlockSpec((ts, D), lambda i, zb: (i, 0))],
        out_specs=pl.BlockSpec(memory_space=pl.ANY),
        scratch_shapes=[pltpu.VMEM((tb, D), F32), pltpu.SemaphoreType.DMA(()), pltpu.SemaphoreType.DMA(())],
    )
    return pl.pallas_call(
        functools.partial(_scatter_kernel, ts=ts, tb=tb),
        grid_spec=grid_spec,
        out_shape=jax.ShapeDtypeStruct((n_rows, D), F32),
        compiler_params=_params("arbitrary"),
        name="moe_scatter",
    )(zero_blocks, dest.reshape(T // ts, 1, 2 * ts), h2)


def _expert_kernel(be_ref, nv_ref, x_ref, w1_ref, w3_ref, w2_ref, o_ref, w1b_ref, w3b_ref, w2b_ref):
    n = pl.program_id(0)
    nv = nv_ref[n]
    new_expert = (n == 0) | (be_ref[n] != be_ref[jnp.maximum(n - 1, 0)])

    @pl.when(new_expert & (nv > 0))
    def _():
        w1b_ref[...] = w1_ref[0].astype(BF16)
        w3b_ref[...] = w3_ref[0].astype(BF16)
        w2b_ref[...] = w2_ref[0].astype(BF16)

    @pl.when(nv > 0)
    def _():
        x = x_ref[...].astype(BF16)
        g = _dot(x, w1b_ref[...])
        u = _dot(x, w3b_ref[...])
        a = (g * _sigmoid(g) * u).astype(BF16)
        o_ref[...] = _dot(a, w2b_ref[...])

    @pl.when(nv == 0)
    def _():
        o_ref[...] = jnp.zeros(o_ref.shape, F32)


def _moe_experts(h_pad, block_e, block_nv, w1, w3, w2):
    P, D = h_pad.shape
    tb = TB_MOE
    wmap = lambda n, be, nv: (be[n], 0, 0)
    grid_spec = pltpu.PrefetchScalarGridSpec(
        num_scalar_prefetch=2,
        grid=(P // tb,),
        in_specs=[pl.BlockSpec((tb, D), lambda n, be, nv: (n, 0)),
                  pl.BlockSpec((1, D, D_EXPERT), wmap),
                  pl.BlockSpec((1, D, D_EXPERT), wmap),
                  pl.BlockSpec((1, D_EXPERT, D), wmap)],
        out_specs=pl.BlockSpec((tb, D), lambda n, be, nv: (n, 0)),
        scratch_shapes=[pltpu.VMEM((D, D_EXPERT), BF16), pltpu.VMEM((D, D_EXPERT), BF16),
                        pltpu.VMEM((D_EXPERT, D), BF16)],
    )
    return pl.pallas_call(
        _expert_kernel,
        grid_spec=grid_spec,
        out_shape=jax.ShapeDtypeStruct((P, D), F32),
        compiler_params=_params("arbitrary"),
        name="moe_experts",
    )(block_e, block_nv, h_pad, w1, w3, w2)


def _combine_kernel(dcur_ref, dnext_ref, x1_ref, route_ref, mod_ref, nw_ref, y_hbm, o_ref, buf, sem, *, tc, final):
    i = pl.program_id(0)
    slot = i % 2

    def gather(d_ref, s):
        def issue(r, carry):
            for k in range(2):
                _row_copy(y_hbm, d_ref[0, 0, 2 * r + k], buf.at[s], k * tc + r, sem.at[s]).start()
            return carry

        lax.fori_loop(0, tc, issue, 0, unroll=8)

    pl.when(i == 0)(lambda: gather(dcur_ref, 0))
    pl.when(i + 1 < pl.num_programs(0))(lambda: gather(dnext_ref, 1 - slot))
    pltpu.make_async_copy(y_hbm.at[pl.ds(0, 2 * tc), :], buf.at[slot], sem.at[slot]).wait()
    route = route_ref[...]
    y = route[:, 2:3] * buf[slot, 0:tc, :] + route[:, 3:4] * buf[slot, tc:2 * tc, :]
    x2 = x1_ref[...] + mod_ref[0][5:6] * y
    o_ref[...] = _rms(x2) * nw_ref[...] if final else x2


def _moe_combine(y_pad, dest, x1, route, mod, norm_w, seq, final):
    T, D = x1.shape
    tc = min(TC_COMBINE, seq)
    per_batch = seq // tc
    row = lambda i: (i, 0)
    n = T // tc
    dest3 = dest.reshape(n, 1, 2 * tc)
    return pl.pallas_call(
        functools.partial(_combine_kernel, tc=tc, final=final),
        grid=(n,),
        in_specs=[pl.BlockSpec((1, 1, 2 * tc), lambda i: (i, 0, 0), memory_space=pltpu.SMEM),
                  pl.BlockSpec((1, 1, 2 * tc), lambda i: (jnp.minimum(i + 1, n - 1), 0, 0),
                               memory_space=pltpu.SMEM),
                  pl.BlockSpec((tc, D), row),
                  pl.BlockSpec((tc, LANES), row),
                  pl.BlockSpec((1, 6, D), lambda i: (i // per_batch, 0, 0)),
                  pl.BlockSpec((1, D), lambda i: (0, 0)),
                  pl.BlockSpec(memory_space=pl.ANY)],
        out_specs=pl.BlockSpec((tc, D), row),
        out_shape=jax.ShapeDtypeStruct((T, D), F32),
        scratch_shapes=[pltpu.VMEM((2, 2 * tc, D), F32), pltpu.SemaphoreType.DMA((2,))],
        compiler_params=_params("arbitrary"),
        name="moe_combine",
    )(dest3, dest3, x1, route, mod, norm_w.reshape(1, D), y_pad)


def _routing_tables(route, counts, n_blocks):
    tb = TB_MOE
    expert = route[:, 0:2].astype(jnp.int32)
    rank = route[:, 4:6].astype(jnp.int32)
    cnt = counts[0, :N_EXPERTS].astype(jnp.int32)
    nblk = (cnt + tb - 1) // tb
    blk_end = jnp.cumsum(nblk)
    blk_start = blk_end - nblk
    hot = expert[:, :, None] == jnp.arange(N_EXPERTS, dtype=jnp.int32)
    dest = jnp.sum(jnp.where(hot, blk_start * tb, 0), axis=-1) + rank
    blocks = jnp.arange(n_blocks, dtype=jnp.int32)
    block_e = jnp.minimum(jnp.sum(blocks[:, None] >= blk_end[None, :], axis=1), N_EXPERTS - 1).astype(jnp.int32)
    left = cnt[block_e] - (blocks - blk_start[block_e]) * tb
    block_nv = jnp.where(blocks < blk_end[-1], jnp.clip(left, 0, tb), 0).astype(jnp.int32)
    tail = blk_end[-1] + jnp.arange(N_EXPERTS, dtype=jnp.int32)
    zero_blocks = jnp.concatenate([jnp.where(nblk > 0, blk_end - 1, -1),
                                   jnp.where(tail < n_blocks, tail, -1)]).astype(jnp.int32)
    return dest.astype(jnp.int32), block_e, block_nv, zero_blocks


def kernel(x, c, positions, w_ada, b_ada, norm_mix, w_in, w_ret_o, w_diff_o, lam_q1, lam_k1, lam_q2, lam_k2,
           diff_norm, w_out, norm_ffn, w_router_group, b_router_group, w_router_expert, b_router_expert,
           w_exp_gate, w_exp_up, w_exp_down, norm_final):
    B, S, D = x.shape
    T = B * S
    depth = w_ada.shape[0]
    assert D_IN == w_in.shape[2] and S % RET_CHUNK == 0
    xf = x.reshape(T, D)
    pos = positions.reshape(T, 1)
    n_blocks = (2 * T) // TB_MOE + N_EXPERTS
    for l in range(depth):
        lambda_init = 0.8 - 0.6 * math.exp(-0.3 * l)
        mod = _adaln(c, w_ada[l], b_ada[l]).reshape(B, 6, D)
        proj = _inproj(xf, pos, mod, norm_mix[l], w_in[l].astype(BF16), S)
        ro = _retention(proj, B, S)
        lam4 = jnp.stack([lam_q1[l], lam_k1[l], lam_q2[l], lam_k2[l]]).astype(F32)
        do = _diff_attention(proj, lam4, diff_norm[l].astype(F32), lambda_init, B, S)

        pad = LANES - N_EXPERTS - N_GROUPS
        w_rt = jnp.concatenate([w_router_expert[l], w_router_group[l], jnp.zeros((D, pad), F32)], axis=1)
        b_rt = jnp.concatenate([b_router_expert[l], b_router_group[l], jnp.zeros((pad,), F32)]).reshape(1, LANES)
        x1, h2, route, counts = _post_mixer(
            ro, do, proj, xf, mod, w_ret_o[l].astype(BF16), w_diff_o[l].astype(BF16), w_out[l].astype(BF16),
            norm_ffn[l], w_rt.astype(BF16), b_rt, S)

        dest, block_e, block_nv, zero_blocks = _routing_tables(route, counts, n_blocks)
        h_pad = _moe_scatter(h2, dest.reshape(-1), zero_blocks, n_blocks * TB_MOE)
        y_pad = _moe_experts(h_pad, block_e, block_nv, w_exp_gate[l], w_exp_up[l], w_exp_down[l])
        xf = _moe_combine(y_pad, dest.reshape(-1), x1, route, mod, norm_final, S, l == depth - 1)
    return xf.reshape(B, S, D)
```

```python
import functools
import math

import jax
import jax.numpy as jnp
from jax import lax
from jax.experimental import pallas as pl
from jax.experimental.pallas import tpu as pltpu

F32 = jnp.float32
BF16 = jnp.bfloat16

EPS = 1e-6
ROPE_THETA = 10000.0
LANES = 128
RET_HEADS, RET_DK, RET_DV, RET_CHUNK = 4, 128, 256, 128
DIFF_HEADS, DIFF_DK, DIFF_DV = 4, 128, 256
N_GROUPS, EXPERTS_PER_GROUP, N_EXPERTS, D_EXPERT = 4, 8, 32, 512
OFF_RQ, OFF_RK, OFF_RV, OFF_RG, OFF_DQ, OFF_DK, OFF_DV, OFF_GR, OFF_GD, D_IN = (
    0, 512, 1024, 2048, 3072, 4096, 5120, 6144, 7168, 8192)
NEG_BIG = -1e30
VMEM_LIMIT_BYTES = 48 * 1024 * 1024
VMEM_LIMIT_INPROJ_BYTES = 56 * 1024 * 1024

TM_INPROJ, TN_INPROJ = 512, 1024
TR_RETENTION = 512
BQ_ATTN = 512
TM_POST = 512
TB_MOE = 256
TS_SCATTER = 256
TC_COMBINE = 256


def _params(*sem):
    return pltpu.CompilerParams(dimension_semantics=sem, vmem_limit_bytes=VMEM_LIMIT_BYTES)


def _sigmoid(v):
    return 1.0 / (1.0 + jnp.exp(-v))


def _rms(v):
    return v * lax.rsqrt(jnp.mean(v * v, axis=-1, keepdims=True) + EPS)


def _dot(a, b):
    return jnp.dot(a, b, preferred_element_type=F32)


def _dot_nt(a, b):
    return lax.dot_general(a, b, (((1,), (1,)), ((), ())), preferred_element_type=F32)


def _ada_kernel(c_ref, w_ref, b_ref, o_ref):
    c = c_ref[...]
    a = (c * _sigmoid(c)).astype(BF16)
    o_ref[...] = _dot(a, w_ref[...].astype(BF16)) + b_ref[...]


def _adaln(c, w, b):
    B, D = c.shape
    n = w.shape[1] // D
    return pl.pallas_call(
        _ada_kernel,
        grid=(n,),
        in_specs=[pl.BlockSpec((B, D), lambda j: (0, 0)),
                  pl.BlockSpec((D, D), lambda j: (0, j)),
                  pl.BlockSpec((1, D), lambda j: (0, j))],
        out_specs=pl.BlockSpec((B, D), lambda j: (0, j)),
        out_shape=jax.ShapeDtypeStruct((B, n * D), F32),
        compiler_params=_params("parallel"),
        name="adaln",
    )(c, w, b.reshape(1, -1))


ROPE_GROUPS = ((OFF_RQ, OFF_RK, 1.0), (OFF_RK, OFF_RV, RET_DK ** -0.5),
               (OFF_DQ, OFF_DK, DIFF_DK ** -0.5 * math.log2(math.e)), (OFF_DK, OFF_DV, 1.0))


def _rope_scale(col):
    for lo, hi, scale in ROPE_GROUPS:
        if lo <= col < hi:
            return scale
    return None


def _inproj_kernel(x0_ref, pos0_ref, mod0_ref, xn_ref, posn_ref, modn_ref, nw_ref, inv_ref, sgn_ref,
                   w_ref, o_ref, h_ref, cos_ref, sin_ref, *, tn):
    i = pl.program_id(0)

    def prepare(x, pos, mod, slot):
        h = _rms(x) * nw_ref[...] * (1.0 + mod[1:2]) + mod[0:1]
        h_ref[slot] = h.astype(BF16)
        ang = pos.astype(F32) * inv_ref[...]
        cos_ref[slot] = jnp.cos(ang)
        sin_ref[slot] = jnp.sin(ang) * sgn_ref[...]

    @pl.when(i == 0)
    def _():
        prepare(x0_ref[...], pos0_ref[...], mod0_ref[0], 0)

    cur = i % 2
    for g in range(o_ref.shape[1] // tn):
        acc = _dot(h_ref[cur], w_ref[:, g * tn:(g + 1) * tn])
        for k in range(tn // LANES):
            first = g * tn + k * LANES
            a = acc[:, k * LANES:(k + 1) * LANES]
            scale = _rope_scale(first)
            if scale is not None:
                a = a * cos_ref[cur] + pltpu.roll(a, LANES // 2, 1) * sin_ref[cur]
                if scale != 1.0:
                    a = a * scale
            o_ref[:, first:first + LANES] = a.astype(BF16)

    prepare(xn_ref[...], posn_ref[...], modn_ref[0], 1 - cur)


def _inproj(xf, pos, mod, norm_w, w_bf16, seq):
    T, D = xf.shape
    tm = min(TM_INPROJ, seq)
    tn = TN_INPROJ
    half = LANES // 2
    inv = ROPE_THETA ** (-jnp.arange(0, LANES, 2, dtype=F32) / LANES)
    inv = jnp.concatenate([inv, inv]).reshape(1, LANES)
    sgn = jnp.concatenate([-jnp.ones((half,), F32), jnp.ones((half,), F32)]).reshape(1, LANES)
    per_batch = seq // tm
    n_m = T // tm
    nxt = lambda i: jnp.minimum(i + 1, n_m - 1)
    once = dict(pipeline_mode=pl.Buffered(1))
    return pl.pallas_call(
        functools.partial(_inproj_kernel, tn=tn),
        grid=(n_m,),
        in_specs=[pl.BlockSpec((tm, D), lambda i: (0, 0), **once),
                  pl.BlockSpec((tm, 1), lambda i: (0, 0), **once),
                  pl.BlockSpec((1, 6, D), lambda i: (0, 0, 0), **once),
                  pl.BlockSpec((tm, D), lambda i: (nxt(i), 0)),
                  pl.BlockSpec((tm, 1), lambda i: (nxt(i), 0)),
                  pl.BlockSpec((1, 6, D), lambda i: (nxt(i) // per_batch, 0, 0)),
                  pl.BlockSpec((1, D), lambda i: (0, 0), **once),
                  pl.BlockSpec((1, LANES), lambda i: (0, 0), **once),
                  pl.BlockSpec((1, LANES), lambda i: (0, 0), **once),
                  pl.BlockSpec((D, D_IN), lambda i: (0, 0), **once)],
        out_specs=pl.BlockSpec((tm, D_IN), lambda i: (i, 0)),
        out_shape=jax.ShapeDtypeStruct((T, D_IN), BF16),
        scratch_shapes=[pltpu.VMEM((2, tm, D), BF16),
                        pltpu.VMEM((2, tm, LANES), F32),
                        pltpu.VMEM((2, tm, LANES), F32)],
        compiler_params=pltpu.CompilerParams(dimension_semantics=("arbitrary",),
                                             vmem_limit_bytes=VMEM_LIMIT_INPROJ_BYTES),
        name="inproj",
    )(xf, pos, mod, xf, pos, mod, norm_w.reshape(1, D), inv, sgn, w_bf16)


def _ret_kernel(q_ref, k_ref, v_ref, g_ref, dmask_ref, xi_ref, zeta_ref, cd_ref, o_ref, state_ref):
    @pl.when(pl.program_id(1) == 0)
    def _():
        state_ref[...] = jnp.zeros(state_ref.shape, F32)

    C = RET_CHUNK
    for h in range(RET_HEADS):
        qk = slice(h * RET_DK, (h + 1) * RET_DK)
        vv = slice(h * RET_DV, (h + 1) * RET_DV)
        st = state_ref[h]
        for n in range(q_ref.shape[0] // C):
            rows = slice(n * C, (n + 1) * C)
            q = q_ref[rows, qk]
            k = k_ref[rows, qk]
            v = v_ref[rows, vv]
            s = _dot_nt(q, k) * dmask_ref[h]
            qx = (q.astype(F32) * xi_ref[h]).astype(BF16)
            o = _dot(s.astype(BF16), v) + _dot(qx, st.astype(BF16))
            kz_t = (k.astype(F32) * zeta_ref[h]).T.astype(BF16)
            st = cd_ref[h] * st + _dot(kz_t, v)
            g = g_ref[rows, vv].astype(F32)
            o_ref[rows, vv] = (_rms(o) * (g * _sigmoid(g))).astype(BF16)
        state_ref[h] = st


def _retention(proj, batch, seq):
    T = proj.shape[0]
    C = RET_CHUNK
    H = RET_HEADS
    nc = seq // C
    gamma = 1.0 - jnp.exp2(-5.0 - jnp.arange(H, dtype=F32))
    log_g = jnp.log(gamma)
    idx = jnp.arange(C, dtype=F32)
    rel = idx[:, None] - idx[None, :]
    dmask = jnp.where(rel >= 0, jnp.exp(log_g[:, None, None] * jnp.maximum(rel, 0.0)), 0.0)
    zeta = jnp.exp(log_g[:, None] * (C - 1 - idx))
    xi = jnp.exp(log_g[:, None] * (idx + 1))
    cd = jnp.exp(log_g * C)
    zeta_b = jnp.broadcast_to(zeta[:, :, None], (H, C, RET_DK))
    xi_b = jnp.broadcast_to(xi[:, :, None], (H, C, RET_DK))
    cd_b = jnp.broadcast_to(cd[:, None, None], (H, 1, RET_DV))
    wq = H * RET_DK
    wv = H * RET_DV
    tr = min(TR_RETENTION, seq)
    ns = seq // tr
    row = lambda b, n: b * ns + n
    const3 = lambda b, n: (0, 0, 0)
    return pl.pallas_call(
        _ret_kernel,
        grid=(batch, ns),
        in_specs=[pl.BlockSpec((tr, wq), lambda b, n: (row(b, n), OFF_RQ // wq)),
                  pl.BlockSpec((tr, wq), lambda b, n: (row(b, n), OFF_RK // wq)),
                  pl.BlockSpec((tr, wv), lambda b, n: (row(b, n), OFF_RV // wv)),
                  pl.BlockSpec((tr, wv), lambda b, n: (row(b, n), OFF_RG // wv)),
                  pl.BlockSpec((H, C, C), const3),
                  pl.BlockSpec((H, C, RET_DK), const3),
                  pl.BlockSpec((H, C, RET_DK), const3),
                  pl.BlockSpec((H, 1, RET_DV), const3)],
        out_specs=pl.BlockSpec((tr, wv), lambda b, n: (row(b, n), 0)),
        out_shape=jax.ShapeDtypeStruct((T, wv), BF16),
        scratch_shapes=[pltpu.VMEM((H, RET_DK, RET_DV), F32)],
        compiler_params=_params("parallel", "arbitrary"),
        name="retention",
    )(proj, proj, proj, proj, dmask, xi_b, zeta_b, cd_b)


def _transpose_bf16(a):
    return a.astype(F32).T.astype(BF16)


def _attn_kernel(lam_ref, dn_ref, q1_ref, q2_ref, k1_ref, k2_ref, v_ref, o_ref,
                 vt_ref, qt_ref, s_ref, pm_ref, m_ref, l_ref, acc_ref, *, bq, bk, lambda_init):
    i = pl.program_id(2)
    n_sub = vt_ref.shape[0]

    @pl.when(i == 0)
    def _():
        for t in range(n_sub):
            vt_ref[t] = _transpose_bf16(v_ref[t * bk:(t + 1) * bk, :])

    m_ref[...] = jnp.full(m_ref.shape, NEG_BIG, F32)
    l_ref[...] = jnp.zeros(l_ref.shape, F32)
    acc_ref[...] = jnp.zeros(acc_ref.shape, F32)
    qt_ref[0] = _transpose_bf16(q1_ref[...])
    qt_ref[1] = _transpose_bf16(q2_ref[...])
    k_refs = (k1_ref, k2_ref)

    def score(t, slot, diagonal_offset=None):
        start = pl.multiple_of(t * bk, bk)
        for c in range(2):
            st = _dot(k_refs[c][pl.ds(start, bk), :], qt_ref[c])
            if diagonal_offset is not None:
                key = lax.broadcasted_iota(jnp.int32, (bk, bq), 0) + diagonal_offset
                qry = lax.broadcasted_iota(jnp.int32, (bk, bq), 1)
                st = jnp.where(key <= qry, st, NEG_BIG)
            s_ref[slot, c] = st
            pm_ref[slot, c] = jnp.max(st, axis=0, keepdims=True)

    def accumulate(t, slot):
        vt = vt_ref[t]
        for c in range(2):
            m_prev = m_ref[c]
            m_new = jnp.maximum(m_prev, pm_ref[slot, c])
            p = jnp.exp2(s_ref[slot, c] - m_new)
            alpha = jnp.exp2(m_prev - m_new)
            l_ref[c] = alpha * l_ref[c] + jnp.sum(p, axis=0, keepdims=True)
            acc_ref[c] = alpha * acc_ref[c] + _dot(vt, p.astype(BF16))
            m_ref[c] = m_new

    def pair(jj, next_is_diagonal):
        t = 2 * jj
        score(t + 1, 1)
        accumulate(t, 0)
        score(t + 2, 0, 0 if next_is_diagonal else None)
        accumulate(t + 1, 1)

    pl.when(i == 0)(lambda: score(0, 0, 0))
    pl.when(i > 0)(lambda: score(0, 0))

    def body(jj, carry):
        pair(jj, False)
        return carry

    lax.fori_loop(0, i - 1, body, 0)
    pl.when(i > 0)(lambda: pair(i - 1, True))
    score(2 * i + 1, 1, bk)
    accumulate(2 * i, 0)
    accumulate(2 * i + 1, 1)

    lam4 = lam_ref[...]
    lam = (jnp.exp(jnp.sum(lam4[0:1] * lam4[1:2], axis=1, keepdims=True))
           - jnp.exp(jnp.sum(lam4[2:3] * lam4[3:4], axis=1, keepdims=True)) + lambda_init)
    ot = acc_ref[0] * (1.0 / l_ref[0]) - lam * (acc_ref[1] * (1.0 / l_ref[1]))
    ot = ot * lax.rsqrt(jnp.mean(ot * ot, axis=0, keepdims=True) + EPS)
    o_ref[...] = (ot.T * dn_ref[...] * (1.0 - lambda_init)).astype(BF16)


def _diff_attention(proj, lam4, diff_norm, lambda_init, batch, seq):
    T = proj.shape[0]
    H = DIFF_HEADS
    bq = min(BQ_ATTN, seq)
    bk = bq // 2
    nq = seq // bq
    q0 = OFF_DQ // DIFF_DK
    k0 = OFF_DK // DIFF_DK
    v0 = OFF_DV // DIFF_DV
    return pl.pallas_call(
        functools.partial(_attn_kernel, bq=bq, bk=bk, lambda_init=lambda_init),
        grid=(batch, H, nq),
        in_specs=[pl.BlockSpec((4, DIFF_DK), lambda b, h, i: (0, 0)),
                  pl.BlockSpec((1, DIFF_DV), lambda b, h, i: (0, 0)),
                  pl.BlockSpec((bq, DIFF_DK), lambda b, h, i: (b * nq + i, q0 + 2 * h)),
                  pl.BlockSpec((bq, DIFF_DK), lambda b, h, i: (b * nq + i, q0 + 2 * h + 1)),
                  pl.BlockSpec((seq, DIFF_DK), lambda b, h, i: (b, k0 + 2 * h)),
                  pl.BlockSpec((seq, DIFF_DK), lambda b, h, i: (b, k0 + 2 * h + 1)),
                  pl.BlockSpec((seq, DIFF_DV), lambda b, h, i: (b, v0 + h))],
        out_specs=pl.BlockSpec((bq, DIFF_DV), lambda b, h, i: (b * nq + i, h)),
        out_shape=jax.ShapeDtypeStruct((T, H * DIFF_DV), BF16),
        scratch_shapes=[pltpu.VMEM((seq // bk, DIFF_DV, bk), BF16),
                        pltpu.VMEM((2, DIFF_DK, bq), BF16),
                        pltpu.VMEM((2, 2, bk, bq), F32),
                        pltpu.VMEM((2, 2, 1, bq), F32),
                        pltpu.VMEM((2, 1, bq), F32),
                        pltpu.VMEM((2, 1, bq), F32),
                        pltpu.VMEM((2, DIFF_DV, bq), F32)],
        compiler_params=_params("parallel", "parallel", "arbitrary"),
        name="diff_attn",
    )(lam4, diff_norm.reshape(1, DIFF_DV), proj, proj, proj, proj, proj)


def _post_kernel(ro_ref, do_ref, gr_ref, gd_ref, x_ref, mod_ref, wr_ref, wd_ref, wo_ref, nf_ref,
                 wrt_ref, brt_ref, x1_ref, h2_ref, route_ref, cnt_ref, carry_ref, *, tm):
    @pl.when(pl.program_id(0) == 0)
    def _():
        carry_ref[...] = jnp.zeros(carry_ref.shape, F32)

    mod = mod_ref[0]
    ret_out = _dot(ro_ref[...], wr_ref[...])
    diff_out = _dot(do_ref[...], wd_ref[...])
    merged = (_sigmoid(gr_ref[...].astype(F32)) * ret_out
              + _sigmoid(gd_ref[...].astype(F32)) * diff_out)
    x1 = x_ref[...] + mod[2:3] * _dot(merged.astype(BF16), wo_ref[...])
    x1_ref[...] = x1
    h2 = _rms(x1) * nf_ref[...] * (1.0 + mod[4:5]) + mod[3:4]
    h2_ref[...] = h2

    lg = _dot(h2.astype(BF16), wrt_ref[...]) + brt_ref[...]
    lane = lax.broadcasted_iota(jnp.int32, lg.shape, 1)
    far = jnp.int32(LANES)

    def top1(vals):
        best = jnp.max(vals, axis=1, keepdims=True)
        return best, jnp.min(jnp.where(vals == best, lane, far), axis=1, keepdims=True)

    is_group = (lane >= N_EXPERTS) & (lane < N_EXPERTS + N_GROUPS)
    g_best, g_lane = top1(jnp.where(is_group, lg, NEG_BIG))
    g_sum = jnp.sum(jnp.where(is_group, jnp.exp(lg - g_best), 0.0), axis=1, keepdims=True)
    g_top = 1.0 / g_sum
    first = (g_lane - N_EXPERTS) * EXPERTS_PER_GROUP
    in_group = (lane >= first) & (lane < first + EXPERTS_PER_GROUP)
    el = jnp.where(in_group, lg, NEG_BIG)
    e_a, i_a = top1(el)
    e_b, i_b = top1(jnp.where(lane == i_a, NEG_BIG, el))
    t = jnp.exp(e_b - e_a)
    w_a = g_top / (1.0 + t)
    w_b = g_top * t / (1.0 + t)

    hot_a = lane == i_a
    hot_b = lane == i_b
    hot = jnp.where(hot_a | hot_b, 1.0, 0.0)
    r_i = lax.broadcasted_iota(jnp.int32, (tm, tm), 0)
    c_i = lax.broadcasted_iota(jnp.int32, (tm, tm), 1)
    lower = jnp.where(c_i < r_i, 1.0, 0.0).astype(BF16)
    before = _dot(lower, hot.astype(BF16)) + carry_ref[...]
    rank_a = jnp.sum(jnp.where(hot_a, before, 0.0), axis=1, keepdims=True)
    rank_b = jnp.sum(jnp.where(hot_b, before, 0.0), axis=1, keepdims=True)
    total = carry_ref[...] + jnp.sum(hot, axis=0, keepdims=True)
    carry_ref[...] = total
    cnt_ref[...] = total

    fields = (i_a.astype(F32), i_b.astype(F32), w_a, w_b, rank_a, rank_b)
    route = jnp.zeros(lg.shape, F32)
    for n, f in enumerate(fields):
        route = jnp.where(lane == n, f, route)
    route_ref[...] = route


def _post_mixer(ro, do, proj, xf, mod, w_ret_o, w_diff_o, w_out, norm_ffn, w_rt, b_rt, seq):
    T, D = xf.shape
    tm = min(TM_POST, seq)
    per_batch = seq // tm
    row = lambda i: (i, 0)
    const = lambda i: (0, 0)
    return pl.pallas_call(
        functools.partial(_post_kernel, tm=tm),
        grid=(T // tm,),
        in_specs=[pl.BlockSpec((tm, D), row),
                  pl.BlockSpec((tm, D), row),
                  pl.BlockSpec((tm, D), lambda i: (i, OFF_GR // D)),
                  pl.BlockSpec((tm, D), lambda i: (i, OFF_GD // D)),
                  pl.BlockSpec((tm, D), row),
                  pl.BlockSpec((1, 6, D), lambda i: (i // per_batch, 0, 0)),
                  pl.BlockSpec((D, D), const),
                  pl.BlockSpec((D, D), const),
                  pl.BlockSpec((D, D), const),
                  pl.BlockSpec((1, D), const),
                  pl.BlockSpec((D, LANES), const),
                  pl.BlockSpec((1, LANES), const)],
        out_specs=[pl.BlockSpec((tm, D), row),
                   pl.BlockSpec((tm, D), row),
                   pl.BlockSpec((tm, LANES), row),
                   pl.BlockSpec((1, LANES), const)],
        out_shape=[jax.ShapeDtypeStruct((T, D), F32),
                   jax.ShapeDtypeStruct((T, D), F32),
                   jax.ShapeDtypeStruct((T, LANES), F32),
                   jax.ShapeDtypeStruct((1, LANES), F32)],
        scratch_shapes=[pltpu.VMEM((1, LANES), F32)],
        compiler_params=_params("arbitrary"),
        name="post_mixer",
    )(ro, do, proj, proj, xf, mod, w_ret_o, w_diff_o, w_out, norm_ffn.reshape(1, D), w_rt, b_rt)


def _row_copy(src, s, dst, d, sem):
    return pltpu.make_async_copy(src.at[pl.ds(s, 1), :], dst.at[pl.ds(d, 1), :], sem)


def _scatter_kernel(zb_ref, dest_ref, h_ref, o_hbm, zero_ref, zsem, sem, *, ts, tb):
    i = pl.program_id(0)

    @pl.when(i == 0)
    def _():
        zero_ref[...] = jnp.zeros(zero_ref.shape, F32)

        def zero_copy(n):
            start = pl.multiple_of(zb_ref[n] * tb, tb)
            return pltpu.make_async_copy(zero_ref, o_hbm.at[pl.ds(start, tb), :], zsem)

        for n in range(zb_ref.shape[0]):
            pl.when(zb_ref[n] >= 0)(lambda n=n: zero_copy(n).start())
        for n in range(zb_ref.shape[0]):
            pl.when(zb_ref[n] >= 0)(lambda n=n: zero_copy(n).wait())

    def issue(r, carry):
        for k in range(2):
            _row_copy(h_ref, r, o_hbm, dest_ref[0, 0, 2 * r + k], sem).start()
        return carry

    lax.fori_loop(0, ts, issue, 0, unroll=True)
    pltpu.make_async_copy(o_hbm.at[pl.ds(0, 2 * ts), :], o_hbm.at[pl.ds(0, 2 * ts), :], sem).wait()


def _moe_scatter(h2, dest, zero_blocks, n_rows):
    T, D = h2.shape
    ts = min(TS_SCATTER, T)
    tb = TB_MOE
    grid_spec = pltpu.PrefetchScalarGridSpec(
        num_scalar_prefetch=1,
        grid=(T // ts,),
        in_specs=[pl.BlockSpec((1, 1, 2 * ts), lambda i, zb: (i, 0, 0), memory_space=pltpu.SMEM),
                  pl.BlockSpec((ts, D), lambda i, zb: (i, 0))],
        out_specs=pl.BlockSpec(memory_space=pl.ANY),
        scratch_shapes=[pltpu.VMEM((tb, D), F32), pltpu.SemaphoreType.DMA(()), pltpu.SemaphoreType.DMA(())],
    )
    return pl.pallas_call(
        functools.partial(_scatter_kernel, ts=ts, tb=tb),
        grid_spec=grid_spec,
        out_shape=jax.ShapeDtypeStruct((n_rows, D), F32),
        compiler_params=_params("arbitrary"),
        name="moe_scatter",
    )(zero_blocks, dest.reshape(T // ts, 1, 2 * ts), h2)


def _expert_kernel(be_ref, nv_ref, x_ref, w1_ref, w3_ref, w2_ref, o_ref, w1b_ref, w3b_ref, w2b_ref):
    n = pl.program_id(0)
    nv = nv_ref[n]
    new_expert = (n == 0) | (be_ref[n] != be_ref[jnp.maximum(n - 1, 0)])

    @pl.when(new_expert & (nv > 0))
    def _():
        w1b_ref[...] = w1_ref[0].astype(BF16)
        w3b_ref[...] = w3_ref[0].astype(BF16)
        w2b_ref[...] = w2_ref[0].astype(BF16)

    @pl.when(nv > 0)
    def _():
        x = x_ref[...].astype(BF16)
        g = _dot(x, w1b_ref[...])
        u = _dot(x, w3b_ref[...])
        a = (g * _sigmoid(g) * u).astype(BF16)
        o_ref[...] = _dot(a, w2b_ref[...])

    @pl.when(nv == 0)
    def _():
        o_ref[...] = jnp.zeros(o_ref.shape, F32)


def _moe_experts(h_pad, block_e, block_nv, w1, w3, w2):
    P, D = h_pad.shape
    tb = TB_MOE
    wmap = lambda n, be, nv: (be[n], 0, 0)
    grid_spec = pltpu.PrefetchScalarGridSpec(
        num_scalar_prefetch=2,
        grid=(P // tb,),
        in_specs=[pl.BlockSpec((tb, D), lambda n, be, nv: (n, 0)),
                  pl.BlockSpec((1, D, D_EXPERT), wmap),
                  pl.BlockSpec((1, D, D_EXPERT), wmap),
                  pl.BlockSpec((1, D_EXPERT, D), wmap)],
        out_specs=pl.BlockSpec((tb, D), lambda n, be, nv: (n, 0)),
        scratch_shapes=[pltpu.VMEM((D, D_EXPERT), BF16), pltpu.VMEM((D, D_EXPERT), BF16),
                        pltpu.VMEM((D_EXPERT, D), BF16)],
    )
    return pl.pallas_call(
        _expert_kernel,
        grid_spec=grid_spec,
        out_shape=jax.ShapeDtypeStruct((P, D), F32),
        compiler_params=_params("arbitrary"),
        name="moe_experts",
    )(block_e, block_nv, h_pad, w1, w3, w2)


def _combine_kernel(dcur_ref, dnext_ref, x1_ref, route_ref, mod_ref, nw_ref, y_hbm, o_ref, buf, sem, *, tc, final):
    i = pl.program_id(0)
    slot = i % 2

    def gather(d_ref, s):
        def issue(r, carry):
            for k in range(2):
                _row_copy(y_hbm, d_ref[0, 0, 2 * r + k], buf.at[s], k * tc + r, sem.at[s]).start()
            return carry

        lax.fori_loop(0, tc, issue, 0, unroll=True)

    pl.when(i == 0)(lambda: gather(dcur_ref, 0))
    pl.when(i + 1 < pl.num_programs(0))(lambda: gather(dnext_ref, 1 - slot))
    pltpu.make_async_copy(y_hbm.at[pl.ds(0, 2 * tc), :], buf.at[slot], sem.at[slot]).wait()
    route = route_ref[...]
    y = route[:, 2:3] * buf[slot, 0:tc, :] + route[:, 3:4] * buf[slot, tc:2 * tc, :]
    x2 = x1_ref[...] + mod_ref[0][5:6] * y
    o_ref[...] = _rms(x2) * nw_ref[...] if final else x2


def _moe_combine(y_pad, dest, x1, route, mod, norm_w, seq, final):
    T, D = x1.shape
    tc = min(TC_COMBINE, seq)
    per_batch = seq // tc
    row = lambda i: (i, 0)
    n = T // tc
    dest3 = dest.reshape(n, 1, 2 * tc)
    return pl.pallas_call(
        functools.partial(_combine_kernel, tc=tc, final=final),
        grid=(n,),
        in_specs=[pl.BlockSpec((1, 1, 2 * tc), lambda i: (i, 0, 0), memory_space=pltpu.SMEM),
                  pl.BlockSpec((1, 1, 2 * tc), lambda i: (jnp.minimum(i + 1, n - 1), 0, 0),
                               memory_space=pltpu.SMEM),
                  pl.BlockSpec((tc, D), row),
                  pl.BlockSpec((tc, LANES), row),
                  pl.BlockSpec((1, 6, D), lambda i: (i // per_batch, 0, 0)),
                  pl.BlockSpec((1, D), lambda i: (0, 0)),
                  pl.BlockSpec(memory_space=pl.ANY)],
        out_specs=pl.BlockSpec((tc, D), row),
        out_shape=jax.ShapeDtypeStruct((T, D), F32),
        scratch_shapes=[pltpu.VMEM((2, 2 * tc, D), F32), pltpu.SemaphoreType.DMA((2,))],
        compiler_params=_params("arbitrary"),
        name="moe_combine",
    )(dest3, dest3, x1, route, mod, norm_w.reshape(1, D), y_pad)


def _routing_tables(route, counts, n_blocks):
    tb = TB_MOE
    expert = route[:, 0:2].astype(jnp.int32)
    rank = route[:, 4:6].astype(jnp.int32)
    cnt = counts[0, :N_EXPERTS].astype(jnp.int32)
    nblk = (cnt + tb - 1) // tb
    blk_end = jnp.cumsum(nblk)
    blk_start = blk_end - nblk
    hot = expert[:, :, None] == jnp.arange(N_EXPERTS, dtype=jnp.int32)
    dest = jnp.sum(jnp.where(hot, blk_start * tb, 0), axis=-1) + rank
    blocks = jnp.arange(n_blocks, dtype=jnp.int32)
    block_e = jnp.minimum(jnp.sum(blocks[:, None] >= blk_end[None, :], axis=1), N_EXPERTS - 1).astype(jnp.int32)
    left = cnt[block_e] - (blocks - blk_start[block_e]) * tb
    block_nv = jnp.where(blocks < blk_end[-1], jnp.clip(left, 0, tb), 0).astype(jnp.int32)
    tail = blk_end[-1] + jnp.arange(N_EXPERTS, dtype=jnp.int32)
    zero_blocks = jnp.concatenate([jnp.where(nblk > 0, blk_end - 1, -1),
                                   jnp.where(tail < n_blocks, tail, -1)]).astype(jnp.int32)
    return dest.astype(jnp.int32), block_e, block_nv, zero_blocks


def kernel(x, c, positions, w_ada, b_ada, norm_mix, w_in, w_ret_o, w_diff_o, lam_q1, lam_k1, lam_q2, lam_k2,
           diff_norm, w_out, norm_ffn, w_router_group, b_router_group, w_router_expert, b_router_expert,
           w_exp_gate, w_exp_up, w_exp_down, norm_final):
    B, S, D = x.shape
    T = B * S
    depth = w_ada.shape[0]
    assert D_IN == w_in.shape[2] and S % RET_CHUNK == 0
    xf = x.reshape(T, D)
    pos = positions.reshape(T, 1)
    n_blocks = (2 * T) // TB_MOE + N_EXPERTS
    for l in range(depth):
        lambda_init = 0.8 - 0.6 * math.exp(-0.3 * l)
        mod = _adaln(c, w_ada[l], b_ada[l]).reshape(B, 6, D)
        proj = _inproj(xf, pos, mod, norm_mix[l], w_in[l].astype(BF16), S)
        ro = _retention(proj, B, S)
        lam4 = jnp.stack([lam_q1[l], lam_k1[l], lam_q2[l], lam_k2[l]]).astype(F32)
        do = _diff_attention(proj, lam4, diff_norm[l].astype(F32), lambda_init, B, S)

        pad = LANES - N_EXPERTS - N_GROUPS
        w_rt = jnp.concatenate([w_router_expert[l], w_router_group[l], jnp.zeros((D, pad), F32)], axis=1)
        b_rt = jnp.concatenate([b_router_expert[l], b_router_group[l], jnp.zeros((pad,), F32)]).reshape(1, LANES)
        x1, h2, route, counts = _post_mixer(
            ro, do, proj, xf, mod, w_ret_o[l].astype(BF16), w_diff_o[l].astype(BF16), w_out[l].astype(BF16),
            norm_ffn[l], w_rt.astype(BF16), b_rt, S)

        dest, block_e, block_nv, zero_blocks = _routing_tables(route, counts, n_blocks)
        h_pad = _moe_scatter(h2, dest.reshape(-1), zero_blocks, n_blocks * TB_MOE)
        y_pad = _moe_experts(h_pad, block_e, block_nv, w_exp_gate[l], w_exp_up[l], w_exp_down[l])
        xf = _moe_combine(y_pad, dest.reshape(-1), x1, route, mod, norm_final, S, l == depth - 1)
    return xf.reshape(B, S, D)
```

```python
import functools
import math

import jax
import jax.numpy as jnp
from jax import lax
from jax.experimental import pallas as pl
from jax.experimental.pallas import tpu as pltpu

F32 = jnp.float32
BF16 = jnp.bfloat16

EPS = 1e-6
ROPE_THETA = 10000.0
LANES = 128
RET_HEADS, RET_DK, RET_DV, RET_CHUNK = 4, 128, 256, 128
DIFF_HEADS, DIFF_DK, DIFF_DV = 4, 128, 256
N_GROUPS, EXPERTS_PER_GROUP, N_EXPERTS, D_EXPERT = 4, 8, 32, 512
OFF_RQ, OFF_RK, OFF_RV, OFF_RG, OFF_DQ, OFF_DK, OFF_DV, OFF_GR, OFF_GD, D_IN = (
    0, 512, 1024, 2048, 3072, 4096, 5120, 6144, 7168, 8192)
NEG_BIG = -1e30
VMEM_LIMIT_BYTES = 48 * 1024 * 1024
VMEM_LIMIT_INPROJ_BYTES = 56 * 1024 * 1024

TM_INPROJ, TN_INPROJ = 512, 1024
TR_RETENTION = 512
BQ_ATTN = 512
TM_POST = 512
TB_MOE = 256
TS_SCATTER = 256
TC_COMBINE = 256


def _params(*sem):
    return pltpu.CompilerParams(dimension_semantics=sem, vmem_limit_bytes=VMEM_LIMIT_BYTES)


def _sigmoid(v):
    return 1.0 / (1.0 + jnp.exp(-v))


def _rms(v):
    return v * lax.rsqrt(jnp.mean(v * v, axis=-1, keepdims=True) + EPS)


def _dot(a, b):
    return jnp.dot(a, b, preferred_element_type=F32)


def _dot_nt(a, b):
    return lax.dot_general(a, b, (((1,), (1,)), ((), ())), preferred_element_type=F32)


def _ada_kernel(c_ref, w_ref, b_ref, o_ref):
    c = c_ref[...]
    a = (c * _sigmoid(c)).astype(BF16)
    o_ref[...] = _dot(a, w_ref[...].astype(BF16)) + b_ref[...]


def _adaln(c, w, b):
    B, D = c.shape
    n = w.shape[1] // D
    return pl.pallas_call(
        _ada_kernel,
        grid=(n,),
        in_specs=[pl.BlockSpec((B, D), lambda j: (0, 0)),
                  pl.BlockSpec((D, D), lambda j: (0, j)),
                  pl.BlockSpec((1, D), lambda j: (0, j))],
        out_specs=pl.BlockSpec((B, D), lambda j: (0, j)),
        out_shape=jax.ShapeDtypeStruct((B, n * D), F32),
        compiler_params=_params("parallel"),
        name="adaln",
    )(c, w, b.reshape(1, -1))


ROPE_GROUPS = ((OFF_RQ, OFF_RK, 1.0), (OFF_RK, OFF_RV, RET_DK ** -0.5),
               (OFF_DQ, OFF_DK, DIFF_DK ** -0.5 * math.log2(math.e)), (OFF_DK, OFF_DV, 1.0))


def _rope_scale(col):
    for lo, hi, scale in ROPE_GROUPS:
        if lo <= col < hi:
            return scale
    return None


def _inproj_kernel(x0_ref, pos0_ref, mod0_ref, xn_ref, posn_ref, modn_ref, nw_ref, inv_ref, sgn_ref,
                   w_ref, o_ref, h_ref, cos_ref, sin_ref, *, tn):
    i = pl.program_id(0)

    def prepare(x, pos, mod, slot):
        h = _rms(x) * nw_ref[...] * (1.0 + mod[1:2]) + mod[0:1]
        h_ref[slot] = h.astype(BF16)
        ang = pos.astype(F32) * inv_ref[...]
        cos_ref[slot] = jnp.cos(ang)
        sin_ref[slot] = jnp.sin(ang) * sgn_ref[...]

    @pl.when(i == 0)
    def _():
        prepare(x0_ref[...], pos0_ref[...], mod0_ref[0], 0)

    cur = i % 2
    for g in range(o_ref.shape[1] // tn):
        acc = _dot(h_ref[cur], w_ref[:, g * tn:(g + 1) * tn])
        for k in range(tn // LANES):
            first = g * tn + k * LANES
            a = acc[:, k * LANES:(k + 1) * LANES]
            scale = _rope_scale(first)
            if scale is not None:
                a = a * cos_ref[cur] + pltpu.roll(a, LANES // 2, 1) * sin_ref[cur]
                if scale != 1.0:
                    a = a * scale
            o_ref[:, first:first + LANES] = a.astype(BF16)

    prepare(xn_ref[...], posn_ref[...], modn_ref[0], 1 - cur)


def _inproj(xf, pos, mod, norm_w, w_bf16, seq):
    T, D = xf.shape
    tm = min(TM_INPROJ, seq)
    tn = TN_INPROJ
    half = LANES // 2
    inv = ROPE_THETA ** (-jnp.arange(0, LANES, 2, dtype=F32) / LANES)
    inv = jnp.concatenate([inv, inv]).reshape(1, LANES)
    sgn = jnp.concatenate([-jnp.ones((half,), F32), jnp.ones((half,), F32)]).reshape(1, LANES)
    per_batch = seq // tm
    n_m = T // tm
    nxt = lambda i: jnp.minimum(i + 1, n_m - 1)
    once = dict(pipeline_mode=pl.Buffered(1))
    return pl.pallas_call(
        functools.partial(_inproj_kernel, tn=tn),
        grid=(n_m,),
        in_specs=[pl.BlockSpec((tm, D), lambda i: (0, 0), **once),
                  pl.BlockSpec((tm, 1), lambda i: (0, 0), **once),
                  pl.BlockSpec((1, 6, D), lambda i: (0, 0, 0), **once),
                  pl.BlockSpec((tm, D), lambda i: (nxt(i), 0)),
                  pl.BlockSpec((tm, 1), lambda i: (nxt(i), 0)),
                  pl.BlockSpec((1, 6, D), lambda i: (nxt(i) // per_batch, 0, 0)),
                  pl.BlockSpec((1, D), lambda i: (0, 0), **once),
                  pl.BlockSpec((1, LANES), lambda i: (0, 0), **once),
                  pl.BlockSpec((1, LANES), lambda i: (0, 0), **once),
                  pl.BlockSpec((D, D_IN), lambda i: (0, 0), **once)],
        out_specs=pl.BlockSpec((tm, D_IN), lambda i: (i, 0)),
        out_shape=jax.ShapeDtypeStruct((T, D_IN), BF16),
        scratch_shapes=[pltpu.VMEM((2, tm, D), BF16),
                        pltpu.VMEM((2, tm, LANES), F32),
                        pltpu.VMEM((2, tm, LANES), F32)],
        compiler_params=pltpu.CompilerParams(dimension_semantics=("arbitrary",),
                                             vmem_limit_bytes=VMEM_LIMIT_INPROJ_BYTES),
        name="inproj",
    )(xf, pos, mod, xf, pos, mod, norm_w.reshape(1, D), inv, sgn, w_bf16)


def _ret_kernel(q_ref, k_ref, v_ref, g_ref, dmask_ref, xi_ref, zeta_ref, cd_ref, o_ref, state_ref):
    @pl.when(pl.program_id(1) == 0)
    def _():
        state_ref[...] = jnp.zeros(state_ref.shape, F32)

    C = RET_CHUNK
    for h in range(RET_HEADS):
        qk = slice(h * RET_DK, (h + 1) * RET_DK)
        vv = slice(h * RET_DV, (h + 1) * RET_DV)
        st = state_ref[h]
        for n in range(q_ref.shape[0] // C):
            rows = slice(n * C, (n + 1) * C)
            q = q_ref[rows, qk]
            k = k_ref[rows, qk]
            v = v_ref[rows, vv]
            s = _dot_nt(q, k) * dmask_ref[h]
            qx = (q.astype(F32) * xi_ref[h]).astype(BF16)
            o = _dot(s.astype(BF16), v) + _dot(qx, st.astype(BF16))
            kz_t = (k.astype(F32) * zeta_ref[h]).T.astype(BF16)
            st = cd_ref[h] * st + _dot(kz_t, v)
            g = g_ref[rows, vv].astype(F32)
            o_ref[rows, vv] = (_rms(o) * (g * _sigmoid(g))).astype(BF16)
        state_ref[h] = st


def _retention(proj, batch, seq):
    T = proj.shape[0]
    C = RET_CHUNK
    H = RET_HEADS
    nc = seq // C
    gamma = 1.0 - jnp.exp2(-5.0 - jnp.arange(H, dtype=F32))
    log_g = jnp.log(gamma)
    idx = jnp.arange(C, dtype=F32)
    rel = idx[:, None] - idx[None, :]
    dmask = jnp.where(rel >= 0, jnp.exp(log_g[:, None, None] * jnp.maximum(rel, 0.0)), 0.0)
    zeta = jnp.exp(log_g[:, None] * (C - 1 - idx))
    xi = jnp.exp(log_g[:, None] * (idx + 1))
    cd = jnp.exp(log_g * C)
    zeta_b = jnp.broadcast_to(zeta[:, :, None], (H, C, RET_DK))
    xi_b = jnp.broadcast_to(xi[:, :, None], (H, C, RET_DK))
    cd_b = jnp.broadcast_to(cd[:, None, None], (H, 1, RET_DV))
    wq = H * RET_DK
    wv = H * RET_DV
    tr = min(TR_RETENTION, seq)
    ns = seq // tr
    row = lambda b, n: b * ns + n
    const3 = lambda b, n: (0, 0, 0)
    return pl.pallas_call(
        _ret_kernel,
        grid=(batch, ns),
        in_specs=[pl.BlockSpec((tr, wq), lambda b, n: (row(b, n), OFF_RQ // wq)),
                  pl.BlockSpec((tr, wq), lambda b, n: (row(b, n), OFF_RK // wq)),
                  pl.BlockSpec((tr, wv), lambda b, n: (row(b, n), OFF_RV // wv)),
                  pl.BlockSpec((tr, wv), lambda b, n: (row(b, n), OFF_RG // wv)),
                  pl.BlockSpec((H, C, C), const3),
                  pl.BlockSpec((H, C, RET_DK), const3),
                  pl.BlockSpec((H, C, RET_DK), const3),
                  pl.BlockSpec((H, 1, RET_DV), const3)],
        out_specs=pl.BlockSpec((tr, wv), lambda b, n: (row(b, n), 0)),
        out_shape=jax.ShapeDtypeStruct((T, wv), BF16),
        scratch_shapes=[pltpu.VMEM((H, RET_DK, RET_DV), F32)],
        compiler_params=_params("parallel", "arbitrary"),
        name="retention",
    )(proj, proj, proj, proj, dmask, xi_b, zeta_b, cd_b)


def _transpose_bf16(a):
    return a.astype(F32).T.astype(BF16)


def _attn_kernel(lam_ref, dn_ref, q1_ref, q2_ref, k1_ref, k2_ref, v_ref, o_ref,
                 vt_ref, qt_ref, s_ref, pm_ref, m_ref, l_ref, acc_ref, *, bq, bk, lambda_init):
    i = pl.program_id(2)
    n_sub = vt_ref.shape[0]

    @pl.when(i == 0)
    def _():
        for t in range(n_sub):
            vt_ref[t] = _transpose_bf16(v_ref[t * bk:(t + 1) * bk, :])

    m_ref[...] = jnp.full(m_ref.shape, NEG_BIG, F32)
    l_ref[...] = jnp.zeros(l_ref.shape, F32)
    acc_ref[...] = jnp.zeros(acc_ref.shape, F32)
    qt_ref[0] = _transpose_bf16(q1_ref[...])
    qt_ref[1] = _transpose_bf16(q2_ref[...])
    k_refs = (k1_ref, k2_ref)

    def score(t, slot, diagonal_offset=None):
        start = pl.multiple_of(t * bk, bk)
        for c in range(2):
            st = _dot(k_refs[c][pl.ds(start, bk), :], qt_ref[c])
            if diagonal_offset is not None:
                key = lax.broadcasted_iota(jnp.int32, (bk, bq), 0) + diagonal_offset
                qry = lax.broadcasted_iota(jnp.int32, (bk, bq), 1)
                st = jnp.where(key <= qry, st, NEG_BIG)
            s_ref[slot, c] = st
            pm_ref[slot, c] = jnp.max(st, axis=0, keepdims=True)

    def accumulate(t, slot):
        vt = vt_ref[t]
        for c in range(2):
            m_prev = m_ref[c]
            m_new = jnp.maximum(m_prev, pm_ref[slot, c])
            p = jnp.exp2(s_ref[slot, c] - m_new)
            alpha = jnp.exp2(m_prev - m_new)
            l_ref[c] = alpha * l_ref[c] + jnp.sum(p, axis=0, keepdims=True)
            acc_ref[c] = alpha * acc_ref[c] + _dot(vt, p.astype(BF16))
            m_ref[c] = m_new

    def pair(jj, next_is_diagonal):
        t = 2 * jj
        score(t + 1, 1)
        accumulate(t, 0)
        score(t + 2, 0, 0 if next_is_diagonal else None)
        accumulate(t + 1, 1)

    pl.when(i == 0)(lambda: score(0, 0, 0))
    pl.when(i > 0)(lambda: score(0, 0))

    def body(jj, carry):
        pair(jj, False)
        return carry

    lax.fori_loop(0, i - 1, body, 0)
    pl.when(i > 0)(lambda: pair(i - 1, True))
    score(2 * i + 1, 1, bk)
    accumulate(2 * i, 0)
    accumulate(2 * i + 1, 1)

    lam4 = lam_ref[...]
    lam = (jnp.exp(jnp.sum(lam4[0:1] * lam4[1:2], axis=1, keepdims=True))
           - jnp.exp(jnp.sum(lam4[2:3] * lam4[3:4], axis=1, keepdims=True)) + lambda_init)
    ot = acc_ref[0] * (1.0 / l_ref[0]) - lam * (acc_ref[1] * (1.0 / l_ref[1]))
    ot = ot * lax.rsqrt(jnp.mean(ot * ot, axis=0, keepdims=True) + EPS)
    o_ref[...] = (ot.T * dn_ref[...] * (1.0 - lambda_init)).astype(BF16)


def _diff_attention(proj, lam4, diff_norm, lambda_init, batch, seq):
    T = proj.shape[0]
    H = DIFF_HEADS
    bq = min(BQ_ATTN, seq)
    bk = bq // 2
    nq = seq // bq
    q0 = OFF_DQ // DIFF_DK
    k0 = OFF_DK // DIFF_DK
    v0 = OFF_DV // DIFF_DV
    return pl.pallas_call(
        functools.partial(_attn_kernel, bq=bq, bk=bk, lambda_init=lambda_init),
        grid=(batch, H, nq),
        in_specs=[pl.BlockSpec((4, DIFF_DK), lambda b, h, i: (0, 0)),
                  pl.BlockSpec((1, DIFF_DV), lambda b, h, i: (0, 0)),
                  pl.BlockSpec((bq, DIFF_DK), lambda b, h, i: (b * nq + i, q0 + 2 * h)),
                  pl.BlockSpec((bq, DIFF_DK), lambda b, h, i: (b * nq + i, q0 + 2 * h + 1)),
                  pl.BlockSpec((seq, DIFF_DK), lambda b, h, i: (b, k0 + 2 * h)),
                  pl.BlockSpec((seq, DIFF_DK), lambda b, h, i: (b, k0 + 2 * h + 1)),
                  pl.BlockSpec((seq, DIFF_DV), lambda b, h, i: (b, v0 + h))],
        out_specs=pl.BlockSpec((bq, DIFF_DV), lambda b, h, i: (b * nq + i, h)),
        out_shape=jax.ShapeDtypeStruct((T, H * DIFF_DV), BF16),
        scratch_shapes=[pltpu.VMEM((seq // bk, DIFF_DV, bk), BF16),
                        pltpu.VMEM((2, DIFF_DK, bq), BF16),
                        pltpu.VMEM((2, 2, bk, bq), F32),
                        pltpu.VMEM((2, 2, 1, bq), F32),
                        pltpu.VMEM((2, 1, bq), F32),
                        pltpu.VMEM((2, 1, bq), F32),
                        pltpu.VMEM((2, DIFF_DV, bq), F32)],
        compiler_params=_params("parallel", "parallel", "arbitrary"),
        name="diff_attn",
    )(lam4, diff_norm.reshape(1, DIFF_DV), proj, proj, proj, proj, proj)


def _post_kernel(ro_ref, do_ref, gr_ref, gd_ref, x_ref, mod_ref, wr_ref, wd_ref, wo_ref, nf_ref,
                 wrt_ref, brt_ref, x1_ref, h2_ref, route_ref, cnt_ref, carry_ref, *, tm):
    @pl.when(pl.program_id(0) == 0)
    def _():
        carry_ref[...] = jnp.zeros(carry_ref.shape, F32)

    mod = mod_ref[0]
    ret_out = _dot(ro_ref[...], wr_ref[...])
    diff_out = _dot(do_ref[...], wd_ref[...])
    merged = (_sigmoid(gr_ref[...].astype(F32)) * ret_out
              + _sigmoid(gd_ref[...].astype(F32)) * diff_out)
    x1 = x_ref[...] + mod[2:3] * _dot(merged.astype(BF16), wo_ref[...])
    x1_ref[...] = x1
    h2 = _rms(x1) * nf_ref[...] * (1.0 + mod[4:5]) + mod[3:4]
    h2_ref[...] = h2

    lg = _dot(h2.astype(BF16), wrt_ref[...]) + brt_ref[...]
    lane = lax.broadcasted_iota(jnp.int32, lg.shape, 1)
    far = jnp.int32(LANES)

    def top1(vals):
        best = jnp.max(vals, axis=1, keepdims=True)
        return best, jnp.min(jnp.where(vals == best, lane, far), axis=1, keepdims=True)

    is_group = (lane >= N_EXPERTS) & (lane < N_EXPERTS + N_GROUPS)
    g_best, g_lane = top1(jnp.where(is_group, lg, NEG_BIG))
    g_sum = jnp.sum(jnp.where(is_group, jnp.exp(lg - g_best), 0.0), axis=1, keepdims=True)
    g_top = 1.0 / g_sum
    first = (g_lane - N_EXPERTS) * EXPERTS_PER_GROUP
    in_group = (lane >= first) & (lane < first + EXPERTS_PER_GROUP)
    el = jnp.where(in_group, lg, NEG_BIG)
    e_a, i_a = top1(el)
    e_b, i_b = top1(jnp.where(lane == i_a, NEG_BIG, el))
    t = jnp.exp(e_b - e_a)
    w_a = g_top / (1.0 + t)
    w_b = g_top * t / (1.0 + t)

    hot_a = lane == i_a
    hot_b = lane == i_b
    hot = jnp.where(hot_a | hot_b, 1.0, 0.0)
    r_i = lax.broadcasted_iota(jnp.int32, (tm, tm), 0)
    c_i = lax.broadcasted_iota(jnp.int32, (tm, tm), 1)
    lower = jnp.where(c_i < r_i, 1.0, 0.0).astype(BF16)
    before = _dot(lower, hot.astype(BF16)) + carry_ref[...]
    rank_a = jnp.sum(jnp.where(hot_a, before, 0.0), axis=1, keepdims=True)
    rank_b = jnp.sum(jnp.where(hot_b, before, 0.0), axis=1, keepdims=True)
    total = carry_ref[...] + jnp.sum(hot, axis=0, keepdims=True)
    carry_ref[...] = total
    cnt_ref[...] = total

    fields = (i_a.astype(F32), i_b.astype(F32), w_a, w_b, rank_a, rank_b)
    route = jnp.zeros(lg.shape, F32)
    for n, f in enumerate(fields):
        route = jnp.where(lane == n, f, route)
    route_ref[...] = route


def _post_mixer(ro, do, proj, xf, mod, w_ret_o, w_diff_o, w_out, norm_ffn, w_rt, b_rt, seq):
    T, D = xf.shape
    tm = min(TM_POST, seq)
    per_batch = seq // tm
    row = lambda i: (i, 0)
    const = lambda i: (0, 0)
    return pl.pallas_call(
        functools.partial(_post_kernel, tm=tm),
        grid=(T // tm,),
        in_specs=[pl.BlockSpec((tm, D), row),
                  pl.BlockSpec((tm, D), row),
                  pl.BlockSpec((tm, D), lambda i: (i, OFF_GR // D)),
                  pl.BlockSpec((tm, D), lambda i: (i, OFF_GD // D)),
                  pl.BlockSpec((tm, D), row),
                  pl.BlockSpec((1, 6, D), lambda i: (i // per_batch, 0, 0)),
                  pl.BlockSpec((D, D), const),
                  pl.BlockSpec((D, D), const),
                  pl.BlockSpec((D, D), const),
                  pl.BlockSpec((1, D), const),
                  pl.BlockSpec((D, LANES), const),
                  pl.BlockSpec((1, LANES), const)],
        out_specs=[pl.BlockSpec((tm, D), row),
                   pl.BlockSpec((tm, D), row),
                   pl.BlockSpec((tm, LANES), row),
                   pl.BlockSpec((1, LANES), const)],
        out_shape=[jax.ShapeDtypeStruct((T, D), F32),
                   jax.ShapeDtypeStruct((T, D), F32),
                   jax.ShapeDtypeStruct((T, LANES), F32),
                   jax.ShapeDtypeStruct((1, LANES), F32)],
        scratch_shapes=[pltpu.VMEM((1, LANES), F32)],
        compiler_params=_params("arbitrary"),
        name="post_mixer",
    )(ro, do, proj, proj, xf, mod, w_ret_o, w_diff_o, w_out, norm_ffn.reshape(1, D), w_rt, b_rt)


def _row_copy(src, s, dst, d, sem):
    return pltpu.make_async_copy(src.at[pl.ds(s, 1), :], dst.at[pl.ds(d, 1), :], sem)


def _scatter_kernel(zb_ref, dest_ref, h_ref, o_hbm, zero_ref, zsem, sem, *, ts, tb):
    i = pl.program_id(0)

    @pl.when(i == 0)
    def _():
        zero_ref[...] = jnp.zeros(zero_ref.shape, F32)

        def zero_copy(n):
            start = pl.multiple_of(zb_ref[n] * tb, tb)
            return pltpu.make_async_copy(zero_ref, o_hbm.at[pl.ds(start, tb), :], zsem)

        for n in range(zb_ref.shape[0]):
            pl.when(zb_ref[n] >= 0)(lambda n=n: zero_copy(n).start())
        for n in range(zb_ref.shape[0]):
            pl.when(zb_ref[n] >= 0)(lambda n=n: zero_copy(n).wait())

    def issue(r, carry):
        for k in range(2):
            _row_copy(h_ref, r, o_hbm, dest_ref[0, 0, 2 * r + k], sem).start(priority=k)
        return carry

    lax.fori_loop(0, ts, issue, 0, unroll=True)
    pltpu.make_async_copy(o_hbm.at[pl.ds(0, 2 * ts), :], o_hbm.at[pl.ds(0, 2 * ts), :], sem).wait()


def _moe_scatter(h2, dest, zero_blocks, n_rows):
    T, D = h2.shape
    ts = min(TS_SCATTER, T)
    tb = TB_MOE
    grid_spec = pltpu.PrefetchScalarGridSpec(
        num_scalar_prefetch=1,
        grid=(T // ts,),
        in_specs=[pl.BlockSpec((1, 1, 2 * ts), lambda i, zb: (i, 0, 0), memory_space=pltpu.SMEM),
                  pl.BlockSpec((ts, D), lambda i, zb: (i, 0))],
        out_specs=pl.BlockSpec(memory_space=pl.ANY),
        scratch_shapes=[pltpu.VMEM((tb, D), F32), pltpu.SemaphoreType.DMA(()), pltpu.SemaphoreType.DMA(())],
    )
    return pl.pallas_call(
        functools.partial(_scatter_kernel, ts=ts, tb=tb),
        grid_spec=grid_spec,
        out_shape=jax.ShapeDtypeStruct((n_rows, D), F32),
        compiler_params=_params("arbitrary"),
        name="moe_scatter",
    )(zero_blocks, dest.reshape(T // ts, 1, 2 * ts), h2)


def _expert_kernel(be_ref, nv_ref, x_ref, w1_ref, w3_ref, w2_ref, o_ref, w1b_ref, w3b_ref, w2b_ref):
    n = pl.program_id(0)
    nv = nv_ref[n]
    new_expert = (n == 0) | (be_ref[n] != be_ref[jnp.maximum(n - 1, 0)])

    @pl.when(new_expert & (nv > 0))
    def _():
        w1b_ref[...] = w1_ref[0].astype(BF16)
        w3b_ref[...] = w3_ref[0].astype(BF16)
        w2b_ref[...] = w2_ref[0].astype(BF16)

    @pl.when(nv > 0)
    def _():
        x = x_ref[...].astype(BF16)
        g = _dot(x, w1b_ref[...])
        u = _dot(x, w3b_ref[...])
        a = (g * _sigmoid(g) * u).astype(BF16)
        o_ref[...] = _dot(a, w2b_ref[...])

    @pl.when(nv == 0)
    def _():
        o_ref[...] = jnp.zeros(o_ref.shape, F32)


def _moe_experts(h_pad, block_e, block_nv, w1, w3, w2):
    P, D = h_pad.shape
    tb = TB_MOE
    wmap = lambda n, be, nv: (be[n], 0, 0)
    grid_spec = pltpu.PrefetchScalarGridSpec(
        num_scalar_prefetch=2,
        grid=(P // tb,),
        in_specs=[pl.BlockSpec((tb, D), lambda n, be, nv: (n, 0)),
                  pl.BlockSpec((1, D, D_EXPERT), wmap),
                  pl.BlockSpec((1, D, D_EXPERT), wmap),
                  pl.BlockSpec((1, D_EXPERT, D), wmap)],
        out_specs=pl.BlockSpec((tb, D), lambda n, be, nv: (n, 0)),
        scratch_shapes=[pltpu.VMEM((D, D_EXPERT), BF16), pltpu.VMEM((D, D_EXPERT), BF16),
                        pltpu.VMEM((D_EXPERT, D), BF16)],
    )
    return pl.pallas_call(
        _expert_kernel,
        grid_spec=grid_spec,
        out_shape=jax.ShapeDtypeStruct((P, D), F32),
        compiler_params=_params("arbitrary"),
        name="moe_experts",
    )(block_e, block_nv, h_pad, w1, w3, w2)


def _combine_kernel(dcur_ref, dnext_ref, x1_ref, route_ref, mod_ref, nw_ref, y_hbm, o_ref, buf, sem, *, tc, final):
    i = pl.program_id(0)
    slot = i % 2

    def gather(d_ref, s):
        def issue(r, carry):
            for k in range(2):
                _row_copy(y_hbm, d_ref[0, 0, 2 * r + k], buf.at[s], k * tc + r, sem.at[s]).start(priority=k)
            return carry

        lax.fori_loop(0, tc, issue, 0, unroll=True)

    pl.when(i == 0)(lambda: gather(dcur_ref, 0))
    pl.when(i + 1 < pl.num_programs(0))(lambda: gather(dnext_ref, 1 - slot))
    pltpu.make_async_copy(y_hbm.at[pl.ds(0, 2 * tc), :], buf.at[slot], sem.at[slot]).wait()
    route = route_ref[...]
    y = route[:, 2:3] * buf[slot, 0:tc, :] + route[:, 3:4] * buf[slot, tc:2 * tc, :]
    x2 = x1_ref[...] + mod_ref[0][5:6] * y
    o_ref[...] = _rms(x2) * nw_ref[...] if final else x2


def _moe_combine(y_pad, dest, x1, route, mod, norm_w, seq, final):
    T, D = x1.shape
    tc = min(TC_COMBINE, seq)
    per_batch = seq // tc
    row = lambda i: (i, 0)
    n = T // tc
    dest3 = dest.reshape(n, 1, 2 * tc)
    return pl.pallas_call(
        functools.partial(_combine_kernel, tc=tc, final=final),
        grid=(n,),
        in_specs=[pl.BlockSpec((1, 1, 2 * tc), lambda i: (i, 0, 0), memory_space=pltpu.SMEM),
                  pl.BlockSpec((1, 1, 2 * tc), lambda i: (jnp.minimum(i + 1, n - 1), 0, 0),
                               memory_space=pltpu.SMEM),
                  pl.BlockSpec((tc, D), row),
                  pl.BlockSpec((tc, LANES), row),
                  pl.BlockSpec((1, 6, D), lambda i: (i // per_batch, 0, 0)),
                  pl.BlockSpec((1, D), lambda i: (0, 0)),
                  pl.BlockSpec(memory_space=pl.ANY)],
        out_specs=pl.BlockSpec((tc, D), row),
        out_shape=jax.ShapeDtypeStruct((T, D), F32),
        scratch_shapes=[pltpu.VMEM((2, 2 * tc, D), F32), pltpu.SemaphoreType.DMA((2,))],
        compiler_params=_params("arbitrary"),
        name="moe_combine",
    )(dest3, dest3, x1, route, mod, norm_w.reshape(1, D), y_pad)


def _routing_tables(route, counts, n_blocks):
    tb = TB_MOE
    expert = route[:, 0:2].astype(jnp.int32)
    rank = route[:, 4:6].astype(jnp.int32)
    cnt = counts[0, :N_EXPERTS].astype(jnp.int32)
    nblk = (cnt + tb - 1) // tb
    blk_end = jnp.cumsum(nblk)
    blk_start = blk_end - nblk
    hot = expert[:, :, None] == jnp.arange(N_EXPERTS, dtype=jnp.int32)
    dest = jnp.sum(jnp.where(hot, blk_start * tb, 0), axis=-1) + rank
    blocks = jnp.arange(n_blocks, dtype=jnp.int32)
    block_e = jnp.minimum(jnp.sum(blocks[:, None] >= blk_end[None, :], axis=1), N_EXPERTS - 1).astype(jnp.int32)
    left = cnt[block_e] - (blocks - blk_start[block_e]) * tb
    block_nv = jnp.where(blocks < blk_end[-1], jnp.clip(left, 0, tb), 0).astype(jnp.int32)
    tail = blk_end[-1] + jnp.arange(N_EXPERTS, dtype=jnp.int32)
    zero_blocks = jnp.concatenate([jnp.where(nblk > 0, blk_end - 1, -1),
                                   jnp.where(tail < n_blocks, tail, -1)]).astype(jnp.int32)
    return dest.astype(jnp.int32), block_e, block_nv, zero_blocks


def kernel(x, c, positions, w_ada, b_ada, norm_mix, w_in, w_ret_o, w_diff_o, lam_q1, lam_k1, lam_q2, lam_k2,
           diff_norm, w_out, norm_ffn, w_router_group, b_router_group, w_router_expert, b_router_expert,
           w_exp_gate, w_exp_up, w_exp_down, norm_final):
    B, S, D = x.shape
    T = B * S
    depth = w_ada.shape[0]
    assert D_IN == w_in.shape[2] and S % RET_CHUNK == 0
    xf = x.reshape(T, D)
    pos = positions.reshape(T, 1)
    n_blocks = (2 * T) // TB_MOE + N_EXPERTS
    for l in range(depth):
        lambda_init = 0.8 - 0.6 * math.exp(-0.3 * l)
        mod = _adaln(c, w_ada[l], b_ada[l]).reshape(B, 6, D)
        proj = _inproj(xf, pos, mod, norm_mix[l], w_in[l].astype(BF16), S)
        ro = _retention(proj, B, S)
        lam4 = jnp.stack([lam_q1[l], lam_k1[l], lam_q2[l], lam_k2[l]]).astype(F32)
        do = _diff_attention(proj, lam4, diff_norm[l].astype(F32), lambda_init, B, S)

        pad = LANES - N_EXPERTS - N_GROUPS
        w_rt = jnp.concatenate([w_router_expert[l], w_router_group[l], jnp.zeros((D, pad), F32)], axis=1)
        b_rt = jnp.concatenate([b_router_expert[l], b_router_group[l], jnp.zeros((pad,), F32)]).reshape(1, LANES)
        x1, h2, route, counts = _post_mixer(
            ro, do, proj, xf, mod, w_ret_o[l].astype(BF16), w_diff_o[l].astype(BF16), w_out[l].astype(BF16),
            norm_ffn[l], w_rt.astype(BF16), b_rt, S)

        dest, block_e, block_nv, zero_blocks = _routing_tables(route, counts, n_blocks)
        h_pad = _moe_scatter(h2, dest.reshape(-1), zero_blocks, n_blocks * TB_MOE)
        y_pad = _moe_experts(h_pad, block_e, block_nv, w_exp_gate[l], w_exp_up[l], w_exp_down[l])
        xf = _moe_combine(y_pad, dest.reshape(-1), x1, route, mod, norm_final, S, l == depth - 1)
    return xf.reshape(B, S, D)
```

```python
import functools
import math

import jax
import jax.numpy as jnp
from jax import lax
from jax.experimental import pallas as pl
from jax.experimental.pallas import tpu as pltpu

F32 = jnp.float32
BF16 = jnp.bfloat16

EPS = 1e-6
ROPE_THETA = 10000.0
LANES = 128
RET_HEADS, RET_DK, RET_DV, RET_CHUNK = 4, 128, 256, 128
DIFF_HEADS, DIFF_DK, DIFF_DV = 4, 128, 256
N_GROUPS, EXPERTS_PER_GROUP, N_EXPERTS, D_EXPERT = 4, 8, 32, 512
OFF_RQ, OFF_RK, OFF_RV, OFF_RG, OFF_DQ, OFF_DK, OFF_DV, OFF_GR, OFF_GD, D_IN = (
    0, 512, 1024, 2048, 3072, 4096, 5120, 6144, 7168, 8192)
NEG_BIG = -1e30
VMEM_LIMIT_BYTES = 48 * 1024 * 1024
VMEM_LIMIT_INPROJ_BYTES = 56 * 1024 * 1024

TM_INPROJ, TN_INPROJ = 512, 1024
TR_RETENTION = 512
BQ_ATTN = 512
TM_POST = 512
TB_MOE = 256
TS_SCATTER = 256
TC_COMBINE = 256


def _params(*sem):
    return pltpu.CompilerParams(dimension_semantics=sem, vmem_limit_bytes=VMEM_LIMIT_BYTES)


def _sigmoid(v):
    return 1.0 / (1.0 + jnp.exp(-v))


def _rms(v):
    return v * lax.rsqrt(jnp.mean(v * v, axis=-1, keepdims=True) + EPS)


def _dot(a, b):
    return jnp.dot(a, b, preferred_element_type=F32)


def _dot_nt(a, b):
    return lax.dot_general(a, b, (((1,), (1,)), ((), ())), preferred_element_type=F32)


def _pack_bf16_pairs(x):
    m = x.shape[1] // 2

    def rounded_high_half(v):
        u = lax.bitcast_convert_type(v, jnp.uint32)
        return (u + jnp.uint32(0x7FFF) + ((u >> 16) & jnp.uint32(1))) & jnp.uint32(0xFFFF0000)

    return (rounded_high_half(x[:, :m]) >> 16) | rounded_high_half(x[:, m:])


def _unpack_bf16_pairs(w):
    lo = lax.bitcast_convert_type(w << 16, F32)
    hi = lax.bitcast_convert_type(w & jnp.uint32(0xFFFF0000), F32)
    return lo.astype(BF16), hi.astype(BF16)


def _ada_kernel(c_ref, w_ref, b_ref, o_ref):
    c = c_ref[...]
    a = (c * _sigmoid(c)).astype(BF16)
    o_ref[...] = _dot(a, w_ref[...].astype(BF16)) + b_ref[...]


def _adaln(c, w, b):
    B, D = c.shape
    n = w.shape[1] // D
    return pl.pallas_call(
        _ada_kernel,
        grid=(n,),
        in_specs=[pl.BlockSpec((B, D), lambda j: (0, 0)),
                  pl.BlockSpec((D, D), lambda j: (0, j)),
                  pl.BlockSpec((1, D), lambda j: (0, j))],
        out_specs=pl.BlockSpec((B, D), lambda j: (0, j)),
        out_shape=jax.ShapeDtypeStruct((B, n * D), F32),
        compiler_params=_params("parallel"),
        name="adaln",
    )(c, w, b.reshape(1, -1))


ROPE_GROUPS = ((OFF_RQ, OFF_RK, 1.0), (OFF_RK, OFF_RV, RET_DK ** -0.5),
               (OFF_DQ, OFF_DK, DIFF_DK ** -0.5 * math.log2(math.e)), (OFF_DK, OFF_DV, 1.0))


def _rope_scale(col):
    for lo, hi, scale in ROPE_GROUPS:
        if lo <= col < hi:
            return scale
    return None


def _inproj_kernel(x0_ref, pos0_ref, mod0_ref, xn_ref, posn_ref, modn_ref, nw_ref, inv_ref, sgn_ref,
                   w_ref, o_ref, h_ref, cos_ref, sin_ref, *, tn):
    i = pl.program_id(0)

    def prepare(x, pos, mod, slot):
        h = _rms(x) * nw_ref[...] * (1.0 + mod[1:2]) + mod[0:1]
        h_ref[slot] = h.astype(BF16)
        ang = pos.astype(F32) * inv_ref[...]
        cos_ref[slot] = jnp.cos(ang)
        sin_ref[slot] = jnp.sin(ang) * sgn_ref[...]

    @pl.when(i == 0)
    def _():
        prepare(x0_ref[...], pos0_ref[...], mod0_ref[0], 0)

    cur = i % 2
    for g in range(o_ref.shape[1] // tn):
        acc = _dot(h_ref[cur], w_ref[:, g * tn:(g + 1) * tn])
        for k in range(tn // LANES):
            first = g * tn + k * LANES
            a = acc[:, k * LANES:(k + 1) * LANES]
            scale = _rope_scale(first)
            if scale is not None:
                a = a * cos_ref[cur] + pltpu.roll(a, LANES // 2, 1) * sin_ref[cur]
                if scale != 1.0:
                    a = a * scale
            o_ref[:, first:first + LANES] = a.astype(BF16)

    prepare(xn_ref[...], posn_ref[...], modn_ref[0], 1 - cur)


def _inproj(xf, pos, mod, norm_w, w_bf16, seq):
    T, D = xf.shape
    tm = min(TM_INPROJ, seq)
    tn = TN_INPROJ
    half = LANES // 2
    inv = ROPE_THETA ** (-jnp.arange(0, LANES, 2, dtype=F32) / LANES)
    inv = jnp.concatenate([inv, inv]).reshape(1, LANES)
    sgn = jnp.concatenate([-jnp.ones((half,), F32), jnp.ones((half,), F32)]).reshape(1, LANES)
    per_batch = seq // tm
    n_m = T // tm
    nxt = lambda i: jnp.minimum(i + 1, n_m - 1)
    once = dict(pipeline_mode=pl.Buffered(1))
    return pl.pallas_call(
        functools.partial(_inproj_kernel, tn=tn),
        grid=(n_m,),
        in_specs=[pl.BlockSpec((tm, D), lambda i: (0, 0), **once),
                  pl.BlockSpec((tm, 1), lambda i: (0, 0), **once),
                  pl.BlockSpec((1, 6, D), lambda i: (0, 0, 0), **once),
                  pl.BlockSpec((tm, D), lambda i: (nxt(i), 0)),
                  pl.BlockSpec((tm, 1), lambda i: (nxt(i), 0)),
                  pl.BlockSpec((1, 6, D), lambda i: (nxt(i) // per_batch, 0, 0)),
                  pl.BlockSpec((1, D), lambda i: (0, 0), **once),
                  pl.BlockSpec((1, LANES), lambda i: (0, 0), **once),
                  pl.BlockSpec((1, LANES), lambda i: (0, 0), **once),
                  pl.BlockSpec((D, D_IN), lambda i: (0, 0), **once)],
        out_specs=pl.BlockSpec((tm, D_IN), lambda i: (i, 0)),
        out_shape=jax.ShapeDtypeStruct((T, D_IN), BF16),
        scratch_shapes=[pltpu.VMEM((2, tm, D), BF16),
                        pltpu.VMEM((2, tm, LANES), F32),
                        pltpu.VMEM((2, tm, LANES), F32)],
        compiler_params=pltpu.CompilerParams(dimension_semantics=("arbitrary",),
                                             vmem_limit_bytes=VMEM_LIMIT_INPROJ_BYTES),
        name="inproj",
    )(xf, pos, mod, xf, pos, mod, norm_w.reshape(1, D), inv, sgn, w_bf16)


def _ret_kernel(q_ref, k_ref, v_ref, g_ref, dmask_ref, xi_ref, zeta_ref, cd_ref, o_ref, state_ref):
    @pl.when(pl.program_id(1) == 0)
    def _():
        state_ref[...] = jnp.zeros(state_ref.shape, F32)

    C = RET_CHUNK
    for h in range(RET_HEADS):
        qk = slice(h * RET_DK, (h + 1) * RET_DK)
        vv = slice(h * RET_DV, (h + 1) * RET_DV)
        st = state_ref[h]
        for n in range(q_ref.shape[0] // C):
            rows = slice(n * C, (n + 1) * C)
            q = q_ref[rows, qk]
            k = k_ref[rows, qk]
            v = v_ref[rows, vv]
            s = _dot_nt(q, k) * dmask_ref[h]
            qx = (q.astype(F32) * xi_ref[h]).astype(BF16)
            o = _dot(s.astype(BF16), v) + _dot(qx, st.astype(BF16))
            kz_t = (k.astype(F32) * zeta_ref[h]).T.astype(BF16)
            st = cd_ref[h] * st + _dot(kz_t, v)
            g = g_ref[rows, vv].astype(F32)
            o_ref[rows, vv] = (_rms(o) * (g * _sigmoid(g))).astype(BF16)
        state_ref[h] = st


def _retention(proj, batch, seq):
    T = proj.shape[0]
    C = RET_CHUNK
    H = RET_HEADS
    nc = seq // C
    gamma = 1.0 - jnp.exp2(-5.0 - jnp.arange(H, dtype=F32))
    log_g = jnp.log(gamma)
    idx = jnp.arange(C, dtype=F32)
    rel = idx[:, None] - idx[None, :]
    dmask = jnp.where(rel >= 0, jnp.exp(log_g[:, None, None] * jnp.maximum(rel, 0.0)), 0.0)
    zeta = jnp.exp(log_g[:, None] * (C - 1 - idx))
    xi = jnp.exp(log_g[:, None] * (idx + 1))
    cd = jnp.exp(log_g * C)
    zeta_b = jnp.broadcast_to(zeta[:, :, None], (H, C, RET_DK))
    xi_b = jnp.broadcast_to(xi[:, :, None], (H, C, RET_DK))
    cd_b = jnp.broadcast_to(cd[:, None, None], (H, 1, RET_DV))
    wq = H * RET_DK
    wv = H * RET_DV
    tr = min(TR_RETENTION, seq)
    ns = seq // tr
    row = lambda b, n: b * ns + n
    const3 = lambda b, n: (0, 0, 0)
    return pl.pallas_call(
        _ret_kernel,
        grid=(batch, ns),
        in_specs=[pl.BlockSpec((tr, wq), lambda b, n: (row(b, n), OFF_RQ // wq)),
                  pl.BlockSpec((tr, wq), lambda b, n: (row(b, n), OFF_RK // wq)),
                  pl.BlockSpec((tr, wv), lambda b, n: (row(b, n), OFF_RV // wv)),
                  pl.BlockSpec((tr, wv), lambda b, n: (row(b, n), OFF_RG // wv)),
                  pl.BlockSpec((H, C, C), const3),
                  pl.BlockSpec((H, C, RET_DK), const3),
                  pl.BlockSpec((H, C, RET_DK), const3),
                  pl.BlockSpec((H, 1, RET_DV), const3)],
        out_specs=pl.BlockSpec((tr, wv), lambda b, n: (row(b, n), 0)),
        out_shape=jax.ShapeDtypeStruct((T, wv), BF16),
        scratch_shapes=[pltpu.VMEM((H, RET_DK, RET_DV), F32)],
        compiler_params=_params("parallel", "arbitrary"),
        name="retention",
    )(proj, proj, proj, proj, dmask, xi_b, zeta_b, cd_b)


def _transpose_bf16(a):
    return a.astype(F32).T.astype(BF16)


def _attn_kernel(lam_ref, dn_ref, q1_ref, q2_ref, k1_ref, k2_ref, v_ref, o_ref,
                 vt_ref, qt_ref, s_ref, pm_ref, m_ref, l_ref, acc_ref, *, bq, bk, lambda_init):
    i = pl.program_id(2)
    n_sub = vt_ref.shape[0]

    @pl.when(i == 0)
    def _():
        for t in range(n_sub):
            vt_ref[t] = _transpose_bf16(v_ref[t * bk:(t + 1) * bk, :])

    m_ref[...] = jnp.full(m_ref.shape, NEG_BIG, F32)
    l_ref[...] = jnp.zeros(l_ref.shape, F32)
    acc_ref[...] = jnp.zeros(acc_ref.shape, F32)
    qt_ref[0] = _transpose_bf16(q1_ref[...])
    qt_ref[1] = _transpose_bf16(q2_ref[...])
    k_refs = (k1_ref, k2_ref)

    def score(t, slot, diagonal_offset=None):
        start = pl.multiple_of(t * bk, bk)
        for c in range(2):
            st = _dot(k_refs[c][pl.ds(start, bk), :], qt_ref[c])
            if diagonal_offset is not None:
                key = lax.broadcasted_iota(jnp.int32, (bk, bq), 0) + diagonal_offset
                qry = lax.broadcasted_iota(jnp.int32, (bk, bq), 1)
                st = jnp.where(key <= qry, st, NEG_BIG)
            s_ref[slot, c] = st
            pm_ref[slot, c] = jnp.max(st, axis=0, keepdims=True)

    def accumulate(t, slot):
        vt = vt_ref[t]
        for c in range(2):
            m_prev = m_ref[c]
            m_new = jnp.maximum(m_prev, pm_ref[slot, c])
            p = jnp.exp2(s_ref[slot, c] - m_new)
            alpha = jnp.exp2(m_prev - m_new)
            l_ref[c] = alpha * l_ref[c] + jnp.sum(p, axis=0, keepdims=True)
            acc_ref[c] = alpha * acc_ref[c] + _dot(vt, p.astype(BF16))
            m_ref[c] = m_new

    def pair(jj, next_is_diagonal):
        t = 2 * jj
        score(t + 1, 1)
        accumulate(t, 0)
        score(t + 2, 0, 0 if next_is_diagonal else None)
        accumulate(t + 1, 1)

    pl.when(i == 0)(lambda: score(0, 0, 0))
    pl.when(i > 0)(lambda: score(0, 0))

    def body(jj, carry):
        pair(jj, False)
        return carry

    lax.fori_loop(0, i - 1, body, 0)
    pl.when(i > 0)(lambda: pair(i - 1, True))
    score(2 * i + 1, 1, bk)
    accumulate(2 * i, 0)
    accumulate(2 * i + 1, 1)

    lam4 = lam_ref[...]
    lam = (jnp.exp(jnp.sum(lam4[0:1] * lam4[1:2], axis=1, keepdims=True))
           - jnp.exp(jnp.sum(lam4[2:3] * lam4[3:4], axis=1, keepdims=True)) + lambda_init)
    ot = acc_ref[0] * (1.0 / l_ref[0]) - lam * (acc_ref[1] * (1.0 / l_ref[1]))
    ot = ot * lax.rsqrt(jnp.mean(ot * ot, axis=0, keepdims=True) + EPS)
    o_ref[...] = (ot.T * dn_ref[...] * (1.0 - lambda_init)).astype(BF16)


def _diff_attention(proj, lam4, diff_norm, lambda_init, batch, seq):
    T = proj.shape[0]
    H = DIFF_HEADS
    bq = min(BQ_ATTN, seq)
    bk = bq // 2
    nq = seq // bq
    q0 = OFF_DQ // DIFF_DK
    k0 = OFF_DK // DIFF_DK
    v0 = OFF_DV // DIFF_DV
    return pl.pallas_call(
        functools.partial(_attn_kernel, bq=bq, bk=bk, lambda_init=lambda_init),
        grid=(batch, H, nq),
        in_specs=[pl.BlockSpec((4, DIFF_DK), lambda b, h, i: (0, 0)),
                  pl.BlockSpec((1, DIFF_DV), lambda b, h, i: (0, 0)),
                  pl.BlockSpec((bq, DIFF_DK), lambda b, h, i: (b * nq + i, q0 + 2 * h)),
                  pl.BlockSpec((bq, DIFF_DK), lambda b, h, i: (b * nq + i, q0 + 2 * h + 1)),
                  pl.BlockSpec((seq, DIFF_DK), lambda b, h, i: (b, k0 + 2 * h)),
                  pl.BlockSpec((seq, DIFF_DK), lambda b, h, i: (b, k0 + 2 * h + 1)),
                  pl.BlockSpec((seq, DIFF_DV), lambda b, h, i: (b, v0 + h))],
        out_specs=pl.BlockSpec((bq, DIFF_DV), lambda b, h, i: (b * nq + i, h)),
        out_shape=jax.ShapeDtypeStruct((T, H * DIFF_DV), BF16),
        scratch_shapes=[pltpu.VMEM((seq // bk, DIFF_DV, bk), BF16),
                        pltpu.VMEM((2, DIFF_DK, bq), BF16),
                        pltpu.VMEM((2, 2, bk, bq), F32),
                        pltpu.VMEM((2, 2, 1, bq), F32),
                        pltpu.VMEM((2, 1, bq), F32),
                        pltpu.VMEM((2, 1, bq), F32),
                        pltpu.VMEM((2, DIFF_DV, bq), F32)],
        compiler_params=_params("parallel", "parallel", "arbitrary"),
        name="diff_attn",
    )(lam4, diff_norm.reshape(1, DIFF_DV), proj, proj, proj, proj, proj)


def _post_kernel(ro_ref, do_ref, gr_ref, gd_ref, x_ref, mod_ref, wr_ref, wd_ref, wo_ref, nf_ref,
                 wrt_ref, brt_ref, x1_ref, h2_ref, route_ref, cnt_ref, carry_ref, *, tm):
    @pl.when(pl.program_id(0) == 0)
    def _():
        carry_ref[...] = jnp.zeros(carry_ref.shape, F32)

    mod = mod_ref[0]
    ret_out = _dot(ro_ref[...], wr_ref[...])
    diff_out = _dot(do_ref[...], wd_ref[...])
    merged = (_sigmoid(gr_ref[...].astype(F32)) * ret_out
              + _sigmoid(gd_ref[...].astype(F32)) * diff_out)
    x1 = x_ref[...] + mod[2:3] * _dot(merged.astype(BF16), wo_ref[...])
    x1_ref[...] = x1
    h2 = _rms(x1) * nf_ref[...] * (1.0 + mod[4:5]) + mod[3:4]
    h2_ref[...] = _pack_bf16_pairs(h2)

    lg = _dot(h2.astype(BF16), wrt_ref[...]) + brt_ref[...]
    lane = lax.broadcasted_iota(jnp.int32, lg.shape, 1)
    far = jnp.int32(LANES)

    def top1(vals):
        best = jnp.max(vals, axis=1, keepdims=True)
        return best, jnp.min(jnp.where(vals == best, lane, far), axis=1, keepdims=True)

    is_group = (lane >= N_EXPERTS) & (lane < N_EXPERTS + N_GROUPS)
    g_best, g_lane = top1(jnp.where(is_group, lg, NEG_BIG))
    g_sum = jnp.sum(jnp.where(is_group, jnp.exp(lg - g_best), 0.0), axis=1, keepdims=True)
    g_top = 1.0 / g_sum
    first = (g_lane - N_EXPERTS) * EXPERTS_PER_GROUP
    in_group = (lane >= first) & (lane < first + EXPERTS_PER_GROUP)
    el = jnp.where(in_group, lg, NEG_BIG)
    e_a, i_a = top1(el)
    e_b, i_b = top1(jnp.where(lane == i_a, NEG_BIG, el))
    t = jnp.exp(e_b - e_a)
    w_a = g_top / (1.0 + t)
    w_b = g_top * t / (1.0 + t)

    hot_a = lane == i_a
    hot_b = lane == i_b
    hot = jnp.where(hot_a | hot_b, 1.0, 0.0)
    r_i = lax.broadcasted_iota(jnp.int32, (tm, tm), 0)
    c_i = lax.broadcasted_iota(jnp.int32, (tm, tm), 1)
    lower = jnp.where(c_i < r_i, 1.0, 0.0).astype(BF16)
    before = _dot(lower, hot.astype(BF16)) + carry_ref[...]
    rank_a = jnp.sum(jnp.where(hot_a, before, 0.0), axis=1, keepdims=True)
    rank_b = jnp.sum(jnp.where(hot_b, before, 0.0), axis=1, keepdims=True)
    total = carry_ref[...] + jnp.sum(hot, axis=0, keepdims=True)
    carry_ref[...] = total
    cnt_ref[...] = total

    fields = (i_a.astype(F32), i_b.astype(F32), w_a, w_b, rank_a, rank_b)
    route = jnp.zeros(lg.shape, F32)
    for n, f in enumerate(fields):
        route = jnp.where(lane == n, f, route)
    route_ref[...] = route


def _post_mixer(ro, do, proj, xf, mod, w_ret_o, w_diff_o, w_out, norm_ffn, w_rt, b_rt, seq):
    T, D = xf.shape
    tm = min(TM_POST, seq)
    per_batch = seq // tm
    row = lambda i: (i, 0)
    const = lambda i: (0, 0)
    return pl.pallas_call(
        functools.partial(_post_kernel, tm=tm),
        grid=(T // tm,),
        in_specs=[pl.BlockSpec((tm, D), row),
                  pl.BlockSpec((tm, D), row),
                  pl.BlockSpec((tm, D), lambda i: (i, OFF_GR // D)),
                  pl.BlockSpec((tm, D), lambda i: (i, OFF_GD // D)),
                  pl.BlockSpec((tm, D), row),
                  pl.BlockSpec((1, 6, D), lambda i: (i // per_batch, 0, 0)),
                  pl.BlockSpec((D, D), const),
                  pl.BlockSpec((D, D), const),
                  pl.BlockSpec((D, D), const),
                  pl.BlockSpec((1, D), const),
                  pl.BlockSpec((D, LANES), const),
                  pl.BlockSpec((1, LANES), const)],
        out_specs=[pl.BlockSpec((tm, D), row),
                   pl.BlockSpec((tm, D // 2), row),
                   pl.BlockSpec((tm, LANES), row),
                   pl.BlockSpec((1, LANES), const)],
        out_shape=[jax.ShapeDtypeStruct((T, D), F32),
                   jax.ShapeDtypeStruct((T, D // 2), jnp.uint32),
                   jax.ShapeDtypeStruct((T, LANES), F32),
                   jax.ShapeDtypeStruct((1, LANES), F32)],
        scratch_shapes=[pltpu.VMEM((1, LANES), F32)],
        compiler_params=_params("arbitrary"),
        name="post_mixer",
    )(ro, do, proj, proj, xf, mod, w_ret_o, w_diff_o, w_out, norm_ffn.reshape(1, D), w_rt, b_rt)


def _row_copy(src, s, dst, d, sem):
    return pltpu.make_async_copy(src.at[pl.ds(s, 1), :], dst.at[pl.ds(d, 1), :], sem)


def _scatter_kernel(zb_ref, dest_ref, h_ref, o_hbm, zero_ref, zsem, sem, *, ts, tb):
    i = pl.program_id(0)

    @pl.when(i == 0)
    def _():
        zero_ref[...] = jnp.zeros(zero_ref.shape, zero_ref.dtype)

        def zero_copy(n):
            start = pl.multiple_of(zb_ref[n] * tb, tb)
            return pltpu.make_async_copy(zero_ref, o_hbm.at[pl.ds(start, tb), :], zsem)

        for n in range(zb_ref.shape[0]):
            pl.when(zb_ref[n] >= 0)(lambda n=n: zero_copy(n).start())
        for n in range(zb_ref.shape[0]):
            pl.when(zb_ref[n] >= 0)(lambda n=n: zero_copy(n).wait())

    def issue(r, carry):
        for k in range(2):
            _row_copy(h_ref, r, o_hbm, dest_ref[0, 0, 2 * r + k], sem).start(priority=k)
        return carry

    lax.fori_loop(0, ts, issue, 0, unroll=True)
    pltpu.make_async_copy(o_hbm.at[pl.ds(0, 2 * ts), :], o_hbm.at[pl.ds(0, 2 * ts), :], sem).wait()


def _moe_scatter(h2, dest, zero_blocks, n_rows):
    T, D = h2.shape
    ts = min(TS_SCATTER, T)
    tb = TB_MOE
    grid_spec = pltpu.PrefetchScalarGridSpec(
        num_scalar_prefetch=1,
        grid=(T // ts,),
        in_specs=[pl.BlockSpec((1, 1, 2 * ts), lambda i, zb: (i, 0, 0), memory_space=pltpu.SMEM),
                  pl.BlockSpec((ts, D), lambda i, zb: (i, 0))],
        out_specs=pl.BlockSpec(memory_space=pl.ANY),
        scratch_shapes=[pltpu.VMEM((tb, D), h2.dtype), pltpu.SemaphoreType.DMA(()), pltpu.SemaphoreType.DMA(())],
    )
    return pl.pallas_call(
        functools.partial(_scatter_kernel, ts=ts, tb=tb),
        grid_spec=grid_spec,
        out_shape=jax.ShapeDtypeStruct((n_rows, D), h2.dtype),
        compiler_params=_params("arbitrary"),
        name="moe_scatter",
    )(zero_blocks, dest.reshape(T // ts, 1, 2 * ts), h2)


def _expert_kernel(be_ref, nv_ref, x_ref, w1_ref, w3_ref, w2_ref, o_ref, w1b_ref, w3b_ref, w2b_ref):
    n = pl.program_id(0)
    nv = nv_ref[n]
    new_expert = (n == 0) | (be_ref[n] != be_ref[jnp.maximum(n - 1, 0)])

    @pl.when(new_expert & (nv > 0))
    def _():
        w1b_ref[...] = w1_ref[0].astype(BF16)
        w3b_ref[...] = w3_ref[0].astype(BF16)
        w2b_ref[...] = w2_ref[0].astype(BF16)

    @pl.when(nv > 0)
    def _():
        x_lo, x_hi = _unpack_bf16_pairs(x_ref[...])
        half = x_lo.shape[1]
        g = _dot(x_lo, w1b_ref[:half, :]) + _dot(x_hi, w1b_ref[half:, :])
        u = _dot(x_lo, w3b_ref[:half, :]) + _dot(x_hi, w3b_ref[half:, :])
        a = (g * _sigmoid(g) * u).astype(BF16)
        o_ref[...] = _pack_bf16_pairs(_dot(a, w2b_ref[...]))

    @pl.when(nv == 0)
    def _():
        o_ref[...] = jnp.zeros(o_ref.shape, o_ref.dtype)


def _moe_experts(h_pad, block_e, block_nv, w1, w3, w2):
    P = h_pad.shape[0]
    D = 2 * h_pad.shape[1]
    tb = TB_MOE
    wmap = lambda n, be, nv: (be[n], 0, 0)
    grid_spec = pltpu.PrefetchScalarGridSpec(
        num_scalar_prefetch=2,
        grid=(P // tb,),
        in_specs=[pl.BlockSpec((tb, D // 2), lambda n, be, nv: (n, 0)),
                  pl.BlockSpec((1, D, D_EXPERT), wmap),
                  pl.BlockSpec((1, D, D_EXPERT), wmap),
                  pl.BlockSpec((1, D_EXPERT, D), wmap)],
        out_specs=pl.BlockSpec((tb, D // 2), lambda n, be, nv: (n, 0)),
        scratch_shapes=[pltpu.VMEM((D, D_EXPERT), BF16), pltpu.VMEM((D, D_EXPERT), BF16),
                        pltpu.VMEM((D_EXPERT, D), BF16)],
    )
    return pl.pallas_call(
        _expert_kernel,
        grid_spec=grid_spec,
        out_shape=jax.ShapeDtypeStruct((P, D // 2), h_pad.dtype),
        compiler_params=_params("arbitrary"),
        name="moe_experts",
    )(block_e, block_nv, h_pad, w1, w3, w2)


def _combine_kernel(dcur_ref, dnext_ref, x1_ref, route_ref, mod_ref, nw_ref, y_hbm, o_ref, buf, sem, *, tc, final):
    i = pl.program_id(0)
    slot = i % 2

    def gather(d_ref, s):
        def issue(r, carry):
            for k in range(2):
                _row_copy(y_hbm, d_ref[0, 0, 2 * r + k], buf.at[s], k * tc + r, sem.at[s]).start(priority=k)
            return carry

        lax.fori_loop(0, tc, issue, 0, unroll=True)

    pl.when(i == 0)(lambda: gather(dcur_ref, 0))
    pl.when(i + 1 < pl.num_programs(0))(lambda: gather(dnext_ref, 1 - slot))
    pltpu.make_async_copy(y_hbm.at[pl.ds(0, 2 * tc), :], buf.at[slot], sem.at[slot]).wait()
    route = route_ref[...]

    def rows(first):
        lo, hi = _unpack_bf16_pairs(buf[slot, first:first + tc, :])
        return jnp.concatenate([lo, hi], axis=1).astype(F32)

    y = route[:, 2:3] * rows(0) + route[:, 3:4] * rows(tc)
    x2 = x1_ref[...] + mod_ref[0][5:6] * y
    o_ref[...] = _rms(x2) * nw_ref[...] if final else x2


def _moe_combine(y_pad, dest, x1, route, mod, norm_w, seq, final):
    T, D = x1.shape
    tc = min(TC_COMBINE, seq)
    per_batch = seq // tc
    row = lambda i: (i, 0)
    n = T // tc
    dest3 = dest.reshape(n, 1, 2 * tc)
    return pl.pallas_call(
        functools.partial(_combine_kernel, tc=tc, final=final),
        grid=(n,),
        in_specs=[pl.BlockSpec((1, 1, 2 * tc), lambda i: (i, 0, 0), memory_space=pltpu.SMEM),
                  pl.BlockSpec((1, 1, 2 * tc), lambda i: (jnp.minimum(i + 1, n - 1), 0, 0),
                               memory_space=pltpu.SMEM),
                  pl.BlockSpec((tc, D), row),
                  pl.BlockSpec((tc, LANES), row),
                  pl.BlockSpec((1, 6, D), lambda i: (i // per_batch, 0, 0)),
                  pl.BlockSpec((1, D), lambda i: (0, 0)),
                  pl.BlockSpec(memory_space=pl.ANY)],
        out_specs=pl.BlockSpec((tc, D), row),
        out_shape=jax.ShapeDtypeStruct((T, D), F32),
        scratch_shapes=[pltpu.VMEM((2, 2 * tc, D // 2), y_pad.dtype), pltpu.SemaphoreType.DMA((2,))],
        compiler_params=_params("arbitrary"),
        name="moe_combine",
    )(dest3, dest3, x1, route, mod, norm_w.reshape(1, D), y_pad)


def _routing_tables(route, counts, n_blocks):
    tb = TB_MOE
    expert = route[:, 0:2].astype(jnp.int32)
    rank = route[:, 4:6].astype(jnp.int32)
    cnt = counts[0, :N_EXPERTS].astype(jnp.int32)
    nblk = (cnt + tb - 1) // tb
    blk_end = jnp.cumsum(nblk)
    blk_start = blk_end - nblk
    hot = expert[:, :, None] == jnp.arange(N_EXPERTS, dtype=jnp.int32)
    dest = jnp.sum(jnp.where(hot, blk_start * tb, 0), axis=-1) + rank
    blocks = jnp.arange(n_blocks, dtype=jnp.int32)
    block_e = jnp.minimum(jnp.sum(blocks[:, None] >= blk_end[None, :], axis=1), N_EXPERTS - 1).astype(jnp.int32)
    left = cnt[block_e] - (blocks - blk_start[block_e]) * tb
    block_nv = jnp.where(blocks < blk_end[-1], jnp.clip(left, 0, tb), 0).astype(jnp.int32)
    tail = blk_end[-1] + jnp.arange(N_EXPERTS, dtype=jnp.int32)
    zero_blocks = jnp.concatenate([jnp.where(nblk > 0, blk_end - 1, -1),
                                   jnp.where(tail < n_blocks, tail, -1)]).astype(jnp.int32)
    return dest.astype(jnp.int32), block_e, block_nv, zero_blocks


def kernel(x, c, positions, w_ada, b_ada, norm_mix, w_in, w_ret_o, w_diff_o, lam_q1, lam_k1, lam_q2, lam_k2,
           diff_norm, w_out, norm_ffn, w_router_group, b_router_group, w_router_expert, b_router_expert,
           w_exp_gate, w_exp_up, w_exp_down, norm_final):
    B, S, D = x.shape
    T = B * S
    depth = w_ada.shape[0]
    assert D_IN == w_in.shape[2] and S % RET_CHUNK == 0
    xf = x.reshape(T, D)
    pos = positions.reshape(T, 1)
    n_blocks = (2 * T) // TB_MOE + N_EXPERTS
    for l in range(depth):
        lambda_init = 0.8 - 0.6 * math.exp(-0.3 * l)
        mod = _adaln(c, w_ada[l], b_ada[l]).reshape(B, 6, D)
        proj = _inproj(xf, pos, mod, norm_mix[l], w_in[l].astype(BF16), S)
        ro = _retention(proj, B, S)
        lam4 = jnp.stack([lam_q1[l], lam_k1[l], lam_q2[l], lam_k2[l]]).astype(F32)
        do = _diff_attention(proj, lam4, diff_norm[l].astype(F32), lambda_init, B, S)

        pad = LANES - N_EXPERTS - N_GROUPS
        w_rt = jnp.concatenate([w_router_expert[l], w_router_group[l], jnp.zeros((D, pad), F32)], axis=1)
        b_rt = jnp.concatenate([b_router_expert[l], b_router_group[l], jnp.zeros((pad,), F32)]).reshape(1, LANES)
        x1, h2, route, counts = _post_mixer(
            ro, do, proj, xf, mod, w_ret_o[l].astype(BF16), w_diff_o[l].astype(BF16), w_out[l].astype(BF16),
            norm_ffn[l], w_rt.astype(BF16), b_rt, S)

        dest, block_e, block_nv, zero_blocks = _routing_tables(route, counts, n_blocks)
        h_pad = _moe_scatter(h2, dest.reshape(-1), zero_blocks, n_blocks * TB_MOE)
        y_pad = _moe_experts(h_pad, block_e, block_nv, w_exp_gate[l], w_exp_up[l], w_exp_down[l])
        xf = _moe_combine(y_pad, dest.reshape(-1), x1, route, mod, norm_final, S, l == depth - 1)
    return xf.reshape(B, S, D)
```

```python
import functools
import math

import jax
import jax.numpy as jnp
from jax import lax
from jax.experimental import pallas as pl
from jax.experimental.pallas import tpu as pltpu

F32 = jnp.float32
BF16 = jnp.bfloat16

EPS = 1e-6
ROPE_THETA = 10000.0
LANES = 128
RET_HEADS, RET_DK, RET_DV, RET_CHUNK = 4, 128, 256, 128
DIFF_HEADS, DIFF_DK, DIFF_DV = 4, 128, 256
N_GROUPS, EXPERTS_PER_GROUP, N_EXPERTS, D_EXPERT = 4, 8, 32, 512
OFF_RQ, OFF_RK, OFF_RV, OFF_RG, OFF_DQ, OFF_DK, OFF_DV, OFF_GR, OFF_GD, D_IN = (
    0, 512, 1024, 2048, 3072, 4096, 5120, 6144, 7168, 8192)
NEG_BIG = -1e30
VMEM_LIMIT_BYTES = 48 * 1024 * 1024
VMEM_LIMIT_INPROJ_BYTES = 56 * 1024 * 1024

TM_INPROJ, TN_INPROJ = 512, 1024
TR_RETENTION = 512
BQ_ATTN = 512
TM_POST = 512
TB_MOE = 256
TS_SCATTER = 256
TC_COMBINE = 256


def _params(*sem):
    return pltpu.CompilerParams(dimension_semantics=sem, vmem_limit_bytes=VMEM_LIMIT_BYTES)


def _sigmoid(v):
    return 1.0 / (1.0 + jnp.exp(-v))


def _rms(v):
    return v * lax.rsqrt(jnp.mean(v * v, axis=-1, keepdims=True) + EPS)


def _dot(a, b):
    return jnp.dot(a, b, preferred_element_type=F32)


def _dot_nt(a, b):
    return lax.dot_general(a, b, (((1,), (1,)), ((), ())), preferred_element_type=F32)


def _pack_bf16_pairs(x):
    m = x.shape[1] // 2

    def rounded_bits(v):
        return lax.bitcast_convert_type(v.astype(BF16).astype(F32), jnp.uint32)

    return (rounded_bits(x[:, :m]) >> 16) | rounded_bits(x[:, m:])


def _unpack_bf16_pairs(w):
    lo = lax.bitcast_convert_type(w << 16, F32)
    hi = lax.bitcast_convert_type(w & jnp.uint32(0xFFFF0000), F32)
    return lo.astype(BF16), hi.astype(BF16)


def _ada_kernel(c_ref, w_ref, b_ref, o_ref):
    c = c_ref[...]
    a = (c * _sigmoid(c)).astype(BF16)
    o_ref[...] = _dot(a, w_ref[...].astype(BF16)) + b_ref[...]


def _adaln(c, w, b):
    B, D = c.shape
    n = w.shape[1] // D
    return pl.pallas_call(
        _ada_kernel,
        grid=(n,),
        in_specs=[pl.BlockSpec((B, D), lambda j: (0, 0)),
                  pl.BlockSpec((D, D), lambda j: (0, j)),
                  pl.BlockSpec((1, D), lambda j: (0, j))],
        out_specs=pl.BlockSpec((B, D), lambda j: (0, j)),
        out_shape=jax.ShapeDtypeStruct((B, n * D), F32),
        compiler_params=_params("parallel"),
        name="adaln",
    )(c, w, b.reshape(1, -1))


ROPE_GROUPS = ((OFF_RQ, OFF_RK, 1.0), (OFF_RK, OFF_RV, RET_DK ** -0.5),
               (OFF_DQ, OFF_DK, DIFF_DK ** -0.5 * math.log2(math.e)), (OFF_DK, OFF_DV, 1.0))


def _rope_scale(col):
    for lo, hi, scale in ROPE_GROUPS:
        if lo <= col < hi:
            return scale
    return None


def _inproj_kernel(x0_ref, pos0_ref, mod0_ref, xn_ref, posn_ref, modn_ref, nw_ref, inv_ref, sgn_ref,
                   w_ref, o_ref, h_ref, cos_ref, sin_ref, *, tn):
    i = pl.program_id(0)

    def prepare(x, pos, mod, slot):
        h = _rms(x) * nw_ref[...] * (1.0 + mod[1:2]) + mod[0:1]
        h_ref[slot] = h.astype(BF16)
        ang = pos.astype(F32) * inv_ref[...]
        cos_ref[slot] = jnp.cos(ang)
        sin_ref[slot] = jnp.sin(ang) * sgn_ref[...]

    @pl.when(i == 0)
    def _():
        prepare(x0_ref[...], pos0_ref[...], mod0_ref[0], 0)

    cur = i % 2
    for g in range(o_ref.shape[1] // tn):
        acc = _dot(h_ref[cur], w_ref[:, g * tn:(g + 1) * tn])
        for k in range(tn // LANES):
            first = g * tn + k * LANES
            a = acc[:, k * LANES:(k + 1) * LANES]
            scale = _rope_scale(first)
            if scale is not None:
                a = a * cos_ref[cur] + pltpu.roll(a, LANES // 2, 1) * sin_ref[cur]
                if scale != 1.0:
                    a = a * scale
            o_ref[:, first:first + LANES] = a.astype(BF16)

    prepare(xn_ref[...], posn_ref[...], modn_ref[0], 1 - cur)


def _inproj(xf, pos, mod, norm_w, w_bf16, seq):
    T, D = xf.shape
    tm = min(TM_INPROJ, seq)
    tn = TN_INPROJ
    half = LANES // 2
    inv = ROPE_THETA ** (-jnp.arange(0, LANES, 2, dtype=F32) / LANES)
    inv = jnp.concatenate([inv, inv]).reshape(1, LANES)
    sgn = jnp.concatenate([-jnp.ones((half,), F32), jnp.ones((half,), F32)]).reshape(1, LANES)
    per_batch = seq // tm
    n_m = T // tm
    nxt = lambda i: jnp.minimum(i + 1, n_m - 1)
    once = dict(pipeline_mode=pl.Buffered(1))
    return pl.pallas_call(
        functools.partial(_inproj_kernel, tn=tn),
        grid=(n_m,),
        in_specs=[pl.BlockSpec((tm, D), lambda i: (0, 0), **once),
                  pl.BlockSpec((tm, 1), lambda i: (0, 0), **once),
                  pl.BlockSpec((1, 6, D), lambda i: (0, 0, 0), **once),
                  pl.BlockSpec((tm, D), lambda i: (nxt(i), 0)),
                  pl.BlockSpec((tm, 1), lambda i: (nxt(i), 0)),
                  pl.BlockSpec((1, 6, D), lambda i: (nxt(i) // per_batch, 0, 0)),
                  pl.BlockSpec((1, D), lambda i: (0, 0), **once),
                  pl.BlockSpec((1, LANES), lambda i: (0, 0), **once),
                  pl.BlockSpec((1, LANES), lambda i: (0, 0), **once),
                  pl.BlockSpec((D, D_IN), lambda i: (0, 0), **once)],
        out_specs=pl.BlockSpec((tm, D_IN), lambda i: (i, 0)),
        out_shape=jax.ShapeDtypeStruct((T, D_IN), BF16),
        scratch_shapes=[pltpu.VMEM((2, tm, D), BF16),
                        pltpu.VMEM((2, tm, LANES), F32),
                        pltpu.VMEM((2, tm, LANES), F32)],
        compiler_params=pltpu.CompilerParams(dimension_semantics=("arbitrary",),
                                             vmem_limit_bytes=VMEM_LIMIT_INPROJ_BYTES),
        name="inproj",
    )(xf, pos, mod, xf, pos, mod, norm_w.reshape(1, D), inv, sgn, w_bf16)


def _ret_kernel(q_ref, k_ref, v_ref, g_ref, dmask_ref, xi_ref, zeta_ref, cd_ref, o_ref, state_ref):
    @pl.when(pl.program_id(1) == 0)
    def _():
        state_ref[...] = jnp.zeros(state_ref.shape, F32)

    C = RET_CHUNK
    for h in range(RET_HEADS):
        qk = slice(h * RET_DK, (h + 1) * RET_DK)
        vv = slice(h * RET_DV, (h + 1) * RET_DV)
        st = state_ref[h]
        for n in range(q_ref.shape[0] // C):
            rows = slice(n * C, (n + 1) * C)
            q = q_ref[rows, qk]
            k = k_ref[rows, qk]
            v = v_ref[rows, vv]
            s = _dot_nt(q, k) * dmask_ref[h]
            qx = (q.astype(F32) * xi_ref[h]).astype(BF16)
            o = _dot(s.astype(BF16), v) + _dot(qx, st.astype(BF16))
            kz_t = (k.astype(F32) * zeta_ref[h]).T.astype(BF16)
            st = cd_ref[h] * st + _dot(kz_t, v)
            g = g_ref[rows, vv].astype(F32)
            o_ref[rows, vv] = (_rms(o) * (g * _sigmoid(g))).astype(BF16)
        state_ref[h] = st


def _retention(proj, batch, seq):
    T = proj.shape[0]
    C = RET_CHUNK
    H = RET_HEADS
    nc = seq // C
    gamma = 1.0 - jnp.exp2(-5.0 - jnp.arange(H, dtype=F32))
    log_g = jnp.log(gamma)
    idx = jnp.arange(C, dtype=F32)
    rel = idx[:, None] - idx[None, :]
    dmask = jnp.where(rel >= 0, jnp.exp(log_g[:, None, None] * jnp.maximum(rel, 0.0)), 0.0)
    zeta = jnp.exp(log_g[:, None] * (C - 1 - idx))
    xi = jnp.exp(log_g[:, None] * (idx + 1))
    cd = jnp.exp(log_g * C)
    zeta_b = jnp.broadcast_to(zeta[:, :, None], (H, C, RET_DK))
    xi_b = jnp.broadcast_to(xi[:, :, None], (H, C, RET_DK))
    cd_b = jnp.broadcast_to(cd[:, None, None], (H, 1, RET_DV))
    wq = H * RET_DK
    wv = H * RET_DV
    tr = min(TR_RETENTION, seq)
    ns = seq // tr
    row = lambda b, n: b * ns + n
    const3 = lambda b, n: (0, 0, 0)
    return pl.pallas_call(
        _ret_kernel,
        grid=(batch, ns),
        in_specs=[pl.BlockSpec((tr, wq), lambda b, n: (row(b, n), OFF_RQ // wq)),
                  pl.BlockSpec((tr, wq), lambda b, n: (row(b, n), OFF_RK // wq)),
                  pl.BlockSpec((tr, wv), lambda b, n: (row(b, n), OFF_RV // wv)),
                  pl.BlockSpec((tr, wv), lambda b, n: (row(b, n), OFF_RG // wv)),
                  pl.BlockSpec((H, C, C), const3),
                  pl.BlockSpec((H, C, RET_DK), const3),
                  pl.BlockSpec((H, C, RET_DK), const3),
                  pl.BlockSpec((H, 1, RET_DV), const3)],
        out_specs=pl.BlockSpec((tr, wv), lambda b, n: (row(b, n), 0)),
        out_shape=jax.ShapeDtypeStruct((T, wv), BF16),
        scratch_shapes=[pltpu.VMEM((H, RET_DK, RET_DV), F32)],
        compiler_params=_params("parallel", "arbitrary"),
        name="retention",
    )(proj, proj, proj, proj, dmask, xi_b, zeta_b, cd_b)


def _transpose_bf16(a):
    return a.astype(F32).T.astype(BF16)


def _attn_kernel(lam_ref, dn_ref, q1_ref, q2_ref, k1_ref, k2_ref, v_ref, o_ref,
                 vt_ref, qt_ref, s_ref, pm_ref, m_ref, l_ref, acc_ref, *, bq, bk, lambda_init):
    i = pl.program_id(2)
    n_sub = vt_ref.shape[0]

    @pl.when(i == 0)
    def _():
        for t in range(n_sub):
            vt_ref[t] = _transpose_bf16(v_ref[t * bk:(t + 1) * bk, :])

    m_ref[...] = jnp.full(m_ref.shape, NEG_BIG, F32)
    l_ref[...] = jnp.zeros(l_ref.shape, F32)
    acc_ref[...] = jnp.zeros(acc_ref.shape, F32)
    qt_ref[0] = _transpose_bf16(q1_ref[...])
    qt_ref[1] = _transpose_bf16(q2_ref[...])
    k_refs = (k1_ref, k2_ref)

    def score(t, slot, diagonal_offset=None):
        start = pl.multiple_of(t * bk, bk)
        for c in range(2):
            st = _dot(k_refs[c][pl.ds(start, bk), :], qt_ref[c])
            if diagonal_offset is not None:
                key = lax.broadcasted_iota(jnp.int32, (bk, bq), 0) + diagonal_offset
                qry = lax.broadcasted_iota(jnp.int32, (bk, bq), 1)
                st = jnp.where(key <= qry, st, NEG_BIG)
            s_ref[slot, c] = st
            pm_ref[slot, c] = jnp.max(st, axis=0, keepdims=True)

    def accumulate(t, slot):
        vt = vt_ref[t]
        for c in range(2):
            m_prev = m_ref[c]
            m_new = jnp.maximum(m_prev, pm_ref[slot, c])
            p = jnp.exp2(s_ref[slot, c] - m_new)
            alpha = jnp.exp2(m_prev - m_new)
            l_ref[c] = alpha * l_ref[c] + jnp.sum(p, axis=0, keepdims=True)
            acc_ref[c] = alpha * acc_ref[c] + _dot(vt, p.astype(BF16))
            m_ref[c] = m_new

    def pair(jj, next_is_diagonal):
        t = 2 * jj
        score(t + 1, 1)
        accumulate(t, 0)
        score(t + 2, 0, 0 if next_is_diagonal else None)
        accumulate(t + 1, 1)

    pl.when(i == 0)(lambda: score(0, 0, 0))
    pl.when(i > 0)(lambda: score(0, 0))

    def body(jj, carry):
        pair(jj, False)
        return carry

    lax.fori_loop(0, i - 1, body, 0)
    pl.when(i > 0)(lambda: pair(i - 1, True))
    score(2 * i + 1, 1, bk)
    accumulate(2 * i, 0)
    accumulate(2 * i + 1, 1)

    lam4 = lam_ref[...]
    lam = (jnp.exp(jnp.sum(lam4[0:1] * lam4[1:2], axis=1, keepdims=True))
           - jnp.exp(jnp.sum(lam4[2:3] * lam4[3:4], axis=1, keepdims=True)) + lambda_init)
    ot = acc_ref[0] * (1.0 / l_ref[0]) - lam * (acc_ref[1] * (1.0 / l_ref[1]))
    ot = ot * lax.rsqrt(jnp.mean(ot * ot, axis=0, keepdims=True) + EPS)
    o_ref[...] = (ot.T * dn_ref[...] * (1.0 - lambda_init)).astype(BF16)


def _diff_attention(proj, lam4, diff_norm, lambda_init, batch, seq):
    T = proj.shape[0]
    H = DIFF_HEADS
    bq = min(BQ_ATTN, seq)
    bk = bq // 2
    nq = seq // bq
    q0 = OFF_DQ // DIFF_DK
    k0 = OFF_DK // DIFF_DK
    v0 = OFF_DV // DIFF_DV
    return pl.pallas_call(
        functools.partial(_attn_kernel, bq=bq, bk=bk, lambda_init=lambda_init),
        grid=(batch, H, nq),
        in_specs=[pl.BlockSpec((4, DIFF_DK), lambda b, h, i: (0, 0)),
                  pl.BlockSpec((1, DIFF_DV), lambda b, h, i: (0, 0)),
                  pl.BlockSpec((bq, DIFF_DK), lambda b, h, i: (b * nq + i, q0 + 2 * h)),
                  pl.BlockSpec((bq, DIFF_DK), lambda b, h, i: (b * nq + i, q0 + 2 * h + 1)),
                  pl.BlockSpec((seq, DIFF_DK), lambda b, h, i: (b, k0 + 2 * h)),
                  pl.BlockSpec((seq, DIFF_DK), lambda b, h, i: (b, k0 + 2 * h + 1)),
                  pl.BlockSpec((seq, DIFF_DV), lambda b, h, i: (b, v0 + h))],
        out_specs=pl.BlockSpec((bq, DIFF_DV), lambda b, h, i: (b * nq + i, h)),
        out_shape=jax.ShapeDtypeStruct((T, H * DIFF_DV), BF16),
        scratch_shapes=[pltpu.VMEM((seq // bk, DIFF_DV, bk), BF16),
                        pltpu.VMEM((2, DIFF_DK, bq), BF16),
                        pltpu.VMEM((2, 2, bk, bq), F32),
                        pltpu.VMEM((2, 2, 1, bq), F32),
                        pltpu.VMEM((2, 1, bq), F32),
                        pltpu.VMEM((2, 1, bq), F32),
                        pltpu.VMEM((2, DIFF_DV, bq), F32)],
        compiler_params=_params("parallel", "parallel", "arbitrary"),
        name="diff_attn",
    )(lam4, diff_norm.reshape(1, DIFF_DV), proj, proj, proj, proj, proj)


def _post_kernel(ro_ref, do_ref, gr_ref, gd_ref, x_ref, mod_ref, modp_ref, wr_ref, wd_ref, wo_ref, nf_ref,
                 wrt_ref, brt_ref, x1_ref, h2_ref, route_ref, cnt_ref, carry_ref, x1s_ref, lower_ref, *, tm):
    i = pl.program_id(0)

    @pl.when(i == 0)
    def _():
        carry_ref[...] = jnp.zeros(carry_ref.shape, F32)
        x1s_ref[...] = jnp.zeros(x1s_ref.shape, F32)
        r_i = lax.broadcasted_iota(jnp.int32, (tm, tm), 0)
        c_i = lax.broadcasted_iota(jnp.int32, (tm, tm), 1)
        lower_ref[...] = jnp.where(c_i < r_i, 1.0, 0.0).astype(BF16)

    modp = modp_ref[0]
    h2 = _rms(x1s_ref[...]) * nf_ref[...] * (1.0 + modp[4:5]) + modp[3:4]
    h2_ref[...] = _pack_bf16_pairs(h2)
    lg = _dot(h2.astype(BF16), wrt_ref[...]) + brt_ref[...]

    mod = mod_ref[0]
    ret_out = _dot(ro_ref[...], wr_ref[...])
    diff_out = _dot(do_ref[...], wd_ref[...])
    merged = (_sigmoid(gr_ref[...].astype(F32)) * ret_out
              + _sigmoid(gd_ref[...].astype(F32)) * diff_out)
    x1 = x_ref[...] + mod[2:3] * _dot(merged.astype(BF16), wo_ref[...])
    x1_ref[...] = x1

    lane = lax.broadcasted_iota(jnp.int32, lg.shape, 1)
    far = jnp.int32(LANES)

    def top1(vals):
        best = jnp.max(vals, axis=1, keepdims=True)
        return best, jnp.min(jnp.where(vals == best, lane, far), axis=1, keepdims=True)

    is_group = (lane >= N_EXPERTS) & (lane < N_EXPERTS + N_GROUPS)
    g_best, g_lane = top1(jnp.where(is_group, lg, NEG_BIG))
    g_sum = jnp.sum(jnp.where(is_group, jnp.exp(lg - g_best), 0.0), axis=1, keepdims=True)
    g_top = 1.0 / g_sum
    first = (g_lane - N_EXPERTS) * EXPERTS_PER_GROUP
    in_group = (lane >= first) & (lane < first + EXPERTS_PER_GROUP)
    el = jnp.where(in_group, lg, NEG_BIG)
    e_a, i_a = top1(el)
    e_b, i_b = top1(jnp.where(lane == i_a, NEG_BIG, el))
    t = jnp.exp(e_b - e_a)
    w_a = g_top / (1.0 + t)
    w_b = g_top * t / (1.0 + t)

    hot_a = lane == i_a
    hot_b = lane == i_b
    hot = jnp.where((hot_a | hot_b) & (i > 0), 1.0, 0.0)
    before = _dot(lower_ref[...], hot.astype(BF16)) + carry_ref[...]
    rank_a = jnp.sum(jnp.where(hot_a, before, 0.0), axis=1, keepdims=True)
    rank_b = jnp.sum(jnp.where(hot_b, before, 0.0), axis=1, keepdims=True)
    total = carry_ref[...] + jnp.sum(hot, axis=0, keepdims=True)
    carry_ref[...] = total
    cnt_ref[...] = total

    fields = (i_a.astype(F32), i_b.astype(F32), w_a, w_b, rank_a, rank_b)
    route = jnp.zeros(lg.shape, F32)
    for n, f in enumerate(fields):
        route = jnp.where(lane == n, f, route)
    route_ref[...] = route
    x1s_ref[...] = x1


def _post_mixer(ro, do, proj, xf, mod, w_ret_o, w_diff_o, w_out, norm_ffn, w_rt, b_rt, seq):
    T, D = xf.shape
    tm = min(TM_POST, seq)
    per_batch = seq // tm
    n = T // tm
    cur = lambda i: jnp.minimum(i, n - 1)
    prev = lambda i: jnp.maximum(i - 1, 0)
    row = lambda i: (cur(i), 0)
    prow = lambda i: (prev(i), 0)
    const = lambda i: (0, 0)
    return pl.pallas_call(
        functools.partial(_post_kernel, tm=tm),
        grid=(n + 1,),
        in_specs=[pl.BlockSpec((tm, D), row),
                  pl.BlockSpec((tm, D), row),
                  pl.BlockSpec((tm, D), lambda i: (cur(i), OFF_GR // D)),
                  pl.BlockSpec((tm, D), lambda i: (cur(i), OFF_GD // D)),
                  pl.BlockSpec((tm, D), row),
                  pl.BlockSpec((1, 6, D), lambda i: (cur(i) // per_batch, 0, 0)),
                  pl.BlockSpec((1, 6, D), lambda i: (prev(i) // per_batch, 0, 0)),
                  pl.BlockSpec((D, D), const),
                  pl.BlockSpec((D, D), const),
                  pl.BlockSpec((D, D), const),
                  pl.BlockSpec((1, D), const),
                  pl.BlockSpec((D, LANES), const),
                  pl.BlockSpec((1, LANES), const)],
        out_specs=[pl.BlockSpec((tm, D), row),
                   pl.BlockSpec((tm, D // 2), prow),
                   pl.BlockSpec((tm, LANES), prow),
                   pl.BlockSpec((1, LANES), const)],
        out_shape=[jax.ShapeDtypeStruct((T, D), F32),
                   jax.ShapeDtypeStruct((T, D // 2), jnp.uint32),
                   jax.ShapeDtypeStruct((T, LANES), F32),
                   jax.ShapeDtypeStruct((1, LANES), F32)],
        scratch_shapes=[pltpu.VMEM((1, LANES), F32), pltpu.VMEM((tm, D), F32), pltpu.VMEM((tm, tm), BF16)],
        compiler_params=_params("arbitrary"),
        name="post_mixer",
    )(ro, do, proj, proj, xf, mod, mod, w_ret_o, w_diff_o, w_out, norm_ffn.reshape(1, D), w_rt, b_rt)


def _row_copy(src, s, dst, d, sem):
    return pltpu.make_async_copy(src.at[pl.ds(s, 1), :], dst.at[pl.ds(d, 1), :], sem)


def _scatter_kernel(zb_ref, dest_ref, h_ref, o_hbm, zero_ref, zsem, sem, *, ts, tb):
    i = pl.program_id(0)

    @pl.when(i == 0)
    def _():
        zero_ref[...] = jnp.zeros(zero_ref.shape, zero_ref.dtype)

        def zero_copy(n):
            start = pl.multiple_of(zb_ref[n] * tb, tb)
            return pltpu.make_async_copy(zero_ref, o_hbm.at[pl.ds(start, tb), :], zsem)

        for n in range(zb_ref.shape[0]):
            pl.when(zb_ref[n] >= 0)(lambda n=n: zero_copy(n).start())
        for n in range(zb_ref.shape[0]):
            pl.when(zb_ref[n] >= 0)(lambda n=n: zero_copy(n).wait())

    def issue(r, carry):
        for k in range(2):
            _row_copy(h_ref, r, o_hbm, dest_ref[0, 0, 2 * r + k], sem).start(priority=k)
        return carry

    lax.fori_loop(0, ts, issue, 0, unroll=True)
    pltpu.make_async_copy(o_hbm.at[pl.ds(0, 2 * ts), :], o_hbm.at[pl.ds(0, 2 * ts), :], sem).wait()


def _moe_scatter(h2, dest, zero_blocks, n_rows):
    T, D = h2.shape
    ts = min(TS_SCATTER, T)
    tb = TB_MOE
    grid_spec = pltpu.PrefetchScalarGridSpec(
        num_scalar_prefetch=1,
        grid=(T // ts,),
        in_specs=[pl.BlockSpec((1, 1, 2 * ts), lambda i, zb: (i, 0, 0), memory_space=pltpu.SMEM),
                  pl.BlockSpec((ts, D), lambda i, zb: (i, 0))],
        out_specs=pl.BlockSpec(memory_space=pl.ANY),
        scratch_shapes=[pltpu.VMEM((tb, D), h2.dtype), pltpu.SemaphoreType.DMA(()), pltpu.SemaphoreType.DMA(())],
    )
    return pl.pallas_call(
        functools.partial(_scatter_kernel, ts=ts, tb=tb),
        grid_spec=grid_spec,
        out_shape=jax.ShapeDtypeStruct((n_rows, D), h2.dtype),
        compiler_params=_params("arbitrary"),
        name="moe_scatter",
    )(zero_blocks, dest.reshape(T // ts, 1, 2 * ts), h2)


def _expert_kernel(be_ref, nv_ref, x_ref, w1_ref, w3_ref, w2_ref, o_ref, w1b_ref, w3b_ref, w2b_ref):
    n = pl.program_id(0)
    nv = nv_ref[n]
    new_expert = (n == 0) | (be_ref[n] != be_ref[jnp.maximum(n - 1, 0)])

    @pl.when(new_expert & (nv > 0))
    def _():
        w1b_ref[...] = w1_ref[0].astype(BF16)
        w3b_ref[...] = w3_ref[0].astype(BF16)
        w2b_ref[...] = w2_ref[0].astype(BF16)

    @pl.when(nv > 0)
    def _():
        x_lo, x_hi = _unpack_bf16_pairs(x_ref[...])
        half = x_lo.shape[1]
        g = _dot(x_lo, w1b_ref[:half, :]) + _dot(x_hi, w1b_ref[half:, :])
        u = _dot(x_lo, w3b_ref[:half, :]) + _dot(x_hi, w3b_ref[half:, :])
        a = (g * _sigmoid(g) * u).astype(BF16)
        o_ref[...] = _pack_bf16_pairs(_dot(a, w2b_ref[...]))

    @pl.when(nv == 0)
    def _():
        o_ref[...] = jnp.zeros(o_ref.shape, o_ref.dtype)


def _moe_experts(h_pad, block_e, block_nv, w1, w3, w2):
    P = h_pad.shape[0]
    D = 2 * h_pad.shape[1]
    tb = TB_MOE
    wmap = lambda n, be, nv: (be[n], 0, 0)
    grid_spec = pltpu.PrefetchScalarGridSpec(
        num_scalar_prefetch=2,
        grid=(P // tb,),
        in_specs=[pl.BlockSpec((tb, D // 2), lambda n, be, nv: (n, 0)),
                  pl.BlockSpec((1, D, D_EXPERT), wmap),
                  pl.BlockSpec((1, D, D_EXPERT), wmap),
                  pl.BlockSpec((1, D_EXPERT, D), wmap)],
        out_specs=pl.BlockSpec((tb, D // 2), lambda n, be, nv: (n, 0)),
        scratch_shapes=[pltpu.VMEM((D, D_EXPERT), BF16), pltpu.VMEM((D, D_EXPERT), BF16),
                        pltpu.VMEM((D_EXPERT, D), BF16)],
    )
    return pl.pallas_call(
        _expert_kernel,
        grid_spec=grid_spec,
        out_shape=jax.ShapeDtypeStruct((P, D // 2), h_pad.dtype),
        compiler_params=_params("arbitrary"),
        name="moe_experts",
    )(block_e, block_nv, h_pad, w1, w3, w2)


def _combine_kernel(dcur_ref, dnext_ref, x1_ref, route_ref, mod_ref, nw_ref, y_hbm, o_ref, buf, sem, *, tc, final):
    i = pl.program_id(0)
    slot = i % 2

    def gather(d_ref, s):
        def issue(r, carry):
            for k in range(2):
                _row_copy(y_hbm, d_ref[0, 0, 2 * r + k], buf.at[s], k * tc + r, sem.at[s]).start(priority=k)
            return carry

        lax.fori_loop(0, tc, issue, 0, unroll=True)

    pl.when(i == 0)(lambda: gather(dcur_ref, 0))
    pl.when(i + 1 < pl.num_programs(0))(lambda: gather(dnext_ref, 1 - slot))
    pltpu.make_async_copy(y_hbm.at[pl.ds(0, 2 * tc), :], buf.at[slot], sem.at[slot]).wait()
    route = route_ref[...]

    def rows(first):
        lo, hi = _unpack_bf16_pairs(buf[slot, first:first + tc, :])
        return jnp.concatenate([lo, hi], axis=1).astype(F32)

    y = route[:, 2:3] * rows(0) + route[:, 3:4] * rows(tc)
    x2 = x1_ref[...] + mod_ref[0][5:6] * y
    o_ref[...] = _rms(x2) * nw_ref[...] if final else x2


def _moe_combine(y_pad, dest, x1, route, mod, norm_w, seq, final):
    T, D = x1.shape
    tc = min(TC_COMBINE, seq)
    per_batch = seq // tc
    row = lambda i: (i, 0)
    n = T // tc
    dest3 = dest.reshape(n, 1, 2 * tc)
    return pl.pallas_call(
        functools.partial(_combine_kernel, tc=tc, final=final),
        grid=(n,),
        in_specs=[pl.BlockSpec((1, 1, 2 * tc), lambda i: (i, 0, 0), memory_space=pltpu.SMEM),
                  pl.BlockSpec((1, 1, 2 * tc), lambda i: (jnp.minimum(i + 1, n - 1), 0, 0),
                               memory_space=pltpu.SMEM),
                  pl.BlockSpec((tc, D), row),
                  pl.BlockSpec((tc, LANES), row),
                  pl.BlockSpec((1, 6, D), lambda i: (i // per_batch, 0, 0)),
                  pl.BlockSpec((1, D), lambda i: (0, 0)),
                  pl.BlockSpec(memory_space=pl.ANY)],
        out_specs=pl.BlockSpec((tc, D), row),
        out_shape=jax.ShapeDtypeStruct((T, D), F32),
        scratch_shapes=[pltpu.VMEM((2, 2 * tc, D // 2), y_pad.dtype), pltpu.SemaphoreType.DMA((2,))],
        compiler_params=_params("arbitrary"),
        name="moe_combine",
    )(dest3, dest3, x1, route, mod, norm_w.reshape(1, D), y_pad)


def _routing_tables(route, counts, n_blocks):
    tb = TB_MOE
    expert = route[:, 0:2].astype(jnp.int32)
    rank = route[:, 4:6].astype(jnp.int32)
    cnt = counts[0, :N_EXPERTS].astype(jnp.int32)
    nblk = (cnt + tb - 1) // tb
    blk_end = jnp.cumsum(nblk)
    blk_start = blk_end - nblk
    hot = expert[:, :, None] == jnp.arange(N_EXPERTS, dtype=jnp.int32)
    dest = jnp.sum(jnp.where(hot, blk_start * tb, 0), axis=-1) + rank
    blocks = jnp.arange(n_blocks, dtype=jnp.int32)
    block_e = jnp.minimum(jnp.sum(blocks[:, None] >= blk_end[None, :], axis=1), N_EXPERTS - 1).astype(jnp.int32)
    left = cnt[block_e] - (blocks - blk_start[block_e]) * tb
    block_nv = jnp.where(blocks < blk_end[-1], jnp.clip(left, 0, tb), 0).astype(jnp.int32)
    tail = blk_end[-1] + jnp.arange(N_EXPERTS, dtype=jnp.int32)
    zero_blocks = jnp.concatenate([jnp.where(nblk > 0, blk_end - 1, -1),
                                   jnp.where(tail < n_blocks, tail, -1)]).astype(jnp.int32)
    return dest.astype(jnp.int32), block_e, block_nv, zero_blocks


def kernel(x, c, positions, w_ada, b_ada, norm_mix, w_in, w_ret_o, w_diff_o, lam_q1, lam_k1, lam_q2, lam_k2,
           diff_norm, w_out, norm_ffn, w_router_group, b_router_group, w_router_expert, b_router_expert,
           w_exp_gate, w_exp_up, w_exp_down, norm_final):
    B, S, D = x.shape
    T = B * S
    depth = w_ada.shape[0]
    assert D_IN == w_in.shape[2] and S % RET_CHUNK == 0
    xf = x.reshape(T, D)
    pos = positions.reshape(T, 1)
    n_blocks = (2 * T) // TB_MOE + N_EXPERTS
    for l in range(depth):
        lambda_init = 0.8 - 0.6 * math.exp(-0.3 * l)
        mod = _adaln(c, w_ada[l], b_ada[l]).reshape(B, 6, D)
        proj = _inproj(xf, pos, mod, norm_mix[l], w_in[l].astype(BF16), S)
        ro = _retention(proj, B, S)
        lam4 = jnp.stack([lam_q1[l], lam_k1[l], lam_q2[l], lam_k2[l]]).astype(F32)
        do = _diff_attention(proj, lam4, diff_norm[l].astype(F32), lambda_init, B, S)

        pad = LANES - N_EXPERTS - N_GROUPS
        w_rt = jnp.concatenate([w_router_expert[l], w_router_group[l], jnp.zeros((D, pad), F32)], axis=1)
        b_rt = jnp.concatenate([b_router_expert[l], b_router_group[l], jnp.zeros((pad,), F32)]).reshape(1, LANES)
        x1, h2, route, counts = _post_mixer(
            ro, do, proj, xf, mod, w_ret_o[l].astype(BF16), w_diff_o[l].astype(BF16), w_out[l].astype(BF16),
            norm_ffn[l], w_rt.astype(BF16), b_rt, S)

        dest, block_e, block_nv, zero_blocks = _routing_tables(route, counts, n_blocks)
        h_pad = _moe_scatter(h2, dest.reshape(-1), zero_blocks, n_blocks * TB_MOE)
        y_pad = _moe_experts(h_pad, block_e, block_nv, w_exp_gate[l], w_exp_up[l], w_exp_down[l])
        xf = _moe_combine(y_pad, dest.reshape(-1), x1, route, mod, norm_final, S, l == depth - 1)
    return xf.reshape(B, S, D)
```

```python
import functools
import math

import jax
import jax.numpy as jnp
from jax import lax
from jax.experimental import pallas as pl
from jax.experimental.pallas import tpu as pltpu

F32 = jnp.float32
BF16 = jnp.bfloat16

EPS = 1e-6
ROPE_THETA = 10000.0
LANES = 128
RET_HEADS, RET_DK, RET_DV, RET_CHUNK = 4, 128, 256, 128
DIFF_HEADS, DIFF_DK, DIFF_DV = 4, 128, 256
N_GROUPS, EXPERTS_PER_GROUP, N_EXPERTS, D_EXPERT = 4, 8, 32, 512
OFF_RQ, OFF_RK, OFF_RV, OFF_RG, OFF_DQ, OFF_DK, OFF_DV, OFF_GR, OFF_GD, D_IN = (
    0, 512, 1024, 2048, 3072, 4096, 5120, 6144, 7168, 8192)
NEG_BIG = -1e30
VMEM_LIMIT_BYTES = 48 * 1024 * 1024
VMEM_LIMIT_INPROJ_BYTES = 56 * 1024 * 1024

TM_INPROJ, TN_INPROJ = 512, 1024
TR_RETENTION = 512
BQ_ATTN = 512
HEADS_PER_ATTN_STEP = 2
TM_POST = 512
TB_MOE = 256
TS_SCATTER = 256
TC_COMBINE = 256


def _params(*sem):
    return pltpu.CompilerParams(dimension_semantics=sem, vmem_limit_bytes=VMEM_LIMIT_BYTES)


def _sigmoid(v):
    return 1.0 / (1.0 + jnp.exp(-v))


def _rms(v):
    return v * lax.rsqrt(jnp.mean(v * v, axis=-1, keepdims=True) + EPS)


def _dot(a, b):
    return jnp.dot(a, b, preferred_element_type=F32)


def _dot_nt(a, b):
    return lax.dot_general(a, b, (((1,), (1,)), ((), ())), preferred_element_type=F32)


def _pack_bf16_pairs(x):
    m = x.shape[1] // 2

    def rounded_bits(v):
        return lax.bitcast_convert_type(v.astype(BF16).astype(F32), jnp.uint32)

    return (rounded_bits(x[:, :m]) >> 16) | rounded_bits(x[:, m:])


def _unpack_bf16_pairs(w):
    lo = lax.bitcast_convert_type(w << 16, F32)
    hi = lax.bitcast_convert_type(w & jnp.uint32(0xFFFF0000), F32)
    return lo.astype(BF16), hi.astype(BF16)


def _ada_kernel(c_ref, w_ref, b_ref, o_ref):
    c = c_ref[...]
    a = (c * _sigmoid(c)).astype(BF16)
    o_ref[...] = _dot(a, w_ref[...].astype(BF16)) + b_ref[...]


def _adaln(c, w, b):
    B, D = c.shape
    n = w.shape[1] // D
    return pl.pallas_call(
        _ada_kernel,
        grid=(n,),
        in_specs=[pl.BlockSpec((B, D), lambda j: (0, 0)),
                  pl.BlockSpec((D, D), lambda j: (0, j)),
                  pl.BlockSpec((1, D), lambda j: (0, j))],
        out_specs=pl.BlockSpec((B, D), lambda j: (0, j)),
        out_shape=jax.ShapeDtypeStruct((B, n * D), F32),
        compiler_params=_params("parallel"),
        name="adaln",
    )(c, w, b.reshape(1, -1))


ROPE_GROUPS = ((OFF_RQ, OFF_RK, 1.0), (OFF_RK, OFF_RV, RET_DK ** -0.5),
               (OFF_DQ, OFF_DK, DIFF_DK ** -0.5 * math.log2(math.e)), (OFF_DK, OFF_DV, 1.0))


def _rope_scale(col):
    for lo, hi, scale in ROPE_GROUPS:
        if lo <= col < hi:
            return scale
    return None


def _inproj_kernel(x0_ref, pos0_ref, mod0_ref, xn_ref, posn_ref, modn_ref, nw_ref, inv_ref, sgn_ref,
                   w_ref, o_ref, h_ref, cos_ref, sin_ref, *, tn):
    i = pl.program_id(0)

    def prepare(x, pos, mod, slot):
        h = _rms(x) * nw_ref[...] * (1.0 + mod[1:2]) + mod[0:1]
        h_ref[slot] = h.astype(BF16)
        ang = pos.astype(F32) * inv_ref[...]
        cos_ref[slot] = jnp.cos(ang)
        sin_ref[slot] = jnp.sin(ang) * sgn_ref[...]

    @pl.when(i == 0)
    def _():
        prepare(x0_ref[...], pos0_ref[...], mod0_ref[0], 0)

    cur = i % 2
    for g in range(o_ref.shape[1] // tn):
        acc = _dot(h_ref[cur], w_ref[:, g * tn:(g + 1) * tn])
        for k in range(tn // LANES):
            first = g * tn + k * LANES
            a = acc[:, k * LANES:(k + 1) * LANES]
            scale = _rope_scale(first)
            if scale is not None:
                a = a * cos_ref[cur] + pltpu.roll(a, LANES // 2, 1) * sin_ref[cur]
                if scale != 1.0:
                    a = a * scale
            o_ref[:, first:first + LANES] = a.astype(BF16)

    prepare(xn_ref[...], posn_ref[...], modn_ref[0], 1 - cur)


def _inproj(xf, pos, mod, norm_w, w_bf16, seq):
    T, D = xf.shape
    tm = min(TM_INPROJ, seq)
    tn = TN_INPROJ
    half = LANES // 2
    inv = ROPE_THETA ** (-jnp.arange(0, LANES, 2, dtype=F32) / LANES)
    inv = jnp.concatenate([inv, inv]).reshape(1, LANES)
    sgn = jnp.concatenate([-jnp.ones((half,), F32), jnp.ones((half,), F32)]).reshape(1, LANES)
    per_batch = seq // tm
    n_m = T // tm
    nxt = lambda i: jnp.minimum(i + 1, n_m - 1)
    once = dict(pipeline_mode=pl.Buffered(1))
    return pl.pallas_call(
        functools.partial(_inproj_kernel, tn=tn),
        grid=(n_m,),
        in_specs=[pl.BlockSpec((tm, D), lambda i: (0, 0), **once),
                  pl.BlockSpec((tm, 1), lambda i: (0, 0), **once),
                  pl.BlockSpec((1, 6, D), lambda i: (0, 0, 0), **once),
                  pl.BlockSpec((tm, D), lambda i: (nxt(i), 0)),
                  pl.BlockSpec((tm, 1), lambda i: (nxt(i), 0)),
                  pl.BlockSpec((1, 6, D), lambda i: (nxt(i) // per_batch, 0, 0)),
                  pl.BlockSpec((1, D), lambda i: (0, 0), **once),
                  pl.BlockSpec((1, LANES), lambda i: (0, 0), **once),
                  pl.BlockSpec((1, LANES), lambda i: (0, 0), **once),
                  pl.BlockSpec((D, D_IN), lambda i: (0, 0), **once)],
        out_specs=pl.BlockSpec((tm, D_IN), lambda i: (i, 0)),
        out_shape=jax.ShapeDtypeStruct((T, D_IN), BF16),
        scratch_shapes=[pltpu.VMEM((2, tm, D), BF16),
                        pltpu.VMEM((2, tm, LANES), F32),
                        pltpu.VMEM((2, tm, LANES), F32)],
        compiler_params=pltpu.CompilerParams(dimension_semantics=("arbitrary",),
                                             vmem_limit_bytes=VMEM_LIMIT_INPROJ_BYTES),
        name="inproj",
    )(xf, pos, mod, xf, pos, mod, norm_w.reshape(1, D), inv, sgn, w_bf16)


def _ret_kernel(q_ref, k_ref, v_ref, g_ref, dmask_ref, xi_ref, zeta_ref, cd_ref, o_ref, state_ref):
    @pl.when(pl.program_id(1) == 0)
    def _():
        state_ref[...] = jnp.zeros(state_ref.shape, F32)

    C = RET_CHUNK
    for h in range(RET_HEADS):
        qk = slice(h * RET_DK, (h + 1) * RET_DK)
        vv = slice(h * RET_DV, (h + 1) * RET_DV)
        st = state_ref[h]
        for n in range(q_ref.shape[0] // C):
            rows = slice(n * C, (n + 1) * C)
            q = q_ref[rows, qk]
            k = k_ref[rows, qk]
            v = v_ref[rows, vv]
            s = _dot_nt(q, k) * dmask_ref[h]
            qx = (q.astype(F32) * xi_ref[h]).astype(BF16)
            o = _dot(s.astype(BF16), v) + _dot(qx, st.astype(BF16))
            kz_t = (k.astype(F32) * zeta_ref[h]).T.astype(BF16)
            st = cd_ref[h] * st + _dot(kz_t, v)
            g = g_ref[rows, vv].astype(F32)
            o_ref[rows, vv] = (_rms(o) * (g * _sigmoid(g))).astype(BF16)
        state_ref[h] = st


def _retention(proj, batch, seq):
    T = proj.shape[0]
    C = RET_CHUNK
    H = RET_HEADS
    nc = seq // C
    gamma = 1.0 - jnp.exp2(-5.0 - jnp.arange(H, dtype=F32))
    log_g = jnp.log(gamma)
    idx = jnp.arange(C, dtype=F32)
    rel = idx[:, None] - idx[None, :]
    dmask = jnp.where(rel >= 0, jnp.exp(log_g[:, None, None] * jnp.maximum(rel, 0.0)), 0.0)
    zeta = jnp.exp(log_g[:, None] * (C - 1 - idx))
    xi = jnp.exp(log_g[:, None] * (idx + 1))
    cd = jnp.exp(log_g * C)
    zeta_b = jnp.broadcast_to(zeta[:, :, None], (H, C, RET_DK))
    xi_b = jnp.broadcast_to(xi[:, :, None], (H, C, RET_DK))
    cd_b = jnp.broadcast_to(cd[:, None, None], (H, 1, RET_DV))
    wq = H * RET_DK
    wv = H * RET_DV
    tr = min(TR_RETENTION, seq)
    ns = seq // tr
    row = lambda b, n: b * ns + n
    const3 = lambda b, n: (0, 0, 0)
    return pl.pallas_call(
        _ret_kernel,
        grid=(batch, ns),
        in_specs=[pl.BlockSpec((tr, wq), lambda b, n: (row(b, n), OFF_RQ // wq)),
                  pl.BlockSpec((tr, wq), lambda b, n: (row(b, n), OFF_RK // wq)),
                  pl.BlockSpec((tr, wv), lambda b, n: (row(b, n), OFF_RV // wv)),
                  pl.BlockSpec((tr, wv), lambda b, n: (row(b, n), OFF_RG // wv)),
                  pl.BlockSpec((H, C, C), const3),
                  pl.BlockSpec((H, C, RET_DK), const3),
                  pl.BlockSpec((H, C, RET_DK), const3),
                  pl.BlockSpec((H, 1, RET_DV), const3)],
        out_specs=pl.BlockSpec((tr, wv), lambda b, n: (row(b, n), 0)),
        out_shape=jax.ShapeDtypeStruct((T, wv), BF16),
        scratch_shapes=[pltpu.VMEM((H, RET_DK, RET_DV), F32)],
        compiler_params=_params("parallel", "arbitrary"),
        name="retention",
    )(proj, proj, proj, proj, dmask, xi_b, zeta_b, cd_b)


def _transpose_bf16(a):
    return a.astype(F32).T.astype(BF16)


def _attn_kernel(lam_ref, dn_ref, *refs, bq, bk, heads, lambda_init):
    in_refs, (o_ref, vt_ref, qt_ref, s_ref, pm_ref, m_ref, l_ref, acc_ref) = refs[:5 * heads], refs[5 * heads:]
    q_refs = [in_refs[5 * h + c] for h in range(heads) for c in range(2)]
    k_refs = [in_refs[5 * h + 2 + c] for h in range(heads) for c in range(2)]
    v_refs = [in_refs[5 * h + 4] for h in range(heads)]
    streams = range(2 * heads)
    i = pl.program_id(2)
    n_sub = vt_ref.shape[1]

    @pl.when(i == 0)
    def _():
        for h in range(heads):
            for t in range(n_sub):
                vt_ref[h, t] = _transpose_bf16(v_refs[h][t * bk:(t + 1) * bk, :])

    m_ref[...] = jnp.full(m_ref.shape, NEG_BIG, F32)
    l_ref[...] = jnp.zeros(l_ref.shape, F32)
    acc_ref[...] = jnp.zeros(acc_ref.shape, F32)
    for n in streams:
        qt_ref[n] = _transpose_bf16(q_refs[n][...])

    def score(t, slot, diagonal_offset=None):
        start = pl.multiple_of(t * bk, bk)
        for n in streams:
            st = _dot(k_refs[n][pl.ds(start, bk), :], qt_ref[n])
            if diagonal_offset is not None:
                key = lax.broadcasted_iota(jnp.int32, (bk, bq), 0) + diagonal_offset
                qry = lax.broadcasted_iota(jnp.int32, (bk, bq), 1)
                st = jnp.where(key <= qry, st, NEG_BIG)
            s_ref[slot, n] = st
            pm_ref[slot, n] = jnp.max(st, axis=0, keepdims=True)

    def accumulate(t, slot):
        for n in streams:
            m_prev = m_ref[n]
            m_new = jnp.maximum(m_prev, pm_ref[slot, n])
            p = jnp.exp2(s_ref[slot, n] - m_new)
            alpha = jnp.exp2(m_prev - m_new)
            l_ref[n] = alpha * l_ref[n] + jnp.sum(p, axis=0, keepdims=True)
            acc_ref[n] = alpha * acc_ref[n] + _dot(vt_ref[n // 2, t], p.astype(BF16))
            m_ref[n] = m_new

    def pair(jj, next_is_diagonal):
        t = 2 * jj
        score(t + 1, 1)
        accumulate(t, 0)
        score(t + 2, 0, 0 if next_is_diagonal else None)
        accumulate(t + 1, 1)

    pl.when(i == 0)(lambda: score(0, 0, 0))
    pl.when(i > 0)(lambda: score(0, 0))

    def body(jj, carry):
        pair(jj, False)
        return carry

    lax.fori_loop(0, i - 1, body, 0)
    pl.when(i > 0)(lambda: pair(i - 1, True))
    score(2 * i + 1, 1, bk)
    accumulate(2 * i, 0)
    accumulate(2 * i + 1, 1)

    lam4 = lam_ref[...]
    lam = (jnp.exp(jnp.sum(lam4[0:1] * lam4[1:2], axis=1, keepdims=True))
           - jnp.exp(jnp.sum(lam4[2:3] * lam4[3:4], axis=1, keepdims=True)) + lambda_init)
    dv = acc_ref.shape[1]
    for h in range(heads):
        a, b = 2 * h, 2 * h + 1
        ot = acc_ref[a] * (1.0 / l_ref[a]) - lam * (acc_ref[b] * (1.0 / l_ref[b]))
        ot = ot * lax.rsqrt(jnp.mean(ot * ot, axis=0, keepdims=True) + EPS)
        o_ref[:, h * dv:(h + 1) * dv] = (ot.T * dn_ref[...] * (1.0 - lambda_init)).astype(BF16)


def _diff_attention(proj, lam4, diff_norm, lambda_init, batch, seq):
    T = proj.shape[0]
    H = DIFF_HEADS
    hp = HEADS_PER_ATTN_STEP
    bq = min(BQ_ATTN, seq)
    bk = bq // 2
    nq = seq // bq
    q0 = OFF_DQ // DIFF_DK
    k0 = OFF_DK // DIFF_DK
    v0 = OFF_DV // DIFF_DV
    head_specs = []
    for j in range(hp):
        head = lambda g, j=j: hp * g + j
        head_specs += [
            pl.BlockSpec((bq, DIFF_DK), lambda b, g, i, head=head: (b * nq + i, q0 + 2 * head(g))),
            pl.BlockSpec((bq, DIFF_DK), lambda b, g, i, head=head: (b * nq + i, q0 + 2 * head(g) + 1)),
            pl.BlockSpec((seq, DIFF_DK), lambda b, g, i, head=head: (b, k0 + 2 * head(g))),
            pl.BlockSpec((seq, DIFF_DK), lambda b, g, i, head=head: (b, k0 + 2 * head(g) + 1)),
            pl.BlockSpec((seq, DIFF_DV), lambda b, g, i, head=head: (b, v0 + head(g)))]
    ns = 2 * hp
    return pl.pallas_call(
        functools.partial(_attn_kernel, bq=bq, bk=bk, heads=hp, lambda_init=lambda_init),
        grid=(batch, H // hp, nq),
        in_specs=[pl.BlockSpec((4, DIFF_DK), lambda b, g, i: (0, 0)),
                  pl.BlockSpec((1, DIFF_DV), lambda b, g, i: (0, 0))] + head_specs,
        out_specs=pl.BlockSpec((bq, hp * DIFF_DV), lambda b, g, i: (b * nq + i, g)),
        out_shape=jax.ShapeDtypeStruct((T, H * DIFF_DV), BF16),
        scratch_shapes=[pltpu.VMEM((hp, seq // bk, DIFF_DV, bk), BF16),
                        pltpu.VMEM((ns, DIFF_DK, bq), BF16),
                        pltpu.VMEM((2, ns, bk, bq), F32),
                        pltpu.VMEM((2, ns, 1, bq), F32),
                        pltpu.VMEM((ns, 1, bq), F32),
                        pltpu.VMEM((ns, 1, bq), F32),
                        pltpu.VMEM((ns, DIFF_DV, bq), F32)],
        compiler_params=_params("parallel", "parallel", "arbitrary"),
        name="diff_attn",
    )(lam4, diff_norm.reshape(1, DIFF_DV), *([proj] * (5 * hp)))


def _post_kernel(ro_ref, do_ref, gr_ref, gd_ref, x_ref, mod_ref, modp_ref, wr_ref, wd_ref, wo_ref, nf_ref,
                 wrt_ref, brt_ref, x1_ref, h2_ref, route_ref, cnt_ref, carry_ref, x1s_ref, lower_ref, *, tm):
    i = pl.program_id(0)

    @pl.when(i == 0)
    def _():
        carry_ref[...] = jnp.zeros(carry_ref.shape, F32)
        x1s_ref[...] = jnp.zeros(x1s_ref.shape, F32)
        r_i = lax.broadcasted_iota(jnp.int32, (tm, tm), 0)
        c_i = lax.broadcasted_iota(jnp.int32, (tm, tm), 1)
        lower_ref[...] = jnp.where(c_i < r_i, 1.0, 0.0).astype(BF16)

    modp = modp_ref[0]
    h2 = _rms(x1s_ref[...]) * nf_ref[...] * (1.0 + modp[4:5]) + modp[3:4]
    h2_ref[...] = _pack_bf16_pairs(h2)
    lg = _dot(h2.astype(BF16), wrt_ref[...]) + brt_ref[...]

    mod = mod_ref[0]
    ret_out = _dot(ro_ref[...], wr_ref[...])
    diff_out = _dot(do_ref[...], wd_ref[...])
    merged = (_sigmoid(gr_ref[...].astype(F32)) * ret_out
              + _sigmoid(gd_ref[...].astype(F32)) * diff_out)
    x1 = x_ref[...] + mod[2:3] * _dot(merged.astype(BF16), wo_ref[...])
    x1_ref[...] = x1

    lane = lax.broadcasted_iota(jnp.int32, lg.shape, 1)
    far = jnp.int32(LANES)

    def top1(vals):
        best = jnp.max(vals, axis=1, keepdims=True)
        return best, jnp.min(jnp.where(vals == best, lane, far), axis=1, keepdims=True)

    is_group = (lane >= N_EXPERTS) & (lane < N_EXPERTS + N_GROUPS)
    g_best, g_lane = top1(jnp.where(is_group, lg, NEG_BIG))
    g_sum = jnp.sum(jnp.where(is_group, jnp.exp(lg - g_best), 0.0), axis=1, keepdims=True)
    g_top = 1.0 / g_sum
    first = (g_lane - N_EXPERTS) * EXPERTS_PER_GROUP
    in_group = (lane >= first) & (lane < first + EXPERTS_PER_GROUP)
    el = jnp.where(in_group, lg, NEG_BIG)
    e_a, i_a = top1(el)
    e_b, i_b = top1(jnp.where(lane == i_a, NEG_BIG, el))
    t = jnp.exp(e_b - e_a)
    w_a = g_top / (1.0 + t)
    w_b = g_top * t / (1.0 + t)

    hot_a = lane == i_a
    hot_b = lane == i_b
    hot = jnp.where((hot_a | hot_b) & (i > 0), 1.0, 0.0)
    before = _dot(lower_ref[...], hot.astype(BF16)) + carry_ref[...]
    rank_a = jnp.sum(jnp.where(hot_a, before, 0.0), axis=1, keepdims=True)
    rank_b = jnp.sum(jnp.where(hot_b, before, 0.0), axis=1, keepdims=True)
    total = carry_ref[...] + jnp.sum(hot, axis=0, keepdims=True)
    carry_ref[...] = total
    cnt_ref[...] = total

    fields = (i_a.astype(F32), i_b.astype(F32), w_a, w_b, rank_a, rank_b)
    route = jnp.zeros(lg.shape, F32)
    for n, f in enumerate(fields):
        route = jnp.where(lane == n, f, route)
    route_ref[...] = route
    x1s_ref[...] = x1


def _post_mixer(ro, do, proj, xf, mod, w_ret_o, w_diff_o, w_out, norm_ffn, w_rt, b_rt, seq):
    T, D = xf.shape
    tm = min(TM_POST, seq)
    per_batch = seq // tm
    n = T // tm
    cur = lambda i: jnp.minimum(i, n - 1)
    prev = lambda i: jnp.maximum(i - 1, 0)
    row = lambda i: (cur(i), 0)
    prow = lambda i: (prev(i), 0)
    const = lambda i: (0, 0)
    return pl.pallas_call(
        functools.partial(_post_kernel, tm=tm),
        grid=(n + 1,),
        in_specs=[pl.BlockSpec((tm, D), row),
                  pl.BlockSpec((tm, D), row),
                  pl.BlockSpec((tm, D), lambda i: (cur(i), OFF_GR // D)),
                  pl.BlockSpec((tm, D), lambda i: (cur(i), OFF_GD // D)),
                  pl.BlockSpec((tm, D), row),
                  pl.BlockSpec((1, 6, D), lambda i: (cur(i) // per_batch, 0, 0)),
                  pl.BlockSpec((1, 6, D), lambda i: (prev(i) // per_batch, 0, 0)),
                  pl.BlockSpec((D, D), const),
                  pl.BlockSpec((D, D), const),
                  pl.BlockSpec((D, D), const),
                  pl.BlockSpec((1, D), const),
                  pl.BlockSpec((D, LANES), const),
                  pl.BlockSpec((1, LANES), const)],
        out_specs=[pl.BlockSpec((tm, D), row),
                   pl.BlockSpec((tm, D // 2), prow),
                   pl.BlockSpec((tm, LANES), prow),
                   pl.BlockSpec((1, LANES), const)],
        out_shape=[jax.ShapeDtypeStruct((T, D), F32),
                   jax.ShapeDtypeStruct((T, D // 2), jnp.uint32),
                   jax.ShapeDtypeStruct((T, LANES), F32),
                   jax.ShapeDtypeStruct((1, LANES), F32)],
        scratch_shapes=[pltpu.VMEM((1, LANES), F32), pltpu.VMEM((tm, D), F32), pltpu.VMEM((tm, tm), BF16)],
        compiler_params=_params("arbitrary"),
        name="post_mixer",
    )(ro, do, proj, proj, xf, mod, mod, w_ret_o, w_diff_o, w_out, norm_ffn.reshape(1, D), w_rt, b_rt)


def _row_copy(src, s, dst, d, sem):
    return pltpu.make_async_copy(src.at[pl.ds(s, 1), :], dst.at[pl.ds(d, 1), :], sem)


def _scatter_kernel(zb_ref, dest_ref, h_ref, o_hbm, zero_ref, zsem, sem, *, ts, tb):
    i = pl.program_id(0)

    @pl.when(i == 0)
    def _():
        zero_ref[...] = jnp.zeros(zero_ref.shape, zero_ref.dtype)

        def zero_copy(n):
            start = pl.multiple_of(zb_ref[n] * tb, tb)
            return pltpu.make_async_copy(zero_ref, o_hbm.at[pl.ds(start, tb), :], zsem)

        for n in range(zb_ref.shape[0]):
            pl.when(zb_ref[n] >= 0)(lambda n=n: zero_copy(n).start())
        for n in range(zb_ref.shape[0]):
            pl.when(zb_ref[n] >= 0)(lambda n=n: zero_copy(n).wait())

    def issue(r, carry):
        for k in range(2):
            _row_copy(h_ref, r, o_hbm, dest_ref[0, 0, 2 * r + k], sem).start(priority=k)
        return carry

    lax.fori_loop(0, ts, issue, 0, unroll=True)
    pltpu.make_async_copy(o_hbm.at[pl.ds(0, 2 * ts), :], o_hbm.at[pl.ds(0, 2 * ts), :], sem).wait()


def _moe_scatter(h2, dest, zero_blocks, n_rows):
    T, D = h2.shape
    ts = min(TS_SCATTER, T)
    tb = TB_MOE
    grid_spec = pltpu.PrefetchScalarGridSpec(
        num_scalar_prefetch=1,
        grid=(T // ts,),
        in_specs=[pl.BlockSpec((1, 1, 2 * ts), lambda i, zb: (i, 0, 0), memory_space=pltpu.SMEM),
                  pl.BlockSpec((ts, D), lambda i, zb: (i, 0))],
        out_specs=pl.BlockSpec(memory_space=pl.ANY),
        scratch_shapes=[pltpu.VMEM((tb, D), h2.dtype), pltpu.SemaphoreType.DMA(()), pltpu.SemaphoreType.DMA(())],
    )
    return pl.pallas_call(
        functools.partial(_scatter_kernel, ts=ts, tb=tb),
        grid_spec=grid_spec,
        out_shape=jax.ShapeDtypeStruct((n_rows, D), h2.dtype),
        compiler_params=_params("arbitrary"),
        name="moe_scatter",
    )(zero_blocks, dest.reshape(T // ts, 1, 2 * ts), h2)


def _expert_kernel(be_ref, nv_ref, x_ref, w1_ref, w3_ref, w2_ref, o_ref, w1b_ref, w3b_ref, w2b_ref):
    n = pl.program_id(0)
    nv = nv_ref[n]
    new_expert = (n == 0) | (be_ref[n] != be_ref[jnp.maximum(n - 1, 0)])

    @pl.when(new_expert & (nv > 0))
    def _():
        w1b_ref[...] = w1_ref[0].astype(BF16)
        w3b_ref[...] = w3_ref[0].astype(BF16)
        w2b_ref[...] = w2_ref[0].astype(BF16)

    @pl.when(nv > 0)
    def _():
        x_lo, x_hi = _unpack_bf16_pairs(x_ref[...])
        half = x_lo.shape[1]
        g = _dot(x_lo, w1b_ref[:half, :]) + _dot(x_hi, w1b_ref[half:, :])
        u = _dot(x_lo, w3b_ref[:half, :]) + _dot(x_hi, w3b_ref[half:, :])
        a = (g * _sigmoid(g) * u).astype(BF16)
        o_ref[...] = _pack_bf16_pairs(_dot(a, w2b_ref[...]))

    @pl.when(nv == 0)
    def _():
        o_ref[...] = jnp.zeros(o_ref.shape, o_ref.dtype)


def _moe_experts(h_pad, block_e, block_nv, w1, w3, w2):
    P = h_pad.shape[0]
    D = 2 * h_pad.shape[1]
    tb = TB_MOE
    wmap = lambda n, be, nv: (be[n], 0, 0)
    grid_spec = pltpu.PrefetchScalarGridSpec(
        num_scalar_prefetch=2,
        grid=(P // tb,),
        in_specs=[pl.BlockSpec((tb, D // 2), lambda n, be, nv: (n, 0)),
                  pl.BlockSpec((1, D, D_EXPERT), wmap),
                  pl.BlockSpec((1, D, D_EXPERT), wmap),
                  pl.BlockSpec((1, D_EXPERT, D), wmap)],
        out_specs=pl.BlockSpec((tb, D // 2), lambda n, be, nv: (n, 0)),
        scratch_shapes=[pltpu.VMEM((D, D_EXPERT), BF16), pltpu.VMEM((D, D_EXPERT), BF16),
                        pltpu.VMEM((D_EXPERT, D), BF16)],
    )
    return pl.pallas_call(
        _expert_kernel,
        grid_spec=grid_spec,
        out_shape=jax.ShapeDtypeStruct((P, D // 2), h_pad.dtype),
        compiler_params=_params("arbitrary"),
        name="moe_experts",
    )(block_e, block_nv, h_pad, w1, w3, w2)


def _combine_kernel(dcur_ref, dnext_ref, x1_ref, route_ref, mod_ref, nw_ref, y_hbm, o_ref, buf, sem, *, tc, final):
    i = pl.program_id(0)
    slot = i % 2

    def gather(d_ref, s):
        def issue(r, carry):
            for k in range(2):
                _row_copy(y_hbm, d_ref[0, 0, 2 * r + k], buf.at[s], k * tc + r, sem.at[s]).start(priority=k)
            return carry

        lax.fori_loop(0, tc, issue, 0, unroll=True)

    pl.when(i == 0)(lambda: gather(dcur_ref, 0))
    pl.when(i + 1 < pl.num_programs(0))(lambda: gather(dnext_ref, 1 - slot))
    pltpu.make_async_copy(y_hbm.at[pl.ds(0, 2 * tc), :], buf.at[slot], sem.at[slot]).wait()
    route = route_ref[...]

    def rows(first):
        lo, hi = _unpack_bf16_pairs(buf[slot, first:first + tc, :])
        return jnp.concatenate([lo, hi], axis=1).astype(F32)

    y = route[:, 2:3] * rows(0) + route[:, 3:4] * rows(tc)
    x2 = x1_ref[...] + mod_ref[0][5:6] * y
    o_ref[...] = _rms(x2) * nw_ref[...] if final else x2


def _moe_combine(y_pad, dest, x1, route, mod, norm_w, seq, final):
    T, D = x1.shape
    tc = min(TC_COMBINE, seq)
    per_batch = seq // tc
    row = lambda i: (i, 0)
    n = T // tc
    dest3 = dest.reshape(n, 1, 2 * tc)
    return pl.pallas_call(
        functools.partial(_combine_kernel, tc=tc, final=final),
        grid=(n,),
        in_specs=[pl.BlockSpec((1, 1, 2 * tc), lambda i: (i, 0, 0), memory_space=pltpu.SMEM),
                  pl.BlockSpec((1, 1, 2 * tc), lambda i: (jnp.minimum(i + 1, n - 1), 0, 0),
                               memory_space=pltpu.SMEM),
                  pl.BlockSpec((tc, D), row),
                  pl.BlockSpec((tc, LANES), row),
                  pl.BlockSpec((1, 6, D), lambda i: (i // per_batch, 0, 0)),
                  pl.BlockSpec((1, D), lambda i: (0, 0)),
                  pl.BlockSpec(memory_space=pl.ANY)],
        out_specs=pl.BlockSpec((tc, D), row),
        out_shape=jax.ShapeDtypeStruct((T, D), F32),
        scratch_shapes=[pltpu.VMEM((2, 2 * tc, D // 2), y_pad.dtype), pltpu.SemaphoreType.DMA((2,))],
        compiler_params=_params("arbitrary"),
        name="moe_combine",
    )(dest3, dest3, x1, route, mod, norm_w.reshape(1, D), y_pad)


def _routing_tables(route, counts, n_blocks):
    tb = TB_MOE
    expert = route[:, 0:2].astype(jnp.int32)
    rank = route[:, 4:6].astype(jnp.int32)
    cnt = counts[0, :N_EXPERTS].astype(jnp.int32)
    nblk = (cnt + tb - 1) // tb
    blk_end = jnp.cumsum(nblk)
    blk_start = blk_end - nblk
    hot = expert[:, :, None] == jnp.arange(N_EXPERTS, dtype=jnp.int32)
    dest = jnp.sum(jnp.where(hot, blk_start * tb, 0), axis=-1) + rank
    blocks = jnp.arange(n_blocks, dtype=jnp.int32)
    block_e = jnp.minimum(jnp.sum(blocks[:, None] >= blk_end[None, :], axis=1), N_EXPERTS - 1).astype(jnp.int32)
    left = cnt[block_e] - (blocks - blk_start[block_e]) * tb
    block_nv = jnp.where(blocks < blk_end[-1], jnp.clip(left, 0, tb), 0).astype(jnp.int32)
    tail = blk_end[-1] + jnp.arange(N_EXPERTS, dtype=jnp.int32)
    zero_blocks = jnp.concatenate([jnp.where(nblk > 0, blk_end - 1, -1),
                                   jnp.where(tail < n_blocks, tail, -1)]).astype(jnp.int32)
    return dest.astype(jnp.int32), block_e, block_nv, zero_blocks


def kernel(x, c, positions, w_ada, b_ada, norm_mix, w_in, w_ret_o, w_diff_o, lam_q1, lam_k1, lam_q2, lam_k2,
           diff_norm, w_out, norm_ffn, w_router_group, b_router_group, w_router_expert, b_router_expert,
           w_exp_gate, w_exp_up, w_exp_down, norm_final):
    B, S, D = x.shape
    T = B * S
    depth = w_ada.shape[0]
    assert D_IN == w_in.shape[2] and S % RET_CHUNK == 0
    xf = x.reshape(T, D)
    pos = positions.reshape(T, 1)
    n_blocks = (2 * T) // TB_MOE + N_EXPERTS
    for l in range(depth):
        lambda_init = 0.8 - 0.6 * math.exp(-0.3 * l)
        mod = _adaln(c, w_ada[l], b_ada[l]).reshape(B, 6, D)
        proj = _inproj(xf, pos, mod, norm_mix[l], w_in[l].astype(BF16), S)
        ro = _retention(proj, B, S)
        lam4 = jnp.stack([lam_q1[l], lam_k1[l], lam_q2[l], lam_k2[l]]).astype(F32)
        do = _diff_attention(proj, lam4, diff_norm[l].astype(F32), lambda_init, B, S)

        pad = LANES - N_EXPERTS - N_GROUPS
        w_rt = jnp.concatenate([w_router_expert[l], w_router_group[l], jnp.zeros((D, pad), F32)], axis=1)
        b_rt = jnp.concatenate([b_router_expert[l], b_router_group[l], jnp.zeros((pad,), F32)]).reshape(1, LANES)
        x1, h2, route, counts = _post_mixer(
            ro, do, proj, xf, mod, w_ret_o[l].astype(BF16), w_diff_o[l].astype(BF16), w_out[l].astype(BF16),
            norm_ffn[l], w_rt.astype(BF16), b_rt, S)

        dest, block_e, block_nv, zero_blocks = _routing_tables(route, counts, n_blocks)
        h_pad = _moe_scatter(h2, dest.reshape(-1), zero_blocks, n_blocks * TB_MOE)
        y_pad = _moe_experts(h_pad, block_e, block_nv, w_exp_gate[l], w_exp_up[l], w_exp_down[l])
        xf = _moe_combine(y_pad, dest.reshape(-1), x1, route, mod, norm_final, S, l == depth - 1)
    return xf.reshape(B, S, D)
```

```python
import functools
import math

import jax
import jax.numpy as jnp
from jax import lax
from jax.experimental import pallas as pl
from jax.experimental.pallas import tpu as pltpu

F32 = jnp.float32
BF16 = jnp.bfloat16

EPS = 1e-6
ROPE_THETA = 10000.0
LANES = 128
RET_HEADS, RET_DK, RET_DV, RET_CHUNK = 4, 128, 256, 128
DIFF_HEADS, DIFF_DK, DIFF_DV = 4, 128, 256
N_GROUPS, EXPERTS_PER_GROUP, N_EXPERTS, D_EXPERT = 4, 8, 32, 512
OFF_RQ, OFF_RK, OFF_RV, OFF_RG, OFF_DQ, OFF_DK, OFF_DV, OFF_GR, OFF_GD, D_IN = (
    0, 512, 1024, 2048, 3072, 4096, 5120, 6144, 7168, 8192)
NEG_BIG = -1e30
ROUTE_FIELDS = 8
VMEM_LIMIT_BYTES = 48 * 1024 * 1024
VMEM_LIMIT_INPROJ_BYTES = 56 * 1024 * 1024

TM_INPROJ, TN_INPROJ = 512, 1024
TR_RETENTION = 512
BQ_ATTN = 512
HEADS_PER_ATTN_STEP = 2
TM_POST = 512
TB_MOE = 512
TS_SCATTER = 256
TC_COMBINE = 256


def _params(*sem):
    return pltpu.CompilerParams(dimension_semantics=sem, vmem_limit_bytes=VMEM_LIMIT_BYTES)


def _sigmoid(v):
    return 1.0 / (1.0 + jnp.exp(-v))


def _rms(v):
    return v * lax.rsqrt(jnp.mean(v * v, axis=-1, keepdims=True) + EPS)


def _dot(a, b):
    return jnp.dot(a, b, preferred_element_type=F32)


def _dot_nt(a, b):
    return lax.dot_general(a, b, (((1,), (1,)), ((), ())), preferred_element_type=F32)


def _pack_bf16_pairs(x):
    m = x.shape[1] // 2

    def rounded_bits(v):
        return lax.bitcast_convert_type(v.astype(BF16).astype(F32), jnp.uint32)

    return (rounded_bits(x[:, :m]) >> 16) | rounded_bits(x[:, m:])


def _unpack_bf16_pairs(w):
    lo = lax.bitcast_convert_type(w << 16, F32)
    hi = lax.bitcast_convert_type(w & jnp.uint32(0xFFFF0000), F32)
    return lo.astype(BF16), hi.astype(BF16)


def _ada_kernel(c_ref, w_ref, b_ref, o_ref):
    c = c_ref[...]
    a = (c * _sigmoid(c)).astype(BF16)
    o_ref[...] = _dot(a, w_ref[...].astype(BF16)) + b_ref[...]


def _adaln(c, w, b):
    B, D = c.shape
    n = w.shape[1] // D
    return pl.pallas_call(
        _ada_kernel,
        grid=(n,),
        in_specs=[pl.BlockSpec((B, D), lambda j: (0, 0)),
                  pl.BlockSpec((D, D), lambda j: (0, j)),
                  pl.BlockSpec((1, D), lambda j: (0, j))],
        out_specs=pl.BlockSpec((B, D), lambda j: (0, j)),
        out_shape=jax.ShapeDtypeStruct((B, n * D), F32),
        compiler_params=_params("parallel"),
        name="adaln",
    )(c, w, b.reshape(1, -1))


ROPE_GROUPS = ((OFF_RQ, OFF_RK, 1.0), (OFF_RK, OFF_RV, RET_DK ** -0.5),
               (OFF_DQ, OFF_DK, DIFF_DK ** -0.5 * math.log2(math.e)), (OFF_DK, OFF_DV, 1.0))


def _rope_scale(col):
    for lo, hi, scale in ROPE_GROUPS:
        if lo <= col < hi:
            return scale
    return None


def _inproj_kernel(x0_ref, pos0_ref, mod0_ref, xn_ref, posn_ref, modn_ref, nw_ref, inv_ref, sgn_ref,
                   w_ref, o_ref, h_ref, cos_ref, sin_ref, *, tn):
    i = pl.program_id(0)

    def prepare(x, pos, mod, slot):
        h = _rms(x) * nw_ref[...] * (1.0 + mod[1:2]) + mod[0:1]
        h_ref[slot] = h.astype(BF16)
        pos_t = jnp.broadcast_to(pos.astype(F32), (LANES, pos.shape[1])).T
        ang = pos_t * inv_ref[...]
        cos_ref[slot] = jnp.cos(ang)
        sin_ref[slot] = jnp.sin(ang) * sgn_ref[...]

    @pl.when(i == 0)
    def _():
        prepare(x0_ref[...], pos0_ref[0], mod0_ref[0], 0)

    cur = i % 2
    for g in range(o_ref.shape[1] // tn):
        acc = _dot(h_ref[cur], w_ref[:, g * tn:(g + 1) * tn])
        for k in range(tn // LANES):
            first = g * tn + k * LANES
            a = acc[:, k * LANES:(k + 1) * LANES]
            scale = _rope_scale(first)
            if scale is not None:
                a = a * cos_ref[cur] + pltpu.roll(a, LANES // 2, 1) * sin_ref[cur]
                if scale != 1.0:
                    a = a * scale
            o_ref[:, first:first + LANES] = a.astype(BF16)

    prepare(xn_ref[...], posn_ref[0], modn_ref[0], 1 - cur)


def _inproj(xf, pos, mod, norm_w, w_bf16, seq):
    T, D = xf.shape
    tm = min(TM_INPROJ, seq)
    tn = TN_INPROJ
    half = LANES // 2
    inv = ROPE_THETA ** (-jnp.arange(0, LANES, 2, dtype=F32) / LANES)
    inv = jnp.concatenate([inv, inv]).reshape(1, LANES)
    sgn = jnp.concatenate([-jnp.ones((half,), F32), jnp.ones((half,), F32)]).reshape(1, LANES)
    per_batch = seq // tm
    n_m = T // tm
    nxt = lambda i: jnp.minimum(i + 1, n_m - 1)
    once = dict(pipeline_mode=pl.Buffered(1))
    return pl.pallas_call(
        functools.partial(_inproj_kernel, tn=tn),
        grid=(n_m,),
        in_specs=[pl.BlockSpec((tm, D), lambda i: (0, 0), **once),
                  pl.BlockSpec((1, 1, tm), lambda i: (0, 0, 0), **once),
                  pl.BlockSpec((1, 6, D), lambda i: (0, 0, 0), **once),
                  pl.BlockSpec((tm, D), lambda i: (nxt(i), 0)),
                  pl.BlockSpec((1, 1, tm), lambda i: (nxt(i), 0, 0)),
                  pl.BlockSpec((1, 6, D), lambda i: (nxt(i) // per_batch, 0, 0)),
                  pl.BlockSpec((1, D), lambda i: (0, 0), **once),
                  pl.BlockSpec((1, LANES), lambda i: (0, 0), **once),
                  pl.BlockSpec((1, LANES), lambda i: (0, 0), **once),
                  pl.BlockSpec((D, D_IN), lambda i: (0, 0), **once)],
        out_specs=pl.BlockSpec((tm, D_IN), lambda i: (i, 0)),
        out_shape=jax.ShapeDtypeStruct((T, D_IN), BF16),
        scratch_shapes=[pltpu.VMEM((2, tm, D), BF16),
                        pltpu.VMEM((2, tm, LANES), F32),
                        pltpu.VMEM((2, tm, LANES), F32)],
        compiler_params=pltpu.CompilerParams(dimension_semantics=("arbitrary",),
                                             vmem_limit_bytes=VMEM_LIMIT_INPROJ_BYTES),
        name="inproj",
    )(xf, pos, mod, xf, pos, mod, norm_w.reshape(1, D), inv, sgn, w_bf16)


def _ret_kernel(q_ref, k_ref, v_ref, g_ref, dmask_ref, xi_ref, zeta_ref, cd_ref, o_ref, state_ref):
    @pl.when(pl.program_id(1) == 0)
    def _():
        state_ref[...] = jnp.zeros(state_ref.shape, F32)

    C = RET_CHUNK
    for h in range(RET_HEADS):
        qk = slice(h * RET_DK, (h + 1) * RET_DK)
        vv = slice(h * RET_DV, (h + 1) * RET_DV)
        st = state_ref[h]
        for n in range(q_ref.shape[0] // C):
            rows = slice(n * C, (n + 1) * C)
            q = q_ref[rows, qk]
            k = k_ref[rows, qk]
            v = v_ref[rows, vv]
            s = _dot_nt(q, k) * dmask_ref[h]
            qx = (q.astype(F32) * xi_ref[h]).astype(BF16)
            o = _dot(s.astype(BF16), v) + _dot(qx, st.astype(BF16))
            kz_t = (k.astype(F32) * zeta_ref[h]).T.astype(BF16)
            st = cd_ref[h] * st + _dot(kz_t, v)
            g = g_ref[rows, vv].astype(F32)
            o_ref[rows, vv] = (_rms(o) * (g * _sigmoid(g))).astype(BF16)
        state_ref[h] = st


def _retention(proj, batch, seq):
    T = proj.shape[0]
    C = RET_CHUNK
    H = RET_HEADS
    nc = seq // C
    gamma = 1.0 - jnp.exp2(-5.0 - jnp.arange(H, dtype=F32))
    log_g = jnp.log(gamma)
    idx = jnp.arange(C, dtype=F32)
    rel = idx[:, None] - idx[None, :]
    dmask = jnp.where(rel >= 0, jnp.exp(log_g[:, None, None] * jnp.maximum(rel, 0.0)), 0.0)
    zeta = jnp.exp(log_g[:, None] * (C - 1 - idx))
    xi = jnp.exp(log_g[:, None] * (idx + 1))
    cd = jnp.exp(log_g * C)
    zeta_b = jnp.broadcast_to(zeta[:, :, None], (H, C, RET_DK))
    xi_b = jnp.broadcast_to(xi[:, :, None], (H, C, RET_DK))
    cd_b = jnp.broadcast_to(cd[:, None, None], (H, 1, RET_DV))
    wq = H * RET_DK
    wv = H * RET_DV
    tr = min(TR_RETENTION, seq)
    ns = seq // tr
    row = lambda b, n: b * ns + n
    const3 = lambda b, n: (0, 0, 0)
    return pl.pallas_call(
        _ret_kernel,
        grid=(batch, ns),
        in_specs=[pl.BlockSpec((tr, wq), lambda b, n: (row(b, n), OFF_RQ // wq)),
                  pl.BlockSpec((tr, wq), lambda b, n: (row(b, n), OFF_RK // wq)),
                  pl.BlockSpec((tr, wv), lambda b, n: (row(b, n), OFF_RV // wv)),
                  pl.BlockSpec((tr, wv), lambda b, n: (row(b, n), OFF_RG // wv)),
                  pl.BlockSpec((H, C, C), const3),
                  pl.BlockSpec((H, C, RET_DK), const3),
                  pl.BlockSpec((H, C, RET_DK), const3),
                  pl.BlockSpec((H, 1, RET_DV), const3)],
        out_specs=pl.BlockSpec((tr, wv), lambda b, n: (row(b, n), 0)),
        out_shape=jax.ShapeDtypeStruct((T, wv), BF16),
        scratch_shapes=[pltpu.VMEM((H, RET_DK, RET_DV), F32)],
        compiler_params=_params("parallel", "arbitrary"),
        name="retention",
    )(proj, proj, proj, proj, dmask, xi_b, zeta_b, cd_b)


def _transpose_bf16(a):
    return a.astype(F32).T.astype(BF16)


def _attn_kernel(lam_ref, dn_ref, *refs, bq, bk, heads, lambda_init):
    in_refs, (o_ref, vt_ref, qt_ref, s_ref, pm_ref, m_ref, l_ref, acc_ref) = refs[:5 * heads], refs[5 * heads:]
    q_refs = [in_refs[5 * h + c] for h in range(heads) for c in range(2)]
    k_refs = [in_refs[5 * h + 2 + c] for h in range(heads) for c in range(2)]
    v_refs = [in_refs[5 * h + 4] for h in range(heads)]
    streams = range(2 * heads)
    i = pl.program_id(2)
    n_sub = vt_ref.shape[1]

    @pl.when(i == 0)
    def _():
        for h in range(heads):
            for t in range(n_sub):
                vt_ref[h, t] = _transpose_bf16(v_refs[h][t * bk:(t + 1) * bk, :])

    m_ref[...] = jnp.full(m_ref.shape, NEG_BIG, F32)
    l_ref[...] = jnp.zeros(l_ref.shape, F32)
    acc_ref[...] = jnp.zeros(acc_ref.shape, F32)
    for n in streams:
        qt_ref[n] = _transpose_bf16(q_refs[n][...])

    def score(t, slot, diagonal_offset=None):
        start = pl.multiple_of(t * bk, bk)
        for n in streams:
            st = _dot(k_refs[n][pl.ds(start, bk), :], qt_ref[n])
            if diagonal_offset is not None:
                key = lax.broadcasted_iota(jnp.int32, (bk, bq), 0) + diagonal_offset
                qry = lax.broadcasted_iota(jnp.int32, (bk, bq), 1)
                st = jnp.where(key <= qry, st, NEG_BIG)
            s_ref[slot, n] = st
            pm_ref[slot, n] = jnp.max(st, axis=0, keepdims=True)

    def accumulate(t, slot):
        for n in streams:
            m_prev = m_ref[n]
            m_new = jnp.maximum(m_prev, pm_ref[slot, n])
            p = jnp.exp2(s_ref[slot, n] - m_new)
            alpha = jnp.exp2(m_prev - m_new)
            l_ref[n] = alpha * l_ref[n] + jnp.sum(p, axis=0, keepdims=True)
            acc_ref[n] = alpha * acc_ref[n] + _dot(vt_ref[n // 2, t], p.astype(BF16))
            m_ref[n] = m_new

    def pair(jj, next_is_diagonal):
        t = 2 * jj
        score(t + 1, 1)
        accumulate(t, 0)
        score(t + 2, 0, 0 if next_is_diagonal else None)
        accumulate(t + 1, 1)

    pl.when(i == 0)(lambda: score(0, 0, 0))
    pl.when(i > 0)(lambda: score(0, 0))

    def body(jj, carry):
        pair(jj, False)
        return carry

    lax.fori_loop(0, i - 1, body, 0)
    pl.when(i > 0)(lambda: pair(i - 1, True))
    score(2 * i + 1, 1, bk)
    accumulate(2 * i, 0)
    accumulate(2 * i + 1, 1)

    lam4 = lam_ref[...]
    lam = (jnp.exp(jnp.sum(lam4[0:1] * lam4[1:2], axis=1, keepdims=True))
           - jnp.exp(jnp.sum(lam4[2:3] * lam4[3:4], axis=1, keepdims=True)) + lambda_init)
    dv = acc_ref.shape[1]
    for h in range(heads):
        a, b = 2 * h, 2 * h + 1
        ot = acc_ref[a] * (1.0 / l_ref[a]) - lam * (acc_ref[b] * (1.0 / l_ref[b]))
        ot = ot * lax.rsqrt(jnp.mean(ot * ot, axis=0, keepdims=True) + EPS)
        o_ref[:, h * dv:(h + 1) * dv] = (ot.T * dn_ref[...] * (1.0 - lambda_init)).astype(BF16)


def _diff_attention(proj, lam4, diff_norm, lambda_init, batch, seq):
    T = proj.shape[0]
    H = DIFF_HEADS
    hp = HEADS_PER_ATTN_STEP
    bq = min(BQ_ATTN, seq)
    bk = bq // 2
    nq = seq // bq
    q0 = OFF_DQ // DIFF_DK
    k0 = OFF_DK // DIFF_DK
    v0 = OFF_DV // DIFF_DV
    head_specs = []
    for j in range(hp):
        head = lambda g, j=j: hp * g + j
        head_specs += [
            pl.BlockSpec((bq, DIFF_DK), lambda b, g, i, head=head: (b * nq + i, q0 + 2 * head(g))),
            pl.BlockSpec((bq, DIFF_DK), lambda b, g, i, head=head: (b * nq + i, q0 + 2 * head(g) + 1)),
            pl.BlockSpec((seq, DIFF_DK), lambda b, g, i, head=head: (b, k0 + 2 * head(g))),
            pl.BlockSpec((seq, DIFF_DK), lambda b, g, i, head=head: (b, k0 + 2 * head(g) + 1)),
            pl.BlockSpec((seq, DIFF_DV), lambda b, g, i, head=head: (b, v0 + head(g)))]
    ns = 2 * hp
    return pl.pallas_call(
        functools.partial(_attn_kernel, bq=bq, bk=bk, heads=hp, lambda_init=lambda_init),
        grid=(batch, H // hp, nq),
        in_specs=[pl.BlockSpec((4, DIFF_DK), lambda b, g, i: (0, 0)),
                  pl.BlockSpec((1, DIFF_DV), lambda b, g, i: (0, 0))] + head_specs,
        out_specs=pl.BlockSpec((bq, hp * DIFF_DV), lambda b, g, i: (b * nq + i, g)),
        out_shape=jax.ShapeDtypeStruct((T, H * DIFF_DV), BF16),
        scratch_shapes=[pltpu.VMEM((hp, seq // bk, DIFF_DV, bk), BF16),
                        pltpu.VMEM((ns, DIFF_DK, bq), BF16),
                        pltpu.VMEM((2, ns, bk, bq), F32),
                        pltpu.VMEM((2, ns, 1, bq), F32),
                        pltpu.VMEM((ns, 1, bq), F32),
                        pltpu.VMEM((ns, 1, bq), F32),
                        pltpu.VMEM((ns, DIFF_DV, bq), F32)],
        compiler_params=_params("parallel", "parallel", "arbitrary"),
        name="diff_attn",
    )(lam4, diff_norm.reshape(1, DIFF_DV), *([proj] * (5 * hp)))


def _post_kernel(ro_ref, do_ref, gr_ref, gd_ref, x_ref, mod_ref, modp_ref, wr_ref, wd_ref, wo_ref, nf_ref,
                 wrt_ref, brt_ref, x1_ref, h2_ref, route_ref, rt_ref, cnt_ref, carry_ref, x1s_ref, lower_ref,
                 *, tm):
    i = pl.program_id(0)

    @pl.when(i == 0)
    def _():
        carry_ref[...] = jnp.zeros(carry_ref.shape, F32)
        x1s_ref[...] = jnp.zeros(x1s_ref.shape, F32)
        r_i = lax.broadcasted_iota(jnp.int32, (tm, tm), 0)
        c_i = lax.broadcasted_iota(jnp.int32, (tm, tm), 1)
        lower_ref[...] = jnp.where(c_i < r_i, 1.0, 0.0).astype(BF16)

    modp = modp_ref[0]
    h2 = _rms(x1s_ref[...]) * nf_ref[...] * (1.0 + modp[4:5]) + modp[3:4]
    h2_ref[...] = _pack_bf16_pairs(h2)
    lg = _dot(h2.astype(BF16), wrt_ref[...]) + brt_ref[...]

    mod = mod_ref[0]
    ret_out = _dot(ro_ref[...], wr_ref[...])
    diff_out = _dot(do_ref[...], wd_ref[...])
    merged = (_sigmoid(gr_ref[...].astype(F32)) * ret_out
              + _sigmoid(gd_ref[...].astype(F32)) * diff_out)
    x1 = x_ref[...] + mod[2:3] * _dot(merged.astype(BF16), wo_ref[...])
    x1_ref[...] = x1

    lane = lax.broadcasted_iota(jnp.int32, lg.shape, 1)
    far = jnp.int32(LANES)

    def top1(vals):
        best = jnp.max(vals, axis=1, keepdims=True)
        return best, jnp.min(jnp.where(vals == best, lane, far), axis=1, keepdims=True)

    is_group = (lane >= N_EXPERTS) & (lane < N_EXPERTS + N_GROUPS)
    g_best, g_lane = top1(jnp.where(is_group, lg, NEG_BIG))
    g_sum = jnp.sum(jnp.where(is_group, jnp.exp(lg - g_best), 0.0), axis=1, keepdims=True)
    g_top = 1.0 / g_sum
    first = (g_lane - N_EXPERTS) * EXPERTS_PER_GROUP
    in_group = (lane >= first) & (lane < first + EXPERTS_PER_GROUP)
    el = jnp.where(in_group, lg, NEG_BIG)
    e_a, i_a = top1(el)
    e_b, i_b = top1(jnp.where(lane == i_a, NEG_BIG, el))
    t = jnp.exp(e_b - e_a)
    w_a = g_top / (1.0 + t)
    w_b = g_top * t / (1.0 + t)

    hot_a = lane == i_a
    hot_b = lane == i_b
    hot = jnp.where((hot_a | hot_b) & (i > 0), 1.0, 0.0)
    before = _dot(lower_ref[...], hot.astype(BF16)) + carry_ref[...]
    rank_a = jnp.sum(jnp.where(hot_a, before, 0.0), axis=1, keepdims=True)
    rank_b = jnp.sum(jnp.where(hot_b, before, 0.0), axis=1, keepdims=True)
    total = carry_ref[...] + jnp.sum(hot, axis=0, keepdims=True)
    carry_ref[...] = total
    cnt_ref[...] = total

    fields = (i_a.astype(F32), i_b.astype(F32), w_a, w_b, rank_a, rank_b)
    route = jnp.zeros(lg.shape, F32)
    for n, f in enumerate(fields):
        route = jnp.where(lane == n, f, route)
    route_ref[...] = route
    rt_ref[...] = route.T[0:ROUTE_FIELDS, :]
    x1s_ref[...] = x1


def _post_mixer(ro, do, proj, xf, mod, w_ret_o, w_diff_o, w_out, norm_ffn, w_rt, b_rt, seq):
    T, D = xf.shape
    tm = min(TM_POST, seq)
    per_batch = seq // tm
    n = T // tm
    cur = lambda i: jnp.minimum(i, n - 1)
    prev = lambda i: jnp.maximum(i - 1, 0)
    row = lambda i: (cur(i), 0)
    prow = lambda i: (prev(i), 0)
    const = lambda i: (0, 0)
    return pl.pallas_call(
        functools.partial(_post_kernel, tm=tm),
        grid=(n + 1,),
        in_specs=[pl.BlockSpec((tm, D), row),
                  pl.BlockSpec((tm, D), row),
                  pl.BlockSpec((tm, D), lambda i: (cur(i), OFF_GR // D)),
                  pl.BlockSpec((tm, D), lambda i: (cur(i), OFF_GD // D)),
                  pl.BlockSpec((tm, D), row),
                  pl.BlockSpec((1, 6, D), lambda i: (cur(i) // per_batch, 0, 0)),
                  pl.BlockSpec((1, 6, D), lambda i: (prev(i) // per_batch, 0, 0)),
                  pl.BlockSpec((D, D), const),
                  pl.BlockSpec((D, D), const),
                  pl.BlockSpec((D, D), const),
                  pl.BlockSpec((1, D), const),
                  pl.BlockSpec((D, LANES), const),
                  pl.BlockSpec((1, LANES), const)],
        out_specs=[pl.BlockSpec((tm, D), row),
                   pl.BlockSpec((tm, D // 2), prow),
                   pl.BlockSpec((tm, LANES), prow),
                   pl.BlockSpec((ROUTE_FIELDS, tm), lambda i: (0, prev(i))),
                   pl.BlockSpec((1, LANES), const)],
        out_shape=[jax.ShapeDtypeStruct((T, D), F32),
                   jax.ShapeDtypeStruct((T, D // 2), jnp.uint32),
                   jax.ShapeDtypeStruct((T, LANES), F32),
                   jax.ShapeDtypeStruct((ROUTE_FIELDS, T), F32),
                   jax.ShapeDtypeStruct((1, LANES), F32)],
        scratch_shapes=[pltpu.VMEM((1, LANES), F32), pltpu.VMEM((tm, D), F32), pltpu.VMEM((tm, tm), BF16)],
        compiler_params=_params("arbitrary"),
        name="post_mixer",
    )(ro, do, proj, proj, xf, mod, mod, w_ret_o, w_diff_o, w_out, norm_ffn.reshape(1, D), w_rt, b_rt)


def _row_copy(src, s, dst, d, sem):
    return pltpu.make_async_copy(src.at[pl.ds(s, 1), :], dst.at[pl.ds(d, 1), :], sem)


def _dest_blocks(dest, tile):
    n = dest.shape[1] // tile
    return dest.reshape(2, n, tile).transpose(1, 0, 2).reshape(n, 1, 2 * tile)


def _scatter_kernel(zb_ref, dest_ref, h_ref, o_hbm, zero_ref, zsem, sem, *, ts, tb):
    i = pl.program_id(0)

    @pl.when(i == 0)
    def _():
        zero_ref[...] = jnp.zeros(zero_ref.shape, zero_ref.dtype)

        def zero_copy(n):
            start = pl.multiple_of(zb_ref[n] * tb, tb)
            return pltpu.make_async_copy(zero_ref, o_hbm.at[pl.ds(start, tb), :], zsem)

        for n in range(zb_ref.shape[0]):
            pl.when(zb_ref[n] >= 0)(lambda n=n: zero_copy(n).start())
        for n in range(zb_ref.shape[0]):
            pl.when(zb_ref[n] >= 0)(lambda n=n: zero_copy(n).wait())

    def issue(r, carry):
        for k in range(2):
            _row_copy(h_ref, r, o_hbm, dest_ref[0, 0, k * ts + r], sem).start(priority=k)
        return carry

    lax.fori_loop(0, ts, issue, 0, unroll=True)
    pltpu.make_async_copy(o_hbm.at[pl.ds(0, 2 * ts), :], o_hbm.at[pl.ds(0, 2 * ts), :], sem).wait()


def _moe_scatter(h2, dest, zero_blocks, n_rows):
    T, D = h2.shape
    ts = min(TS_SCATTER, T)
    tb = TB_MOE
    grid_spec = pltpu.PrefetchScalarGridSpec(
        num_scalar_prefetch=1,
        grid=(T // ts,),
        in_specs=[pl.BlockSpec((1, 1, 2 * ts), lambda i, zb: (i, 0, 0), memory_space=pltpu.SMEM),
                  pl.BlockSpec((ts, D), lambda i, zb: (i, 0))],
        out_specs=pl.BlockSpec(memory_space=pl.ANY),
        scratch_shapes=[pltpu.VMEM((tb, D), h2.dtype), pltpu.SemaphoreType.DMA(()), pltpu.SemaphoreType.DMA(())],
    )
    return pl.pallas_call(
        functools.partial(_scatter_kernel, ts=ts, tb=tb),
        grid_spec=grid_spec,
        out_shape=jax.ShapeDtypeStruct((n_rows, D), h2.dtype),
        compiler_params=_params("arbitrary"),
        name="moe_scatter",
    )(zero_blocks, _dest_blocks(dest, ts), h2)


def _expert_kernel(be_ref, nv_ref, x_ref, w1_ref, w3_ref, w2_ref, o_ref, w1b_ref, w3b_ref, w2b_ref):
    n = pl.program_id(0)
    nv = nv_ref[n]
    new_expert = (n == 0) | (be_ref[n] != be_ref[jnp.maximum(n - 1, 0)])

    @pl.when(new_expert & (nv > 0))
    def _():
        w1b_ref[...] = w1_ref[0].astype(BF16)
        w3b_ref[...] = w3_ref[0].astype(BF16)
        w2b_ref[...] = w2_ref[0].astype(BF16)

    @pl.when(nv > 0)
    def _():
        x_lo, x_hi = _unpack_bf16_pairs(x_ref[...])
        half = x_lo.shape[1]
        g = _dot(x_lo, w1b_ref[:half, :]) + _dot(x_hi, w1b_ref[half:, :])
        u = _dot(x_lo, w3b_ref[:half, :]) + _dot(x_hi, w3b_ref[half:, :])
        a = (g * _sigmoid(g) * u).astype(BF16)
        o_ref[...] = _pack_bf16_pairs(_dot(a, w2b_ref[...]))

    @pl.when(nv == 0)
    def _():
        o_ref[...] = jnp.zeros(o_ref.shape, o_ref.dtype)


def _moe_experts(h_pad, block_e, block_nv, w1, w3, w2):
    P = h_pad.shape[0]
    D = 2 * h_pad.shape[1]
    tb = TB_MOE
    wmap = lambda n, be, nv: (be[n], 0, 0)
    grid_spec = pltpu.PrefetchScalarGridSpec(
        num_scalar_prefetch=2,
        grid=(P // tb,),
        in_specs=[pl.BlockSpec((tb, D // 2), lambda n, be, nv: (n, 0)),
                  pl.BlockSpec((1, D, D_EXPERT), wmap),
                  pl.BlockSpec((1, D, D_EXPERT), wmap),
                  pl.BlockSpec((1, D_EXPERT, D), wmap)],
        out_specs=pl.BlockSpec((tb, D // 2), lambda n, be, nv: (n, 0)),
        scratch_shapes=[pltpu.VMEM((D, D_EXPERT), BF16), pltpu.VMEM((D, D_EXPERT), BF16),
                        pltpu.VMEM((D_EXPERT, D), BF16)],
    )
    return pl.pallas_call(
        _expert_kernel,
        grid_spec=grid_spec,
        out_shape=jax.ShapeDtypeStruct((P, D // 2), h_pad.dtype),
        compiler_params=_params("arbitrary"),
        name="moe_experts",
    )(block_e, block_nv, h_pad, w1, w3, w2)


def _combine_kernel(dcur_ref, dnext_ref, x1_ref, route_ref, mod_ref, nw_ref, y_hbm, o_ref, buf, sem, *, tc, final):
    i = pl.program_id(0)
    slot = i % 2

    def gather(d_ref, s):
        def issue(r, carry):
            for k in range(2):
                _row_copy(y_hbm, d_ref[0, 0, k * tc + r], buf.at[s], k * tc + r, sem.at[s]).start(priority=k)
            return carry

        lax.fori_loop(0, tc, issue, 0, unroll=True)

    pl.when(i == 0)(lambda: gather(dcur_ref, 0))
    pl.when(i + 1 < pl.num_programs(0))(lambda: gather(dnext_ref, 1 - slot))
    pltpu.make_async_copy(y_hbm.at[pl.ds(0, 2 * tc), :], buf.at[slot], sem.at[slot]).wait()
    route = route_ref[...]

    def rows(first):
        lo, hi = _unpack_bf16_pairs(buf[slot, first:first + tc, :])
        return jnp.concatenate([lo, hi], axis=1).astype(F32)

    y = route[:, 2:3] * rows(0) + route[:, 3:4] * rows(tc)
    x2 = x1_ref[...] + mod_ref[0][5:6] * y
    o_ref[...] = _rms(x2) * nw_ref[...] if final else x2


def _moe_combine(y_pad, dest, x1, route, mod, norm_w, seq, final):
    T, D = x1.shape
    tc = min(TC_COMBINE, seq)
    per_batch = seq // tc
    row = lambda i: (i, 0)
    n = T // tc
    dest3 = _dest_blocks(dest, tc)
    return pl.pallas_call(
        functools.partial(_combine_kernel, tc=tc, final=final),
        grid=(n,),
        in_specs=[pl.BlockSpec((1, 1, 2 * tc), lambda i: (i, 0, 0), memory_space=pltpu.SMEM),
                  pl.BlockSpec((1, 1, 2 * tc), lambda i: (jnp.minimum(i + 1, n - 1), 0, 0),
                               memory_space=pltpu.SMEM),
                  pl.BlockSpec((tc, D), row),
                  pl.BlockSpec((tc, LANES), row),
                  pl.BlockSpec((1, 6, D), lambda i: (i // per_batch, 0, 0)),
                  pl.BlockSpec((1, D), lambda i: (0, 0)),
                  pl.BlockSpec(memory_space=pl.ANY)],
        out_specs=pl.BlockSpec((tc, D), row),
        out_shape=jax.ShapeDtypeStruct((T, D), F32),
        scratch_shapes=[pltpu.VMEM((2, 2 * tc, D // 2), y_pad.dtype), pltpu.SemaphoreType.DMA((2,))],
        compiler_params=_params("arbitrary"),
        name="moe_combine",
    )(dest3, dest3, x1, route, mod, norm_w.reshape(1, D), y_pad)


def _routing_tables(rt, counts, n_blocks):
    tb = TB_MOE
    expert = rt[0:2].astype(jnp.int32)
    rank = rt[4:6].astype(jnp.int32)
    cnt = counts[0, :N_EXPERTS].astype(jnp.int32)
    nblk = (cnt + tb - 1) // tb
    blk_end = jnp.cumsum(nblk)
    blk_start = blk_end - nblk
    hot = expert[None] == jnp.arange(N_EXPERTS, dtype=jnp.int32)[:, None, None]
    dest = jnp.sum(jnp.where(hot, (blk_start * tb)[:, None, None], 0), axis=0) + rank
    blocks = jnp.arange(n_blocks, dtype=jnp.int32)
    block_e = jnp.minimum(jnp.sum(blocks[:, None] >= blk_end[None, :], axis=1), N_EXPERTS - 1).astype(jnp.int32)
    left = cnt[block_e] - (blocks - blk_start[block_e]) * tb
    block_nv = jnp.where(blocks < blk_end[-1], jnp.clip(left, 0, tb), 0).astype(jnp.int32)
    tail = blk_end[-1] + jnp.arange(N_EXPERTS, dtype=jnp.int32)
    zero_blocks = jnp.concatenate([jnp.where(nblk > 0, blk_end - 1, -1),
                                   jnp.where(tail < n_blocks, tail, -1)]).astype(jnp.int32)
    return dest.astype(jnp.int32), block_e, block_nv, zero_blocks


def kernel(x, c, positions, w_ada, b_ada, norm_mix, w_in, w_ret_o, w_diff_o, lam_q1, lam_k1, lam_q2, lam_k2,
           diff_norm, w_out, norm_ffn, w_router_group, b_router_group, w_router_expert, b_router_expert,
           w_exp_gate, w_exp_up, w_exp_down, norm_final):
    B, S, D = x.shape
    T = B * S
    depth = w_ada.shape[0]
    assert D_IN == w_in.shape[2] and S % RET_CHUNK == 0
    xf = x.reshape(T, D)
    tm_in = min(TM_INPROJ, S)
    pos = positions.reshape(T // tm_in, 1, tm_in)
    n_blocks = (2 * T) // TB_MOE + N_EXPERTS
    for l in range(depth):
        lambda_init = 0.8 - 0.6 * math.exp(-0.3 * l)
        mod = _adaln(c, w_ada[l], b_ada[l]).reshape(B, 6, D)
        proj = _inproj(xf, pos, mod, norm_mix[l], w_in[l].astype(BF16), S)
        ro = _retention(proj, B, S)
        lam4 = jnp.stack([lam_q1[l], lam_k1[l], lam_q2[l], lam_k2[l]]).astype(F32)
        do = _diff_attention(proj, lam4, diff_norm[l].astype(F32), lambda_init, B, S)

        pad = LANES - N_EXPERTS - N_GROUPS
        w_rt = jnp.concatenate([w_router_expert[l], w_router_group[l], jnp.zeros((D, pad), F32)], axis=1)
        b_rt = jnp.concatenate([b_router_expert[l], b_router_group[l], jnp.zeros((pad,), F32)]).reshape(1, LANES)
        x1, h2, route, rt, counts = _post_mixer(
            ro, do, proj, xf, mod, w_ret_o[l].astype(BF16), w_diff_o[l].astype(BF16), w_out[l].astype(BF16),
            norm_ffn[l], w_rt.astype(BF16), b_rt, S)

        dest, block_e, block_nv, zero_blocks = _routing_tables(rt, counts, n_blocks)
        h_pad = _moe_scatter(h2, dest, zero_blocks, n_blocks * TB_MOE)
        y_pad = _moe_experts(h_pad, block_e, block_nv, w_exp_gate[l], w_exp_up[l], w_exp_down[l])
        xf = _moe_combine(y_pad, dest, x1, route, mod, norm_final, S, l == depth - 1)
    return xf.reshape(B, S, D)
```

```python
import functools
import math

import jax
import jax.numpy as jnp
from jax import lax
from jax.experimental import pallas as pl
from jax.experimental.pallas import tpu as pltpu

F32 = jnp.float32
BF16 = jnp.bfloat16

EPS = 1e-6
ROPE_THETA = 10000.0
LANES = 128
RET_HEADS, RET_DK, RET_DV, RET_CHUNK = 4, 128, 256, 128
DIFF_HEADS, DIFF_DK, DIFF_DV = 4, 128, 256
N_GROUPS, EXPERTS_PER_GROUP, N_EXPERTS, D_EXPERT = 4, 8, 32, 512
OFF_RQ, OFF_RK, OFF_RV, OFF_RG, OFF_DQ, OFF_DK, OFF_DV, OFF_GR, OFF_GD, D_IN = (
    0, 512, 1024, 2048, 3072, 4096, 5120, 6144, 7168, 8192)
NEG_BIG = -1e30
ROUTE_FIELDS = 8
VMEM_LIMIT_BYTES = 48 * 1024 * 1024
VMEM_LIMIT_INPROJ_BYTES = 56 * 1024 * 1024

TM_INPROJ, TN_INPROJ = 512, 1024
TR_RETENTION = 512
BQ_ATTN = 512
HEADS_PER_ATTN_STEP = 2
TM_POST = 512
TB_MOE = 512
TS_SCATTER = 256
TC_COMBINE = 256


def _params(*sem):
    return pltpu.CompilerParams(dimension_semantics=sem, vmem_limit_bytes=VMEM_LIMIT_BYTES)


def _sigmoid(v):
    return 1.0 / (1.0 + jnp.exp(-v))


def _rms(v):
    return v * lax.rsqrt(jnp.mean(v * v, axis=-1, keepdims=True) + EPS)


def _dot(a, b):
    return jnp.dot(a, b, preferred_element_type=F32)


def _dot_nt(a, b):
    return lax.dot_general(a, b, (((1,), (1,)), ((), ())), preferred_element_type=F32)


def _pack_bf16_pairs(x):
    m = x.shape[1] // 2

    def rounded_bits(v):
        return lax.bitcast_convert_type(v.astype(BF16).astype(F32), jnp.uint32)

    return (rounded_bits(x[:, :m]) >> 16) | rounded_bits(x[:, m:])


def _unpack_bf16_pairs(w):
    lo = lax.bitcast_convert_type(w << 16, F32)
    hi = lax.bitcast_convert_type(w & jnp.uint32(0xFFFF0000), F32)
    return lo.astype(BF16), hi.astype(BF16)


def _ada_kernel(c_ref, w_ref, b_ref, o_ref):
    c = c_ref[...]
    a = (c * _sigmoid(c)).astype(BF16)
    o_ref[...] = _dot(a, w_ref[...].astype(BF16)) + b_ref[...]


def _adaln(c, w, b):
    B, D = c.shape
    n = w.shape[1] // D
    return pl.pallas_call(
        _ada_kernel,
        grid=(n,),
        in_specs=[pl.BlockSpec((B, D), lambda j: (0, 0)),
                  pl.BlockSpec((D, D), lambda j: (0, j)),
                  pl.BlockSpec((1, D), lambda j: (0, j))],
        out_specs=pl.BlockSpec((B, D), lambda j: (0, j)),
        out_shape=jax.ShapeDtypeStruct((B, n * D), F32),
        compiler_params=_params("parallel"),
        name="adaln",
    )(c, w, b.reshape(1, -1))


ROPE_GROUPS = ((OFF_RQ, OFF_RK, 1.0), (OFF_RK, OFF_RV, RET_DK ** -0.5),
               (OFF_DQ, OFF_DK, DIFF_DK ** -0.5 * math.log2(math.e)), (OFF_DK, OFF_DV, 1.0))


def _rope_scale(col):
    for lo, hi, scale in ROPE_GROUPS:
        if lo <= col < hi:
            return scale
    return None


def _inproj_kernel(x0_ref, pos0_ref, mod0_ref, xn_ref, posn_ref, modn_ref, nw_ref, inv_ref,
                   w_ref, o_ref, h_ref, cos_ref, sin_ref, *, tn):
    i = pl.program_id(0)

    def prepare(x, pos, mod, slot):
        h = _rms(x) * nw_ref[...] * (1.0 + mod[1:2]) + mod[0:1]
        h_ref[slot] = h.astype(BF16)
        pos_t = jnp.broadcast_to(pos.astype(F32), (LANES, pos.shape[1])).T
        half = pos_t.shape[0] // 2
        low = lax.broadcasted_iota(jnp.int32, (half, LANES), 1) < LANES // 2
        ang = jnp.where(low, pos_t[:half], pos_t[half:]) * inv_ref[...]
        cos = jnp.cos(ang)
        sin = jnp.sin(ang)
        cos_r = pltpu.roll(cos, LANES // 2, 1)
        sin_r = pltpu.roll(sin, LANES // 2, 1)
        cos_ref[slot, :half] = jnp.where(low, cos, cos_r)
        cos_ref[slot, half:] = jnp.where(low, cos_r, cos)
        sin_ref[slot, :half] = jnp.where(low, -sin, sin_r)
        sin_ref[slot, half:] = jnp.where(low, -sin_r, sin)

    @pl.when(i == 0)
    def _():
        prepare(x0_ref[...], pos0_ref[0], mod0_ref[0], 0)

    cur = i % 2
    for g in range(o_ref.shape[1] // tn):
        acc = _dot(h_ref[cur], w_ref[:, g * tn:(g + 1) * tn])
        for k in range(tn // LANES):
            first = g * tn + k * LANES
            a = acc[:, k * LANES:(k + 1) * LANES]
            scale = _rope_scale(first)
            if scale is not None:
                a = a * cos_ref[cur] + pltpu.roll(a, LANES // 2, 1) * sin_ref[cur]
                if scale != 1.0:
                    a = a * scale
            o_ref[:, first:first + LANES] = a.astype(BF16)

    prepare(xn_ref[...], posn_ref[0], modn_ref[0], 1 - cur)


def _inproj(xf, pos, mod, norm_w, w_bf16, seq):
    T, D = xf.shape
    tm = min(TM_INPROJ, seq)
    tn = TN_INPROJ
    half = LANES // 2
    inv = ROPE_THETA ** (-jnp.arange(0, LANES, 2, dtype=F32) / LANES)
    inv = jnp.concatenate([inv, inv]).reshape(1, LANES)
    per_batch = seq // tm
    n_m = T // tm
    nxt = lambda i: jnp.minimum(i + 1, n_m - 1)
    once = dict(pipeline_mode=pl.Buffered(1))
    return pl.pallas_call(
        functools.partial(_inproj_kernel, tn=tn),
        grid=(n_m,),
        in_specs=[pl.BlockSpec((tm, D), lambda i: (0, 0), **once),
                  pl.BlockSpec((1, 1, tm), lambda i: (0, 0, 0), **once),
                  pl.BlockSpec((1, 6, D), lambda i: (0, 0, 0), **once),
                  pl.BlockSpec((tm, D), lambda i: (nxt(i), 0)),
                  pl.BlockSpec((1, 1, tm), lambda i: (nxt(i), 0, 0)),
                  pl.BlockSpec((1, 6, D), lambda i: (nxt(i) // per_batch, 0, 0)),
                  pl.BlockSpec((1, D), lambda i: (0, 0), **once),
                  pl.BlockSpec((1, LANES), lambda i: (0, 0), **once),
                  pl.BlockSpec((D, D_IN), lambda i: (0, 0), **once)],
        out_specs=pl.BlockSpec((tm, D_IN), lambda i: (i, 0)),
        out_shape=jax.ShapeDtypeStruct((T, D_IN), BF16),
        scratch_shapes=[pltpu.VMEM((2, tm, D), BF16),
                        pltpu.VMEM((2, tm, LANES), F32),
                        pltpu.VMEM((2, tm, LANES), F32)],
        compiler_params=pltpu.CompilerParams(dimension_semantics=("arbitrary",),
                                             vmem_limit_bytes=VMEM_LIMIT_INPROJ_BYTES),
        name="inproj",
    )(xf, pos, mod, xf, pos, mod, norm_w.reshape(1, D), inv, w_bf16)


def _ret_kernel(q_ref, k_ref, v_ref, g_ref, dmask_ref, xi_ref, zeta_ref, cd_ref, o_ref, state_ref):
    @pl.when(pl.program_id(1) == 0)
    def _():
        state_ref[...] = jnp.zeros(state_ref.shape, F32)

    C = RET_CHUNK
    for h in range(RET_HEADS):
        qk = slice(h * RET_DK, (h + 1) * RET_DK)
        vv = slice(h * RET_DV, (h + 1) * RET_DV)
        st = state_ref[h]
        for n in range(q_ref.shape[0] // C):
            rows = slice(n * C, (n + 1) * C)
            q = q_ref[rows, qk]
            k = k_ref[rows, qk]
            v = v_ref[rows, vv]
            s = _dot_nt(q, k) * dmask_ref[h]
            qx = (q.astype(F32) * xi_ref[h]).astype(BF16)
            o = _dot(s.astype(BF16), v) + _dot(qx, st.astype(BF16))
            kz_t = (k.astype(F32) * zeta_ref[h]).T.astype(BF16)
            st = cd_ref[h] * st + _dot(kz_t, v)
            g = g_ref[rows, vv].astype(F32)
            o_ref[rows, vv] = (_rms(o) * (g * _sigmoid(g))).astype(BF16)
        state_ref[h] = st


def _retention(proj, batch, seq):
    T = proj.shape[0]
    C = RET_CHUNK
    H = RET_HEADS
    nc = seq // C
    gamma = 1.0 - jnp.exp2(-5.0 - jnp.arange(H, dtype=F32))
    log_g = jnp.log(gamma)
    idx = jnp.arange(C, dtype=F32)
    rel = idx[:, None] - idx[None, :]
    dmask = jnp.where(rel >= 0, jnp.exp(log_g[:, None, None] * jnp.maximum(rel, 0.0)), 0.0)
    zeta = jnp.exp(log_g[:, None] * (C - 1 - idx))
    xi = jnp.exp(log_g[:, None] * (idx + 1))
    cd = jnp.exp(log_g * C)
    zeta_b = jnp.broadcast_to(zeta[:, :, None], (H, C, RET_DK))
    xi_b = jnp.broadcast_to(xi[:, :, None], (H, C, RET_DK))
    cd_b = jnp.broadcast_to(cd[:, None, None], (H, 1, RET_DV))
    wq = H * RET_DK
    wv = H * RET_DV
    tr = min(TR_RETENTION, seq)
    ns = seq // tr
    row = lambda b, n: b * ns + n
    const3 = lambda b, n: (0, 0, 0)
    return pl.pallas_call(
        _ret_kernel,
        grid=(batch, ns),
        in_specs=[pl.BlockSpec((tr, wq), lambda b, n: (row(b, n), OFF_RQ // wq)),
                  pl.BlockSpec((tr, wq), lambda b, n: (row(b, n), OFF_RK // wq)),
                  pl.BlockSpec((tr, wv), lambda b, n: (row(b, n), OFF_RV // wv)),
                  pl.BlockSpec((tr, wv), lambda b, n: (row(b, n), OFF_RG // wv)),
                  pl.BlockSpec((H, C, C), const3),
                  pl.BlockSpec((H, C, RET_DK), const3),
                  pl.BlockSpec((H, C, RET_DK), const3),
                  pl.BlockSpec((H, 1, RET_DV), const3)],
        out_specs=pl.BlockSpec((tr, wv), lambda b, n: (row(b, n), 0)),
        out_shape=jax.ShapeDtypeStruct((T, wv), BF16),
        scratch_shapes=[pltpu.VMEM((H, RET_DK, RET_DV), F32)],
        compiler_params=_params("parallel", "arbitrary"),
        name="retention",
    )(proj, proj, proj, proj, dmask, xi_b, zeta_b, cd_b)


def _transpose_bf16(a):
    return a.astype(F32).T.astype(BF16)


def _attn_kernel(lam_ref, dn_ref, *refs, bq, bk, heads, lambda_init):
    in_refs, (o_ref, vt_ref, qt_ref, s_ref, pm_ref, m_ref, l_ref, acc_ref) = refs[:5 * heads], refs[5 * heads:]
    q_refs = [in_refs[5 * h + c] for h in range(heads) for c in range(2)]
    k_refs = [in_refs[5 * h + 2 + c] for h in range(heads) for c in range(2)]
    v_refs = [in_refs[5 * h + 4] for h in range(heads)]
    streams = range(2 * heads)
    i = pl.program_id(2)
    n_sub = vt_ref.shape[1]

    @pl.when(i == 0)
    def _():
        for h in range(heads):
            for t in range(n_sub):
                vt_ref[h, t] = _transpose_bf16(v_refs[h][t * bk:(t + 1) * bk, :])

    m_ref[...] = jnp.full(m_ref.shape, NEG_BIG, F32)
    l_ref[...] = jnp.zeros(l_ref.shape, F32)
    acc_ref[...] = jnp.zeros(acc_ref.shape, F32)
    for n in streams:
        qt_ref[n] = _transpose_bf16(q_refs[n][...])

    def score(t, slot, diagonal_offset=None, q_from=0):
        start = pl.multiple_of(t * bk, bk)
        for n in streams:
            st = _dot(k_refs[n][pl.ds(start, bk), :], qt_ref[n, :, q_from:])
            if diagonal_offset is not None:
                key = lax.broadcasted_iota(jnp.int32, st.shape, 0) + diagonal_offset
                qry = lax.broadcasted_iota(jnp.int32, st.shape, 1) + q_from
                st = jnp.where(key <= qry, st, NEG_BIG)
            s_ref[slot, n, :, q_from:] = st
            pm_ref[slot, n, :, q_from:] = jnp.max(st, axis=0, keepdims=True)

    def accumulate(t, slot, q_from=0):
        for n in streams:
            m_prev = m_ref[n, :, q_from:]
            m_new = jnp.maximum(m_prev, pm_ref[slot, n, :, q_from:])
            p = jnp.exp2(s_ref[slot, n, :, q_from:] - m_new)
            alpha = jnp.exp2(m_prev - m_new)
            l_ref[n, :, q_from:] = alpha * l_ref[n, :, q_from:] + jnp.sum(p, axis=0, keepdims=True)
            acc_ref[n, :, q_from:] = (alpha * acc_ref[n, :, q_from:]
                                      + _dot(vt_ref[n // 2, t], p.astype(BF16)))
            m_ref[n, :, q_from:] = m_new

    def pair(jj, next_is_diagonal):
        t = 2 * jj
        score(t + 1, 1)
        accumulate(t, 0)
        score(t + 2, 0, 0 if next_is_diagonal else None)
        accumulate(t + 1, 1)

    pl.when(i == 0)(lambda: score(0, 0, 0))
    pl.when(i > 0)(lambda: score(0, 0))

    def body(jj, carry):
        pair(jj, False)
        return carry

    lax.fori_loop(0, i - 1, body, 0)
    pl.when(i > 0)(lambda: pair(i - 1, True))
    score(2 * i + 1, 1, bk, q_from=bk)
    accumulate(2 * i, 0)
    accumulate(2 * i + 1, 1, q_from=bk)

    lam4 = lam_ref[...]
    lam = (jnp.exp(jnp.sum(lam4[0:1] * lam4[1:2], axis=1, keepdims=True))
           - jnp.exp(jnp.sum(lam4[2:3] * lam4[3:4], axis=1, keepdims=True)) + lambda_init)
    dv = acc_ref.shape[1]
    for h in range(heads):
        a, b = 2 * h, 2 * h + 1
        ot = acc_ref[a] * (1.0 / l_ref[a]) - lam * (acc_ref[b] * (1.0 / l_ref[b]))
        ot = ot * lax.rsqrt(jnp.mean(ot * ot, axis=0, keepdims=True) + EPS)
        o_ref[:, h * dv:(h + 1) * dv] = (ot.T * dn_ref[...] * (1.0 - lambda_init)).astype(BF16)


def _diff_attention(proj, lam4, diff_norm, lambda_init, batch, seq):
    T = proj.shape[0]
    H = DIFF_HEADS
    hp = HEADS_PER_ATTN_STEP
    bq = min(BQ_ATTN, seq)
    bk = bq // 2
    nq = seq // bq
    q0 = OFF_DQ // DIFF_DK
    k0 = OFF_DK // DIFF_DK
    v0 = OFF_DV // DIFF_DV
    head_specs = []
    for j in range(hp):
        head = lambda g, j=j: hp * g + j
        head_specs += [
            pl.BlockSpec((bq, DIFF_DK), lambda b, g, i, head=head: (b * nq + i, q0 + 2 * head(g))),
            pl.BlockSpec((bq, DIFF_DK), lambda b, g, i, head=head: (b * nq + i, q0 + 2 * head(g) + 1)),
            pl.BlockSpec((seq, DIFF_DK), lambda b, g, i, head=head: (b, k0 + 2 * head(g))),
            pl.BlockSpec((seq, DIFF_DK), lambda b, g, i, head=head: (b, k0 + 2 * head(g) + 1)),
            pl.BlockSpec((seq, DIFF_DV), lambda b, g, i, head=head: (b, v0 + head(g)))]
    ns = 2 * hp
    return pl.pallas_call(
        functools.partial(_attn_kernel, bq=bq, bk=bk, heads=hp, lambda_init=lambda_init),
        grid=(batch, H // hp, nq),
        in_specs=[pl.BlockSpec((4, DIFF_DK), lambda b, g, i: (0, 0)),
                  pl.BlockSpec((1, DIFF_DV), lambda b, g, i: (0, 0))] + head_specs,
        out_specs=pl.BlockSpec((bq, hp * DIFF_DV), lambda b, g, i: (b * nq + i, g)),
        out_shape=jax.ShapeDtypeStruct((T, H * DIFF_DV), BF16),
        scratch_shapes=[pltpu.VMEM((hp, seq // bk, DIFF_DV, bk), BF16),
                        pltpu.VMEM((ns, DIFF_DK, bq), BF16),
                        pltpu.VMEM((2, ns, bk, bq), F32),
                        pltpu.VMEM((2, ns, 1, bq), F32),
                        pltpu.VMEM((ns, 1, bq), F32),
                        pltpu.VMEM((ns, 1, bq), F32),
                        pltpu.VMEM((ns, DIFF_DV, bq), F32)],
        compiler_params=_params("parallel", "parallel", "arbitrary"),
        name="diff_attn",
    )(lam4, diff_norm.reshape(1, DIFF_DV), *([proj] * (5 * hp)))


def _post_kernel(ro_ref, do_ref, gr_ref, gd_ref, x_ref, mod_ref, modp_ref, wr_ref, wd_ref, wo_ref, nf_ref,
                 wrt_ref, brt_ref, x1_ref, h2_ref, route_ref, rt_ref, cnt_ref, carry_ref, x1s_ref, lower_ref,
                 *, tm):
    i = pl.program_id(0)

    @pl.when(i == 0)
    def _():
        carry_ref[...] = jnp.zeros(carry_ref.shape, F32)
        x1s_ref[...] = jnp.zeros(x1s_ref.shape, F32)
        r_i = lax.broadcasted_iota(jnp.int32, (tm, tm), 0)
        c_i = lax.broadcasted_iota(jnp.int32, (tm, tm), 1)
        lower_ref[...] = jnp.where(c_i < r_i, 1.0, 0.0).astype(BF16)

    modp = modp_ref[0]
    h2 = _rms(x1s_ref[...]) * nf_ref[...] * (1.0 + modp[4:5]) + modp[3:4]
    h2_ref[...] = _pack_bf16_pairs(h2)
    lg = _dot(h2.astype(BF16), wrt_ref[...]) + brt_ref[...]

    mod = mod_ref[0]
    ret_out = _dot(ro_ref[...], wr_ref[...])
    diff_out = _dot(do_ref[...], wd_ref[...])
    merged = (_sigmoid(gr_ref[...].astype(F32)) * ret_out
              + _sigmoid(gd_ref[...].astype(F32)) * diff_out)
    x1 = x_ref[...] + mod[2:3] * _dot(merged.astype(BF16), wo_ref[...])
    x1_ref[...] = x1

    lane = lax.broadcasted_iota(jnp.int32, lg.shape, 1)
    far = jnp.int32(LANES)

    def top1(vals):
        best = jnp.max(vals, axis=1, keepdims=True)
        return best, jnp.min(jnp.where(vals == best, lane, far), axis=1, keepdims=True)

    is_group = (lane >= N_EXPERTS) & (lane < N_EXPERTS + N_GROUPS)
    g_best, g_lane = top1(jnp.where(is_group, lg, NEG_BIG))
    g_sum = jnp.sum(jnp.where(is_group, jnp.exp(lg - g_best), 0.0), axis=1, keepdims=True)
    g_top = 1.0 / g_sum
    first = (g_lane - N_EXPERTS) * EXPERTS_PER_GROUP
    in_group = (lane >= first) & (lane < first + EXPERTS_PER_GROUP)
    el = jnp.where(in_group, lg, NEG_BIG)
    e_a, i_a = top1(el)
    e_b, i_b = top1(jnp.where(lane == i_a, NEG_BIG, el))
    t = jnp.exp(e_b - e_a)
    w_a = g_top / (1.0 + t)
    w_b = g_top * t / (1.0 + t)

    hot_a = lane == i_a
    hot_b = lane == i_b
    hot = jnp.where((hot_a | hot_b) & (i > 0), 1.0, 0.0)
    before = _dot(lower_ref[...], hot.astype(BF16)) + carry_ref[...]
    rank_a = jnp.sum(jnp.where(hot_a, before, 0.0), axis=1, keepdims=True)
    rank_b = jnp.sum(jnp.where(hot_b, before, 0.0), axis=1, keepdims=True)
    total = carry_ref[...] + jnp.sum(hot, axis=0, keepdims=True)
    carry_ref[...] = total
    cnt_ref[...] = total

    fields = (i_a.astype(F32), i_b.astype(F32), w_a, w_b, rank_a, rank_b)
    route = jnp.zeros(lg.shape, F32)
    for n, f in enumerate(fields):
        route = jnp.where(lane == n, f, route)
    route_ref[...] = route
    rt_ref[...] = route.T[0:ROUTE_FIELDS, :]
    x1s_ref[...] = x1


def _post_mixer(ro, do, proj, xf, mod, w_ret_o, w_diff_o, w_out, norm_ffn, w_rt, b_rt, seq):
    T, D = xf.shape
    tm = min(TM_POST, seq)
    per_batch = seq // tm
    n = T // tm
    cur = lambda i: jnp.minimum(i, n - 1)
    prev = lambda i: jnp.maximum(i - 1, 0)
    row = lambda i: (cur(i), 0)
    prow = lambda i: (prev(i), 0)
    const = lambda i: (0, 0)
    return pl.pallas_call(
        functools.partial(_post_kernel, tm=tm),
        grid=(n + 1,),
        in_specs=[pl.BlockSpec((tm, D), row),
                  pl.BlockSpec((tm, D), row),
                  pl.BlockSpec((tm, D), lambda i: (cur(i), OFF_GR // D)),
                  pl.BlockSpec((tm, D), lambda i: (cur(i), OFF_GD // D)),
                  pl.BlockSpec((tm, D), row),
                  pl.BlockSpec((1, 6, D), lambda i: (cur(i) // per_batch, 0, 0)),
                  pl.BlockSpec((1, 6, D), lambda i: (prev(i) // per_batch, 0, 0)),
                  pl.BlockSpec((D, D), const),
                  pl.BlockSpec((D, D), const),
                  pl.BlockSpec((D, D), const),
                  pl.BlockSpec((1, D), const),
                  pl.BlockSpec((D, LANES), const),
                  pl.BlockSpec((1, LANES), const)],
        out_specs=[pl.BlockSpec((tm, D), row),
                   pl.BlockSpec((tm, D // 2), prow),
                   pl.BlockSpec((tm, LANES), prow),
                   pl.BlockSpec((ROUTE_FIELDS, tm), lambda i: (0, prev(i))),
                   pl.BlockSpec((1, LANES), const)],
        out_shape=[jax.ShapeDtypeStruct((T, D), F32),
                   jax.ShapeDtypeStruct((T, D // 2), jnp.uint32),
                   jax.ShapeDtypeStruct((T, LANES), F32),
                   jax.ShapeDtypeStruct((ROUTE_FIELDS, T), F32),
                   jax.ShapeDtypeStruct((1, LANES), F32)],
        scratch_shapes=[pltpu.VMEM((1, LANES), F32), pltpu.VMEM((tm, D), F32), pltpu.VMEM((tm, tm), BF16)],
        compiler_params=_params("arbitrary"),
        name="post_mixer",
    )(ro, do, proj, proj, xf, mod, mod, w_ret_o, w_diff_o, w_out, norm_ffn.reshape(1, D), w_rt, b_rt)


def _row_copy(src, s, dst, d, sem):
    return pltpu.make_async_copy(src.at[pl.ds(s, 1), :], dst.at[pl.ds(d, 1), :], sem)


def _dest_blocks(dest, tile):
    n = dest.shape[1] // tile
    return dest.reshape(2, n, tile).transpose(1, 0, 2).reshape(n, 1, 2 * tile)


def _scatter_kernel(zb_ref, dest_ref, h_ref, o_hbm, zero_ref, zsem, sem, *, ts, tb):
    i = pl.program_id(0)

    @pl.when(i == 0)
    def _():
        zero_ref[...] = jnp.zeros(zero_ref.shape, zero_ref.dtype)

        def zero_copy(n):
            start = pl.multiple_of(zb_ref[n] * tb, tb)
            return pltpu.make_async_copy(zero_ref, o_hbm.at[pl.ds(start, tb), :], zsem)

        for n in range(zb_ref.shape[0]):
            pl.when(zb_ref[n] >= 0)(lambda n=n: zero_copy(n).start())
        for n in range(zb_ref.shape[0]):
            pl.when(zb_ref[n] >= 0)(lambda n=n: zero_copy(n).wait())

    def issue(r, carry):
        for k in range(2):
            _row_copy(h_ref, r, o_hbm, dest_ref[0, 0, k * ts + r], sem).start(priority=k)
        return carry

    lax.fori_loop(0, ts, issue, 0, unroll=True)
    pltpu.make_async_copy(o_hbm.at[pl.ds(0, 2 * ts), :], o_hbm.at[pl.ds(0, 2 * ts), :], sem).wait()


def _moe_scatter(h2, dest, zero_blocks, n_rows):
    T, D = h2.shape
    ts = min(TS_SCATTER, T)
    tb = TB_MOE
    grid_spec = pltpu.PrefetchScalarGridSpec(
        num_scalar_prefetch=1,
        grid=(T // ts,),
        in_specs=[pl.BlockSpec((1, 1, 2 * ts), lambda i, zb: (i, 0, 0), memory_space=pltpu.SMEM),
                  pl.BlockSpec((ts, D), lambda i, zb: (i, 0))],
        out_specs=pl.BlockSpec(memory_space=pl.ANY),
        scratch_shapes=[pltpu.VMEM((tb, D), h2.dtype), pltpu.SemaphoreType.DMA(()), pltpu.SemaphoreType.DMA(())],
    )
    return pl.pallas_call(
        functools.partial(_scatter_kernel, ts=ts, tb=tb),
        grid_spec=grid_spec,
        out_shape=jax.ShapeDtypeStruct((n_rows, D), h2.dtype),
        compiler_params=_params("arbitrary"),
        name="moe_scatter",
    )(zero_blocks, _dest_blocks(dest, ts), h2)


def _expert_kernel(be_ref, nv_ref, x_ref, w1_ref, w3_ref, w2_ref, o_ref, w1b_ref, w3b_ref, w2b_ref):
    n = pl.program_id(0)
    nv = nv_ref[n]
    new_expert = (n == 0) | (be_ref[n] != be_ref[jnp.maximum(n - 1, 0)])

    @pl.when(new_expert & (nv > 0))
    def _():
        w1b_ref[...] = w1_ref[0].astype(BF16)
        w3b_ref[...] = w3_ref[0].astype(BF16)
        w2b_ref[...] = w2_ref[0].astype(BF16)

    @pl.when(nv > 0)
    def _():
        x_lo, x_hi = _unpack_bf16_pairs(x_ref[...])
        half = x_lo.shape[1]
        g = _dot(x_lo, w1b_ref[:half, :]) + _dot(x_hi, w1b_ref[half:, :])
        u = _dot(x_lo, w3b_ref[:half, :]) + _dot(x_hi, w3b_ref[half:, :])
        a = (g * _sigmoid(g) * u).astype(BF16)
        o_ref[...] = _pack_bf16_pairs(_dot(a, w2b_ref[...]))

    @pl.when(nv == 0)
    def _():
        o_ref[...] = jnp.zeros(o_ref.shape, o_ref.dtype)


def _moe_experts(h_pad, block_e, block_nv, w1, w3, w2):
    P = h_pad.shape[0]
    D = 2 * h_pad.shape[1]
    tb = TB_MOE
    wmap = lambda n, be, nv: (be[n], 0, 0)
    grid_spec = pltpu.PrefetchScalarGridSpec(
        num_scalar_prefetch=2,
        grid=(P // tb,),
        in_specs=[pl.BlockSpec((tb, D // 2), lambda n, be, nv: (n, 0)),
                  pl.BlockSpec((1, D, D_EXPERT), wmap),
                  pl.BlockSpec((1, D, D_EXPERT), wmap),
                  pl.BlockSpec((1, D_EXPERT, D), wmap)],
        out_specs=pl.BlockSpec((tb, D // 2), lambda n, be, nv: (n, 0)),
        scratch_shapes=[pltpu.VMEM((D, D_EXPERT), BF16), pltpu.VMEM((D, D_EXPERT), BF16),
                        pltpu.VMEM((D_EXPERT, D), BF16)],
    )
    return pl.pallas_call(
        _expert_kernel,
        grid_spec=grid_spec,
        out_shape=jax.ShapeDtypeStruct((P, D // 2), h_pad.dtype),
        compiler_params=_params("arbitrary"),
        name="moe_experts",
    )(block_e, block_nv, h_pad, w1, w3, w2)


def _combine_kernel(dcur_ref, dnext_ref, x1_ref, route_ref, mod_ref, nw_ref, y_hbm, o_ref, buf, sem, *, tc, final):
    i = pl.program_id(0)
    slot = i % 2

    def gather(d_ref, s):
        def issue(r, carry):
            for k in range(2):
                _row_copy(y_hbm, d_ref[0, 0, k * tc + r], buf.at[s], k * tc + r, sem.at[s]).start(priority=k)
            return carry

        lax.fori_loop(0, tc, issue, 0, unroll=True)

    pl.when(i == 0)(lambda: gather(dcur_ref, 0))
    pl.when(i + 1 < pl.num_programs(0))(lambda: gather(dnext_ref, 1 - slot))
    pltpu.make_async_copy(y_hbm.at[pl.ds(0, 2 * tc), :], buf.at[slot], sem.at[slot]).wait()
    route = route_ref[...]

    def rows(first):
        lo, hi = _unpack_bf16_pairs(buf[slot, first:first + tc, :])
        return jnp.concatenate([lo, hi], axis=1).astype(F32)

    y = route[:, 2:3] * rows(0) + route[:, 3:4] * rows(tc)
    x2 = x1_ref[...] + mod_ref[0][5:6] * y
    o_ref[...] = _rms(x2) * nw_ref[...] if final else x2


def _moe_combine(y_pad, dest, x1, route, mod, norm_w, seq, final):
    T, D = x1.shape
    tc = min(TC_COMBINE, seq)
    per_batch = seq // tc
    row = lambda i: (i, 0)
    n = T // tc
    dest3 = _dest_blocks(dest, tc)
    return pl.pallas_call(
        functools.partial(_combine_kernel, tc=tc, final=final),
        grid=(n,),
        in_specs=[pl.BlockSpec((1, 1, 2 * tc), lambda i: (i, 0, 0), memory_space=pltpu.SMEM),
                  pl.BlockSpec((1, 1, 2 * tc), lambda i: (jnp.minimum(i + 1, n - 1), 0, 0),
                               memory_space=pltpu.SMEM),
                  pl.BlockSpec((tc, D), row),
                  pl.BlockSpec((tc, LANES), row),
                  pl.BlockSpec((1, 6, D), lambda i: (i // per_batch, 0, 0)),
                  pl.BlockSpec((1, D), lambda i: (0, 0)),
                  pl.BlockSpec(memory_space=pl.ANY)],
        out_specs=pl.BlockSpec((tc, D), row),
        out_shape=jax.ShapeDtypeStruct((T, D), F32),
        scratch_shapes=[pltpu.VMEM((2, 2 * tc, D // 2), y_pad.dtype), pltpu.SemaphoreType.DMA((2,))],
        compiler_params=_params("arbitrary"),
        name="moe_combine",
    )(dest3, dest3, x1, route, mod, norm_w.reshape(1, D), y_pad)


def _dest_kernel(start_ref, rt_ref, o_ref):
    expert = rt_ref[0:2, :].astype(jnp.int32)
    rank = rt_ref[4:6, :].astype(jnp.int32)
    base = jnp.zeros(expert.shape, jnp.int32)
    for e in range(N_EXPERTS):
        base = jnp.where(expert == e, start_ref[e], base)
    o_ref[...] = base + rank


def _dest_slots(rt, region_start):
    T = rt.shape[1]
    grid_spec = pltpu.PrefetchScalarGridSpec(
        num_scalar_prefetch=1,
        grid=(1,),
        in_specs=[pl.BlockSpec((ROUTE_FIELDS, T), lambda i, s: (0, 0))],
        out_specs=pl.BlockSpec((2, T), lambda i, s: (0, 0)),
    )
    return pl.pallas_call(
        _dest_kernel,
        grid_spec=grid_spec,
        out_shape=jax.ShapeDtypeStruct((2, T), jnp.int32),
        compiler_params=_params("arbitrary"),
        name="moe_dest",
    )(region_start, rt)


def _routing_tables(rt, counts, n_blocks):
    tb = TB_MOE
    cnt = counts[0, :N_EXPERTS].astype(jnp.int32)
    nblk = (cnt + tb - 1) // tb
    blk_end = jnp.cumsum(nblk)
    blk_start = blk_end - nblk
    dest = _dest_slots(rt, (blk_start * tb).astype(jnp.int32))
    blocks = jnp.arange(n_blocks, dtype=jnp.int32)
    block_e = jnp.minimum(jnp.sum(blocks[:, None] >= blk_end[None, :], axis=1), N_EXPERTS - 1).astype(jnp.int32)
    left = cnt[block_e] - (blocks - blk_start[block_e]) * tb
    block_nv = jnp.where(blocks < blk_end[-1], jnp.clip(left, 0, tb), 0).astype(jnp.int32)
    tail = blk_end[-1] + jnp.arange(N_EXPERTS, dtype=jnp.int32)
    zero_blocks = jnp.concatenate([jnp.where(nblk > 0, blk_end - 1, -1),
                                   jnp.where(tail < n_blocks, tail, -1)]).astype(jnp.int32)
    return dest.astype(jnp.int32), block_e, block_nv, zero_blocks


def kernel(x, c, positions, w_ada, b_ada, norm_mix, w_in, w_ret_o, w_diff_o, lam_q1, lam_k1, lam_q2, lam_k2,
           diff_norm, w_out, norm_ffn, w_router_group, b_router_group, w_router_expert, b_router_expert,
           w_exp_gate, w_exp_up, w_exp_down, norm_final):
    B, S, D = x.shape
    T = B * S
    depth = w_ada.shape[0]
    assert D_IN == w_in.shape[2] and S % RET_CHUNK == 0
    xf = x.reshape(T, D)
    tm_in = min(TM_INPROJ, S)
    pos = positions.reshape(T // tm_in, 1, tm_in)
    n_blocks = (2 * T) // TB_MOE + N_EXPERTS
    for l in range(depth):
        lambda_init = 0.8 - 0.6 * math.exp(-0.3 * l)
        mod = _adaln(c, w_ada[l], b_ada[l]).reshape(B, 6, D)
        proj = _inproj(xf, pos, mod, norm_mix[l], w_in[l].astype(BF16), S)
        ro = _retention(proj, B, S)
        lam4 = jnp.stack([lam_q1[l], lam_k1[l], lam_q2[l], lam_k2[l]]).astype(F32)
        do = _diff_attention(proj, lam4, diff_norm[l].astype(F32), lambda_init, B, S)

        pad = LANES - N_EXPERTS - N_GROUPS
        w_rt = jnp.concatenate([w_router_expert[l], w_router_group[l], jnp.zeros((D, pad), F32)], axis=1)
        b_rt = jnp.concatenate([b_router_expert[l], b_router_group[l], jnp.zeros((pad,), F32)]).reshape(1, LANES)
        x1, h2, route, rt, counts = _post_mixer(
            ro, do, proj, xf, mod, w_ret_o[l].astype(BF16), w_diff_o[l].astype(BF16), w_out[l].astype(BF16),
            norm_ffn[l], w_rt.astype(BF16), b_rt, S)

        dest, block_e, block_nv, zero_blocks = _routing_tables(rt, counts, n_blocks)
        h_pad = _moe_scatter(h2, dest, zero_blocks, n_blocks * TB_MOE)
        y_pad = _moe_experts(h_pad, block_e, block_nv, w_exp_gate[l], w_exp_up[l], w_exp_down[l])
        xf = _moe_combine(y_pad, dest, x1, route, mod, norm_final, S, l == depth - 1)
    return xf.reshape(B, S, D)
```

```python
import functools
import math

import jax
import jax.numpy as jnp
from jax import lax
from jax.experimental import pallas as pl
from jax.experimental.pallas import tpu as pltpu

F32 = jnp.float32
BF16 = jnp.bfloat16

EPS = 1e-6
ROPE_THETA = 10000.0
LANES = 128
RET_HEADS, RET_DK, RET_DV, RET_CHUNK = 4, 128, 256, 128
DIFF_HEADS, DIFF_DK, DIFF_DV = 4, 128, 256
N_GROUPS, EXPERTS_PER_GROUP, N_EXPERTS, D_EXPERT = 4, 8, 32, 512
OFF_RQ, OFF_RK, OFF_RV, OFF_RG, OFF_DQ, OFF_DK, OFF_DV, OFF_GR, OFF_GD, D_IN = (
    0, 512, 1024, 2048, 3072, 4096, 5120, 6144, 7168, 8192)
NEG_BIG = -1e30
ROUTE_FIELDS = 8
VMEM_LIMIT_BYTES = 48 * 1024 * 1024
VMEM_LIMIT_INPROJ_BYTES = 56 * 1024 * 1024

TM_INPROJ, TN_INPROJ = 512, 1024
TR_RETENTION = 512
BQ_ATTN = 512
HEADS_PER_ATTN_STEP = 2
TM_POST = 512
TB_MOE = 512
TS_SCATTER = 512
TC_COMBINE = 512


def _params(*sem):
    return pltpu.CompilerParams(dimension_semantics=sem, vmem_limit_bytes=VMEM_LIMIT_BYTES)


def _sigmoid(v):
    return 1.0 / (1.0 + jnp.exp(-v))


def _rms(v):
    return v * lax.rsqrt(jnp.mean(v * v, axis=-1, keepdims=True) + EPS)


def _dot(a, b):
    return jnp.dot(a, b, preferred_element_type=F32)


def _dot_nt(a, b):
    return lax.dot_general(a, b, (((1,), (1,)), ((), ())), preferred_element_type=F32)


def _pack_bf16_pairs(x):
    m = x.shape[1] // 2

    def rounded_bits(v):
        return lax.bitcast_convert_type(v.astype(BF16).astype(F32), jnp.uint32)

    return (rounded_bits(x[:, :m]) >> 16) | rounded_bits(x[:, m:])


def _unpack_bf16_pairs(w):
    lo = lax.bitcast_convert_type(w << 16, F32)
    hi = lax.bitcast_convert_type(w & jnp.uint32(0xFFFF0000), F32)
    return lo.astype(BF16), hi.astype(BF16)


def _ada_kernel(c_ref, w_ref, b_ref, o_ref):
    c = c_ref[...]
    a = (c * _sigmoid(c)).astype(BF16)
    o_ref[...] = _dot(a, w_ref[...].astype(BF16)) + b_ref[...]


def _adaln(c, w, b):
    B, D = c.shape
    n = w.shape[1] // D
    return pl.pallas_call(
        _ada_kernel,
        grid=(n,),
        in_specs=[pl.BlockSpec((B, D), lambda j: (0, 0)),
                  pl.BlockSpec((D, D), lambda j: (0, j)),
                  pl.BlockSpec((1, D), lambda j: (0, j))],
        out_specs=pl.BlockSpec((B, D), lambda j: (0, j)),
        out_shape=jax.ShapeDtypeStruct((B, n * D), F32),
        compiler_params=_params("parallel"),
        name="adaln",
    )(c, w, b.reshape(1, -1))


ROPE_GROUPS = ((OFF_RQ, OFF_RK, 1.0), (OFF_RK, OFF_RV, RET_DK ** -0.5),
               (OFF_DQ, OFF_DK, DIFF_DK ** -0.5 * math.log2(math.e)), (OFF_DK, OFF_DV, 1.0))


def _rope_scale(col):
    for lo, hi, scale in ROPE_GROUPS:
        if lo <= col < hi:
            return scale
    return None


def _inproj_kernel(x0_ref, pos0_ref, mod0_ref, xn_ref, posn_ref, modn_ref, nw_ref, inv_ref,
                   w_ref, o_ref, h_ref, cos_ref, sin_ref, *, tn):
    i = pl.program_id(0)

    def prepare(x, pos, mod, slot):
        h = _rms(x) * nw_ref[...] * (1.0 + mod[1:2]) + mod[0:1]
        h_ref[slot] = h.astype(BF16)
        pos_t = jnp.broadcast_to(pos.astype(F32), (LANES, pos.shape[1])).T
        half = pos_t.shape[0] // 2
        low = lax.broadcasted_iota(jnp.int32, (half, LANES), 1) < LANES // 2
        ang = jnp.where(low, pos_t[:half], pos_t[half:]) * inv_ref[...]
        cos = jnp.cos(ang)
        sin = jnp.sin(ang)
        cos_r = pltpu.roll(cos, LANES // 2, 1)
        sin_r = pltpu.roll(sin, LANES // 2, 1)
        cos_ref[slot, :half] = jnp.where(low, cos, cos_r)
        cos_ref[slot, half:] = jnp.where(low, cos_r, cos)
        sin_ref[slot, :half] = jnp.where(low, -sin, sin_r)
        sin_ref[slot, half:] = jnp.where(low, -sin_r, sin)

    @pl.when(i == 0)
    def _():
        prepare(x0_ref[...], pos0_ref[0], mod0_ref[0], 0)

    cur = i % 2
    for g in range(o_ref.shape[1] // tn):
        acc = _dot(h_ref[cur], w_ref[:, g * tn:(g + 1) * tn])
        for k in range(tn // LANES):
            first = g * tn + k * LANES
            a = acc[:, k * LANES:(k + 1) * LANES]
            scale = _rope_scale(first)
            if scale is not None:
                a = a * cos_ref[cur] + pltpu.roll(a, LANES // 2, 1) * sin_ref[cur]
                if scale != 1.0:
                    a = a * scale
            o_ref[:, first:first + LANES] = a.astype(BF16)

    prepare(xn_ref[...], posn_ref[0], modn_ref[0], 1 - cur)


def _inproj(xf, pos, mod, norm_w, w_bf16, seq):
    T, D = xf.shape
    tm = min(TM_INPROJ, seq)
    tn = TN_INPROJ
    half = LANES // 2
    inv = ROPE_THETA ** (-jnp.arange(0, LANES, 2, dtype=F32) / LANES)
    inv = jnp.concatenate([inv, inv]).reshape(1, LANES)
    per_batch = seq // tm
    n_m = T // tm
    nxt = lambda i: jnp.minimum(i + 1, n_m - 1)
    once = dict(pipeline_mode=pl.Buffered(1))
    return pl.pallas_call(
        functools.partial(_inproj_kernel, tn=tn),
        grid=(n_m,),
        in_specs=[pl.BlockSpec((tm, D), lambda i: (0, 0), **once),
                  pl.BlockSpec((1, 1, tm), lambda i: (0, 0, 0), **once),
                  pl.BlockSpec((1, 6, D), lambda i: (0, 0, 0), **once),
                  pl.BlockSpec((tm, D), lambda i: (nxt(i), 0)),
                  pl.BlockSpec((1, 1, tm), lambda i: (nxt(i), 0, 0)),
                  pl.BlockSpec((1, 6, D), lambda i: (nxt(i) // per_batch, 0, 0)),
                  pl.BlockSpec((1, D), lambda i: (0, 0), **once),
                  pl.BlockSpec((1, LANES), lambda i: (0, 0), **once),
                  pl.BlockSpec((D, D_IN), lambda i: (0, 0), **once)],
        out_specs=pl.BlockSpec((tm, D_IN), lambda i: (i, 0)),
        out_shape=jax.ShapeDtypeStruct((T, D_IN), BF16),
        scratch_shapes=[pltpu.VMEM((2, tm, D), BF16),
                        pltpu.VMEM((2, tm, LANES), F32),
                        pltpu.VMEM((2, tm, LANES), F32)],
        compiler_params=pltpu.CompilerParams(dimension_semantics=("arbitrary",),
                                             vmem_limit_bytes=VMEM_LIMIT_INPROJ_BYTES),
        name="inproj",
    )(xf, pos, mod, xf, pos, mod, norm_w.reshape(1, D), inv, w_bf16)


def _ret_kernel(q_ref, k_ref, v_ref, g_ref, dmask_ref, xi_ref, zeta_ref, cd_ref, o_ref, state_ref):
    @pl.when(pl.program_id(1) == 0)
    def _():
        state_ref[...] = jnp.zeros(state_ref.shape, F32)

    C = RET_CHUNK
    for h in range(RET_HEADS):
        qk = slice(h * RET_DK, (h + 1) * RET_DK)
        vv = slice(h * RET_DV, (h + 1) * RET_DV)
        st = state_ref[h]
        for n in range(q_ref.shape[0] // C):
            rows = slice(n * C, (n + 1) * C)
            q = q_ref[rows, qk]
            k = k_ref[rows, qk]
            v = v_ref[rows, vv]
            s = _dot_nt(q, k) * dmask_ref[h]
            qx = (q.astype(F32) * xi_ref[h]).astype(BF16)
            o = _dot(s.astype(BF16), v) + _dot(qx, st.astype(BF16))
            kz_t = (k.astype(F32) * zeta_ref[h]).T.astype(BF16)
            st = cd_ref[h] * st + _dot(kz_t, v)
            g = g_ref[rows, vv].astype(F32)
            o_ref[rows, vv] = (_rms(o) * (g * _sigmoid(g))).astype(BF16)
        state_ref[h] = st


def _retention(proj, batch, seq):
    T = proj.shape[0]
    C = RET_CHUNK
    H = RET_HEADS
    nc = seq // C
    gamma = 1.0 - jnp.exp2(-5.0 - jnp.arange(H, dtype=F32))
    log_g = jnp.log(gamma)
    idx = jnp.arange(C, dtype=F32)
    rel = idx[:, None] - idx[None, :]
    dmask = jnp.where(rel >= 0, jnp.exp(log_g[:, None, None] * jnp.maximum(rel, 0.0)), 0.0)
    zeta = jnp.exp(log_g[:, None] * (C - 1 - idx))
    xi = jnp.exp(log_g[:, None] * (idx + 1))
    cd = jnp.exp(log_g * C)
    zeta_b = jnp.broadcast_to(zeta[:, :, None], (H, C, RET_DK))
    xi_b = jnp.broadcast_to(xi[:, :, None], (H, C, RET_DK))
    cd_b = jnp.broadcast_to(cd[:, None, None], (H, 1, RET_DV))
    wq = H * RET_DK
    wv = H * RET_DV
    tr = min(TR_RETENTION, seq)
    ns = seq // tr
    row = lambda b, n: b * ns + n
    const3 = lambda b, n: (0, 0, 0)
    return pl.pallas_call(
        _ret_kernel,
        grid=(batch, ns),
        in_specs=[pl.BlockSpec((tr, wq), lambda b, n: (row(b, n), OFF_RQ // wq)),
                  pl.BlockSpec((tr, wq), lambda b, n: (row(b, n), OFF_RK // wq)),
                  pl.BlockSpec((tr, wv), lambda b, n: (row(b, n), OFF_RV // wv)),
                  pl.BlockSpec((tr, wv), lambda b, n: (row(b, n), OFF_RG // wv)),
                  pl.BlockSpec((H, C, C), const3),
                  pl.BlockSpec((H, C, RET_DK), const3),
                  pl.BlockSpec((H, C, RET_DK), const3),
                  pl.BlockSpec((H, 1, RET_DV), const3)],
        out_specs=pl.BlockSpec((tr, wv), lambda b, n: (row(b, n), 0)),
        out_shape=jax.ShapeDtypeStruct((T, wv), BF16),
        scratch_shapes=[pltpu.VMEM((H, RET_DK, RET_DV), F32)],
        compiler_params=_params("parallel", "arbitrary"),
        name="retention",
    )(proj, proj, proj, proj, dmask, xi_b, zeta_b, cd_b)


def _transpose_bf16(a):
    return a.astype(F32).T.astype(BF16)


def _attn_kernel(lam_ref, dn_ref, *refs, bq, bk, heads, lambda_init):
    in_refs, (o_ref, vt_ref, qt_ref, s_ref, pm_ref, m_ref, l_ref, acc_ref) = refs[:5 * heads], refs[5 * heads:]
    q_refs = [in_refs[5 * h + c] for h in range(heads) for c in range(2)]
    k_refs = [in_refs[5 * h + 2 + c] for h in range(heads) for c in range(2)]
    v_refs = [in_refs[5 * h + 4] for h in range(heads)]
    streams = range(2 * heads)
    i = pl.program_id(2)
    n_sub = vt_ref.shape[1]

    @pl.when(i == 0)
    def _():
        for h in range(heads):
            for t in range(n_sub):
                vt_ref[h, t] = _transpose_bf16(v_refs[h][t * bk:(t + 1) * bk, :])

    m_ref[...] = jnp.full(m_ref.shape, NEG_BIG, F32)
    l_ref[...] = jnp.zeros(l_ref.shape, F32)
    acc_ref[...] = jnp.zeros(acc_ref.shape, F32)
    for n in streams:
        qt_ref[n] = _transpose_bf16(q_refs[n][...])

    def score(t, slot, diagonal_offset=None, q_from=0):
        start = pl.multiple_of(t * bk, bk)
        for n in streams:
            st = _dot(k_refs[n][pl.ds(start, bk), :], qt_ref[n, :, q_from:])
            if diagonal_offset is not None:
                key = lax.broadcasted_iota(jnp.int32, st.shape, 0) + diagonal_offset
                qry = lax.broadcasted_iota(jnp.int32, st.shape, 1) + q_from
                st = jnp.where(key <= qry, st, NEG_BIG)
            s_ref[slot, n, :, q_from:] = st
            pm_ref[slot, n, :, q_from:] = jnp.max(st, axis=0, keepdims=True)

    def accumulate(t, slot, q_from=0):
        for n in streams:
            m_prev = m_ref[n, :, q_from:]
            m_new = jnp.maximum(m_prev, pm_ref[slot, n, :, q_from:])
            p = jnp.exp2(s_ref[slot, n, :, q_from:] - m_new)
            alpha = jnp.exp2(m_prev - m_new)
            l_ref[n, :, q_from:] = alpha * l_ref[n, :, q_from:] + jnp.sum(p, axis=0, keepdims=True)
            acc_ref[n, :, q_from:] = (alpha * acc_ref[n, :, q_from:]
                                      + _dot(vt_ref[n // 2, t], p.astype(BF16)))
            m_ref[n, :, q_from:] = m_new

    def pair(jj, next_is_diagonal):
        t = 2 * jj
        score(t + 1, 1)
        accumulate(t, 0)
        score(t + 2, 0, 0 if next_is_diagonal else None)
        accumulate(t + 1, 1)

    pl.when(i == 0)(lambda: score(0, 0, 0))
    pl.when(i > 0)(lambda: score(0, 0))

    def body(jj, carry):
        pair(jj, False)
        return carry

    lax.fori_loop(0, i - 1, body, 0)
    pl.when(i > 0)(lambda: pair(i - 1, True))
    score(2 * i + 1, 1, bk, q_from=bk)
    accumulate(2 * i, 0)
    accumulate(2 * i + 1, 1, q_from=bk)

    lam4 = lam_ref[...]
    lam = (jnp.exp(jnp.sum(lam4[0:1] * lam4[1:2], axis=1, keepdims=True))
           - jnp.exp(jnp.sum(lam4[2:3] * lam4[3:4], axis=1, keepdims=True)) + lambda_init)
    dv = acc_ref.shape[1]
    for h in range(heads):
        a, b = 2 * h, 2 * h + 1
        ot = acc_ref[a] * (1.0 / l_ref[a]) - lam * (acc_ref[b] * (1.0 / l_ref[b]))
        ot = ot * lax.rsqrt(jnp.mean(ot * ot, axis=0, keepdims=True) + EPS)
        o_ref[:, h * dv:(h + 1) * dv] = (ot.T * dn_ref[...] * (1.0 - lambda_init)).astype(BF16)


def _diff_attention(proj, lam4, diff_norm, lambda_init, batch, seq):
    T = proj.shape[0]
    H = DIFF_HEADS
    hp = HEADS_PER_ATTN_STEP
    bq = min(BQ_ATTN, seq)
    bk = bq // 2
    nq = seq // bq
    q0 = OFF_DQ // DIFF_DK
    k0 = OFF_DK // DIFF_DK
    v0 = OFF_DV // DIFF_DV
    head_specs = []
    for j in range(hp):
        head = lambda g, j=j: hp * g + j
        head_specs += [
            pl.BlockSpec((bq, DIFF_DK), lambda b, g, i, head=head: (b * nq + i, q0 + 2 * head(g))),
            pl.BlockSpec((bq, DIFF_DK), lambda b, g, i, head=head: (b * nq + i, q0 + 2 * head(g) + 1)),
            pl.BlockSpec((seq, DIFF_DK), lambda b, g, i, head=head: (b, k0 + 2 * head(g))),
            pl.BlockSpec((seq, DIFF_DK), lambda b, g, i, head=head: (b, k0 + 2 * head(g) + 1)),
            pl.BlockSpec((seq, DIFF_DV), lambda b, g, i, head=head: (b, v0 + head(g)))]
    ns = 2 * hp
    return pl.pallas_call(
        functools.partial(_attn_kernel, bq=bq, bk=bk, heads=hp, lambda_init=lambda_init),
        grid=(batch, H // hp, nq),
        in_specs=[pl.BlockSpec((4, DIFF_DK), lambda b, g, i: (0, 0)),
                  pl.BlockSpec((1, DIFF_DV), lambda b, g, i: (0, 0))] + head_specs,
        out_specs=pl.BlockSpec((bq, hp * DIFF_DV), lambda b, g, i: (b * nq + i, g)),
        out_shape=jax.ShapeDtypeStruct((T, H * DIFF_DV), BF16),
        scratch_shapes=[pltpu.VMEM((hp, seq // bk, DIFF_DV, bk), BF16),
                        pltpu.VMEM((ns, DIFF_DK, bq), BF16),
                        pltpu.VMEM((2, ns, bk, bq), F32),
                        pltpu.VMEM((2, ns, 1, bq), F32),
                        pltpu.VMEM((ns, 1, bq), F32),
                        pltpu.VMEM((ns, 1, bq), F32),
                        pltpu.VMEM((ns, DIFF_DV, bq), F32)],
        compiler_params=_params("parallel", "parallel", "arbitrary"),
        name="diff_attn",
    )(lam4, diff_norm.reshape(1, DIFF_DV), *([proj] * (5 * hp)))


def _post_kernel(ro_ref, do_ref, gr_ref, gd_ref, x_ref, mod_ref, modp_ref, wr_ref, wd_ref, wo_ref, nf_ref,
                 wrt_ref, brt_ref, x1_ref, h2_ref, route_ref, rt_ref, cnt_ref, carry_ref, x1s_ref, lower_ref,
                 *, tm):
    i = pl.program_id(0)

    @pl.when(i == 0)
    def _():
        carry_ref[...] = jnp.zeros(carry_ref.shape, F32)
        x1s_ref[...] = jnp.zeros(x1s_ref.shape, F32)
        r_i = lax.broadcasted_iota(jnp.int32, (tm, tm), 0)
        c_i = lax.broadcasted_iota(jnp.int32, (tm, tm), 1)
        lower_ref[...] = jnp.where(c_i < r_i, 1.0, 0.0).astype(BF16)

    modp = modp_ref[0]
    h2 = _rms(x1s_ref[...]) * nf_ref[...] * (1.0 + modp[4:5]) + modp[3:4]
    h2_ref[...] = _pack_bf16_pairs(h2)
    lg = _dot(h2.astype(BF16), wrt_ref[...]) + brt_ref[...]

    mod = mod_ref[0]
    ret_out = _dot(ro_ref[...], wr_ref[...])
    diff_out = _dot(do_ref[...], wd_ref[...])
    merged = (_sigmoid(gr_ref[...].astype(F32)) * ret_out
              + _sigmoid(gd_ref[...].astype(F32)) * diff_out)
    x1 = x_ref[...] + mod[2:3] * _dot(merged.astype(BF16), wo_ref[...])
    x1_ref[...] = x1

    lane = lax.broadcasted_iota(jnp.int32, lg.shape, 1)
    far = jnp.int32(LANES)

    def top1(vals):
        best = jnp.max(vals, axis=1, keepdims=True)
        return best, jnp.min(jnp.where(vals == best, lane, far), axis=1, keepdims=True)

    is_group = (lane >= N_EXPERTS) & (lane < N_EXPERTS + N_GROUPS)
    g_best, g_lane = top1(jnp.where(is_group, lg, NEG_BIG))
    g_sum = jnp.sum(jnp.where(is_group, jnp.exp(lg - g_best), 0.0), axis=1, keepdims=True)
    g_top = 1.0 / g_sum
    first = (g_lane - N_EXPERTS) * EXPERTS_PER_GROUP
    in_group = (lane >= first) & (lane < first + EXPERTS_PER_GROUP)
    el = jnp.where(in_group, lg, NEG_BIG)
    e_a, i_a = top1(el)
    e_b, i_b = top1(jnp.where(lane == i_a, NEG_BIG, el))
    t = jnp.exp(e_b - e_a)
    w_a = g_top / (1.0 + t)
    w_b = g_top * t / (1.0 + t)

    hot_a = lane == i_a
    hot_b = lane == i_b
    hot = jnp.where((hot_a | hot_b) & (i > 0), 1.0, 0.0)
    before = _dot(lower_ref[...], hot.astype(BF16)) + carry_ref[...]
    rank_a = jnp.sum(jnp.where(hot_a, before, 0.0), axis=1, keepdims=True)
    rank_b = jnp.sum(jnp.where(hot_b, before, 0.0), axis=1, keepdims=True)
    total = carry_ref[...] + jnp.sum(hot, axis=0, keepdims=True)
    carry_ref[...] = total
    cnt_ref[...] = total

    fields = (i_a.astype(F32), i_b.astype(F32), w_a, w_b, rank_a, rank_b)
    route = jnp.zeros(lg.shape, F32)
    for n, f in enumerate(fields):
        route = jnp.where(lane == n, f, route)
    route_ref[...] = route
    rt_ref[...] = route.T[0:ROUTE_FIELDS, :]
    x1s_ref[...] = x1


def _post_mixer(ro, do, proj, xf, mod, w_ret_o, w_diff_o, w_out, norm_ffn, w_rt, b_rt, seq):
    T, D = xf.shape
    tm = min(TM_POST, seq)
    per_batch = seq // tm
    n = T // tm
    cur = lambda i: jnp.minimum(i, n - 1)
    prev = lambda i: jnp.maximum(i - 1, 0)
    row = lambda i: (cur(i), 0)
    prow = lambda i: (prev(i), 0)
    const = lambda i: (0, 0)
    return pl.pallas_call(
        functools.partial(_post_kernel, tm=tm),
        grid=(n + 1,),
        in_specs=[pl.BlockSpec((tm, D), row),
                  pl.BlockSpec((tm, D), row),
                  pl.BlockSpec((tm, D), lambda i: (cur(i), OFF_GR // D)),
                  pl.BlockSpec((tm, D), lambda i: (cur(i), OFF_GD // D)),
                  pl.BlockSpec((tm, D), row),
                  pl.BlockSpec((1, 6, D), lambda i: (cur(i) // per_batch, 0, 0)),
                  pl.BlockSpec((1, 6, D), lambda i: (prev(i) // per_batch, 0, 0)),
                  pl.BlockSpec((D, D), const),
                  pl.BlockSpec((D, D), const),
                  pl.BlockSpec((D, D), const),
                  pl.BlockSpec((1, D), const),
                  pl.BlockSpec((D, LANES), const),
                  pl.BlockSpec((1, LANES), const)],
        out_specs=[pl.BlockSpec((tm, D), row),
                   pl.BlockSpec((tm, D // 2), prow),
                   pl.BlockSpec((tm, LANES), prow),
                   pl.BlockSpec((ROUTE_FIELDS, tm), lambda i: (0, prev(i))),
                   pl.BlockSpec((1, LANES), const)],
        out_shape=[jax.ShapeDtypeStruct((T, D), F32),
                   jax.ShapeDtypeStruct((T, D // 2), jnp.uint32),
                   jax.ShapeDtypeStruct((T, LANES), F32),
                   jax.ShapeDtypeStruct((ROUTE_FIELDS, T), F32),
                   jax.ShapeDtypeStruct((1, LANES), F32)],
        scratch_shapes=[pltpu.VMEM((1, LANES), F32), pltpu.VMEM((tm, D), F32), pltpu.VMEM((tm, tm), BF16)],
        compiler_params=_params("arbitrary"),
        name="post_mixer",
    )(ro, do, proj, proj, xf, mod, mod, w_ret_o, w_diff_o, w_out, norm_ffn.reshape(1, D), w_rt, b_rt)


def _row_copy(src, s, dst, d, sem):
    return pltpu.make_async_copy(src.at[pl.ds(s, 1), :], dst.at[pl.ds(d, 1), :], sem)


def _dest_blocks(dest, tile):
    n = dest.shape[1] // tile
    return dest.reshape(2, n, tile).transpose(1, 0, 2).reshape(n, 1, 2 * tile)


def _scatter_kernel(zb_ref, dest_ref, h_ref, o_hbm, zero_ref, zsem, sem, *, ts, tb):
    i = pl.program_id(0)

    @pl.when(i == 0)
    def _():
        zero_ref[...] = jnp.zeros(zero_ref.shape, zero_ref.dtype)

        def zero_copy(n):
            start = pl.multiple_of(zb_ref[n] * tb, tb)
            return pltpu.make_async_copy(zero_ref, o_hbm.at[pl.ds(start, tb), :], zsem)

        for n in range(zb_ref.shape[0]):
            pl.when(zb_ref[n] >= 0)(lambda n=n: zero_copy(n).start())
        for n in range(zb_ref.shape[0]):
            pl.when(zb_ref[n] >= 0)(lambda n=n: zero_copy(n).wait())

    def issue(r, carry):
        for k in range(2):
            _row_copy(h_ref, r, o_hbm, dest_ref[0, 0, k * ts + r], sem).start(priority=k)
        return carry

    lax.fori_loop(0, ts, issue, 0, unroll=True)
    pltpu.make_async_copy(o_hbm.at[pl.ds(0, 2 * ts), :], o_hbm.at[pl.ds(0, 2 * ts), :], sem).wait()


def _moe_scatter(h2, dest, zero_blocks, n_rows):
    T, D = h2.shape
    ts = min(TS_SCATTER, T)
    tb = TB_MOE
    grid_spec = pltpu.PrefetchScalarGridSpec(
        num_scalar_prefetch=1,
        grid=(T // ts,),
        in_specs=[pl.BlockSpec((1, 1, 2 * ts), lambda i, zb: (i, 0, 0), memory_space=pltpu.SMEM),
                  pl.BlockSpec((ts, D), lambda i, zb: (i, 0))],
        out_specs=pl.BlockSpec(memory_space=pl.ANY),
        scratch_shapes=[pltpu.VMEM((tb, D), h2.dtype), pltpu.SemaphoreType.DMA(()), pltpu.SemaphoreType.DMA(())],
    )
    return pl.pallas_call(
        functools.partial(_scatter_kernel, ts=ts, tb=tb),
        grid_spec=grid_spec,
        out_shape=jax.ShapeDtypeStruct((n_rows, D), h2.dtype),
        compiler_params=_params("arbitrary"),
        name="moe_scatter",
    )(zero_blocks, _dest_blocks(dest, ts), h2)


def _expert_kernel(be_ref, nv_ref, x_ref, w1_ref, w3_ref, w2_ref, o_ref, w1b_ref, w3b_ref, w2b_ref):
    n = pl.program_id(0)
    nv = nv_ref[n]
    new_expert = (n == 0) | (be_ref[n] != be_ref[jnp.maximum(n - 1, 0)])

    @pl.when(new_expert & (nv > 0))
    def _():
        w1b_ref[...] = w1_ref[0].astype(BF16)
        w3b_ref[...] = w3_ref[0].astype(BF16)
        w2b_ref[...] = w2_ref[0].astype(BF16)

    @pl.when(nv > 0)
    def _():
        x_lo, x_hi = _unpack_bf16_pairs(x_ref[...])
        half = x_lo.shape[1]
        g = _dot(x_lo, w1b_ref[:half, :]) + _dot(x_hi, w1b_ref[half:, :])
        u = _dot(x_lo, w3b_ref[:half, :]) + _dot(x_hi, w3b_ref[half:, :])
        a = (g * _sigmoid(g) * u).astype(BF16)
        o_ref[...] = _pack_bf16_pairs(_dot(a, w2b_ref[...]))

    @pl.when(nv == 0)
    def _():
        o_ref[...] = jnp.zeros(o_ref.shape, o_ref.dtype)


def _moe_experts(h_pad, block_e, block_nv, w1, w3, w2):
    P = h_pad.shape[0]
    D = 2 * h_pad.shape[1]
    tb = TB_MOE
    wmap = lambda n, be, nv: (be[n], 0, 0)
    grid_spec = pltpu.PrefetchScalarGridSpec(
        num_scalar_prefetch=2,
        grid=(P // tb,),
        in_specs=[pl.BlockSpec((tb, D // 2), lambda n, be, nv: (n, 0)),
                  pl.BlockSpec((1, D, D_EXPERT), wmap),
                  pl.BlockSpec((1, D, D_EXPERT), wmap),
                  pl.BlockSpec((1, D_EXPERT, D), wmap)],
        out_specs=pl.BlockSpec((tb, D // 2), lambda n, be, nv: (n, 0)),
        scratch_shapes=[pltpu.VMEM((D, D_EXPERT), BF16), pltpu.VMEM((D, D_EXPERT), BF16),
                        pltpu.VMEM((D_EXPERT, D), BF16)],
    )
    return pl.pallas_call(
        _expert_kernel,
        grid_spec=grid_spec,
        out_shape=jax.ShapeDtypeStruct((P, D // 2), h_pad.dtype),
        compiler_params=_params("arbitrary"),
        name="moe_experts",
    )(block_e, block_nv, h_pad, w1, w3, w2)


def _combine_kernel(dcur_ref, dnext_ref, x1_ref, route_ref, mod_ref, nw_ref, y_hbm, o_ref, buf, sem, *, tc, final):
    i = pl.program_id(0)
    slot = i % 2

    def gather(d_ref, s):
        def issue(r, carry):
            for k in range(2):
                _row_copy(y_hbm, d_ref[0, 0, k * tc + r], buf.at[s], k * tc + r, sem.at[s]).start(priority=k)
            return carry

        lax.fori_loop(0, tc, issue, 0, unroll=True)

    def wait_slot(s):
        pltpu.make_async_copy(y_hbm.at[pl.ds(0, 2 * tc), :], buf.at[s], sem.at[s]).wait()

    pl.when(i == 0)(lambda: gather(dcur_ref, 0))
    wait_slot(slot)
    gather(dnext_ref, 1 - slot)
    route = route_ref[...]

    def rows(first):
        lo, hi = _unpack_bf16_pairs(buf[slot, first:first + tc, :])
        return jnp.concatenate([lo, hi], axis=1).astype(F32)

    y = route[:, 2:3] * rows(0) + route[:, 3:4] * rows(tc)
    x2 = x1_ref[...] + mod_ref[0][5:6] * y
    o_ref[...] = _rms(x2) * nw_ref[...] if final else x2
    pl.when(i == pl.num_programs(0) - 1)(lambda: wait_slot(1 - slot))


def _moe_combine(y_pad, dest, x1, route, mod, norm_w, seq, final):
    T, D = x1.shape
    tc = min(TC_COMBINE, seq)
    per_batch = seq // tc
    row = lambda i: (i, 0)
    n = T // tc
    dest3 = _dest_blocks(dest, tc)
    return pl.pallas_call(
        functools.partial(_combine_kernel, tc=tc, final=final),
        grid=(n,),
        in_specs=[pl.BlockSpec((1, 1, 2 * tc), lambda i: (i, 0, 0), memory_space=pltpu.SMEM),
                  pl.BlockSpec((1, 1, 2 * tc), lambda i: (jnp.minimum(i + 1, n - 1), 0, 0),
                               memory_space=pltpu.SMEM),
                  pl.BlockSpec((tc, D), row),
                  pl.BlockSpec((tc, LANES), row),
                  pl.BlockSpec((1, 6, D), lambda i: (i // per_batch, 0, 0)),
                  pl.BlockSpec((1, D), lambda i: (0, 0)),
                  pl.BlockSpec(memory_space=pl.ANY)],
        out_specs=pl.BlockSpec((tc, D), row),
        out_shape=jax.ShapeDtypeStruct((T, D), F32),
        scratch_shapes=[pltpu.VMEM((2, 2 * tc, D // 2), y_pad.dtype), pltpu.SemaphoreType.DMA((2,))],
        compiler_params=_params("arbitrary"),
        name="moe_combine",
    )(dest3, dest3, x1, route, mod, norm_w.reshape(1, D), y_pad)


def _dest_kernel(start_ref, rt_ref, o_ref):
    expert = rt_ref[0:2, :].astype(jnp.int32)
    rank = rt_ref[4:6, :].astype(jnp.int32)
    base = jnp.zeros(expert.shape, jnp.int32)
    for e in range(N_EXPERTS):
        base = jnp.where(expert == e, start_ref[e], base)
    o_ref[...] = base + rank


def _dest_slots(rt, region_start):
    T = rt.shape[1]
    grid_spec = pltpu.PrefetchScalarGridSpec(
        num_scalar_prefetch=1,
        grid=(1,),
        in_specs=[pl.BlockSpec((ROUTE_FIELDS, T), lambda i, s: (0, 0))],
        out_specs=pl.BlockSpec((2, T), lambda i, s: (0, 0)),
    )
    return pl.pallas_call(
        _dest_kernel,
        grid_spec=grid_spec,
        out_shape=jax.ShapeDtypeStruct((2, T), jnp.int32),
        compiler_params=_params("arbitrary"),
        name="moe_dest",
    )(region_start, rt)


def _routing_tables(rt, counts, n_blocks):
    tb = TB_MOE
    cnt = counts[0, :N_EXPERTS].astype(jnp.int32)
    nblk = (cnt + tb - 1) // tb
    blk_end = jnp.cumsum(nblk)
    blk_start = blk_end - nblk
    dest = _dest_slots(rt, (blk_start * tb).astype(jnp.int32))
    blocks = jnp.arange(n_blocks, dtype=jnp.int32)
    block_e = jnp.minimum(jnp.sum(blocks[:, None] >= blk_end[None, :], axis=1), N_EXPERTS - 1).astype(jnp.int32)
    mine = block_e[:, None] == jnp.arange(N_EXPERTS, dtype=jnp.int32)[None, :]
    left = jnp.sum(jnp.where(mine, cnt[None, :] - (blocks[:, None] - blk_start[None, :]) * tb, 0), axis=1)
    block_nv = jnp.where(blocks < blk_end[-1], jnp.clip(left, 0, tb), 0).astype(jnp.int32)
    tail = blk_end[-1] + jnp.arange(N_EXPERTS, dtype=jnp.int32)
    zero_blocks = jnp.concatenate([jnp.where(nblk > 0, blk_end - 1, -1),
                                   jnp.where(tail < n_blocks, tail, -1)]).astype(jnp.int32)
    return dest.astype(jnp.int32), block_e, block_nv, zero_blocks


def kernel(x, c, positions, w_ada, b_ada, norm_mix, w_in, w_ret_o, w_diff_o, lam_q1, lam_k1, lam_q2, lam_k2,
           diff_norm, w_out, norm_ffn, w_router_group, b_router_group, w_router_expert, b_router_expert,
           w_exp_gate, w_exp_up, w_exp_down, norm_final):
    B, S, D = x.shape
    T = B * S
    depth = w_ada.shape[0]
    assert D_IN == w_in.shape[2] and S % RET_CHUNK == 0
    xf = x.reshape(T, D)
    tm_in = min(TM_INPROJ, S)
    pos = positions.reshape(T // tm_in, 1, tm_in)
    n_blocks = (2 * T) // TB_MOE + N_EXPERTS
    for l in range(depth):
        lambda_init = 0.8 - 0.6 * math.exp(-0.3 * l)
        mod = _adaln(c, w_ada[l], b_ada[l]).reshape(B, 6, D)
        proj = _inproj(xf, pos, mod, norm_mix[l], w_in[l].astype(BF16), S)
        ro = _retention(proj, B, S)
        lam4 = jnp.stack([lam_q1[l], lam_k1[l], lam_q2[l], lam_k2[l]]).astype(F32)
        do = _diff_attention(proj, lam4, diff_norm[l].astype(F32), lambda_init, B, S)

        pad = LANES - N_EXPERTS - N_GROUPS
        w_rt = jnp.concatenate([w_router_expert[l], w_router_group[l], jnp.zeros((D, pad), F32)], axis=1)
        b_rt = jnp.concatenate([b_router_expert[l], b_router_group[l], jnp.zeros((pad,), F32)]).reshape(1, LANES)
        x1, h2, route, rt, counts = _post_mixer(
            ro, do, proj, xf, mod, w_ret_o[l].astype(BF16), w_diff_o[l].astype(BF16), w_out[l].astype(BF16),
            norm_ffn[l], w_rt.astype(BF16), b_rt, S)

        dest, block_e, block_nv, zero_blocks = _routing_tables(rt, counts, n_blocks)
        h_pad = _moe_scatter(h2, dest, zero_blocks, n_blocks * TB_MOE)
        y_pad = _moe_experts(h_pad, block_e, block_nv, w_exp_gate[l], w_exp_up[l], w_exp_down[l])
        xf = _moe_combine(y_pad, dest, x1, route, mod, norm_final, S, l == depth - 1)
    return xf.reshape(B, S, D)
```

```python
import functools
import math

import jax
import jax.numpy as jnp
from jax import lax
from jax.experimental import pallas as pl
from jax.experimental.pallas import tpu as pltpu

F32 = jnp.float32
BF16 = jnp.bfloat16

EPS = 1e-6
ROPE_THETA = 10000.0
LANES = 128
RET_HEADS, RET_DK, RET_DV, RET_CHUNK = 4, 128, 256, 128
DIFF_HEADS, DIFF_DK, DIFF_DV = 4, 128, 256
N_GROUPS, EXPERTS_PER_GROUP, N_EXPERTS, D_EXPERT = 4, 8, 32, 512
OFF_RQ, OFF_RK, OFF_RV, OFF_RG, OFF_DQ, OFF_DK, OFF_DV, OFF_GR, OFF_GD, D_IN = (
    0, 512, 1024, 2048, 3072, 4096, 5120, 6144, 7168, 8192)
NEG_BIG = -1e30
ROUTE_FIELDS = 8
VMEM_LIMIT_BYTES = 48 * 1024 * 1024
VMEM_LIMIT_INPROJ_BYTES = 56 * 1024 * 1024
VMEM_LIMIT_ATTN_BYTES = 56 * 1024 * 1024

TM_INPROJ, TN_INPROJ = 512, 1024
TR_RETENTION = 512
BQ_ATTN = 512
HEADS_PER_ATTN_STEP = 2
TM_POST = 512
TB_MOE = 512
TS_SCATTER = 512
TC_COMBINE = 512


def _params(*sem):
    return pltpu.CompilerParams(dimension_semantics=sem, vmem_limit_bytes=VMEM_LIMIT_BYTES)


def _sigmoid(v):
    return 1.0 / (1.0 + jnp.exp(-v))


def _rms(v):
    return v * lax.rsqrt(jnp.mean(v * v, axis=-1, keepdims=True) + EPS)


def _dot(a, b):
    return jnp.dot(a, b, preferred_element_type=F32)


def _dot_nt(a, b):
    return lax.dot_general(a, b, (((1,), (1,)), ((), ())), preferred_element_type=F32)


def _pack_bf16_pairs(x):
    m = x.shape[1] // 2

    def rounded_bits(v):
        return lax.bitcast_convert_type(v.astype(BF16).astype(F32), jnp.uint32)

    return (rounded_bits(x[:, :m]) >> 16) | rounded_bits(x[:, m:])


def _unpack_bf16_pairs(w):
    lo = lax.bitcast_convert_type(w << 16, F32)
    hi = lax.bitcast_convert_type(w & jnp.uint32(0xFFFF0000), F32)
    return lo.astype(BF16), hi.astype(BF16)


def _ada_kernel(c_ref, w_ref, b_ref, o_ref):
    c = c_ref[...]
    a = (c * _sigmoid(c)).astype(BF16)
    o_ref[...] = _dot(a, w_ref[...].astype(BF16)) + b_ref[...]


def _adaln(c, w, b):
    B, D = c.shape
    n = w.shape[1] // D
    return pl.pallas_call(
        _ada_kernel,
        grid=(n,),
        in_specs=[pl.BlockSpec((B, D), lambda j: (0, 0)),
                  pl.BlockSpec((D, D), lambda j: (0, j)),
                  pl.BlockSpec((1, D), lambda j: (0, j))],
        out_specs=pl.BlockSpec((B, D), lambda j: (0, j)),
        out_shape=jax.ShapeDtypeStruct((B, n * D), F32),
        compiler_params=_params("parallel"),
        name="adaln",
    )(c, w, b.reshape(1, -1))


ROPE_GROUPS = ((OFF_RQ, OFF_RK, 1.0), (OFF_RK, OFF_RV, RET_DK ** -0.5),
               (OFF_DQ, OFF_DK, DIFF_DK ** -0.5 * math.log2(math.e)), (OFF_DK, OFF_DV, 1.0))


def _rope_scale(col):
    for lo, hi, scale in ROPE_GROUPS:
        if lo <= col < hi:
            return scale
    return None


def _inproj_kernel(x0_ref, pos0_ref, mod0_ref, xn_ref, posn_ref, modn_ref, nw_ref, inv_ref,
                   w_ref, o_ref, h_ref, cos_ref, sin_ref, *, tn):
    i = pl.program_id(0)

    def prepare(x, pos, mod, slot):
        h = _rms(x) * nw_ref[...] * (1.0 + mod[1:2]) + mod[0:1]
        h_ref[slot] = h.astype(BF16)
        pos_t = jnp.broadcast_to(pos.astype(F32), (LANES, pos.shape[1])).T
        half = pos_t.shape[0] // 2
        low = lax.broadcasted_iota(jnp.int32, (half, LANES), 1) < LANES // 2
        ang = jnp.where(low, pos_t[:half], pos_t[half:]) * inv_ref[...]
        cos = jnp.cos(ang)
        sin = jnp.sin(ang)
        cos_r = pltpu.roll(cos, LANES // 2, 1)
        sin_r = pltpu.roll(sin, LANES // 2, 1)
        cos_ref[slot, :half] = jnp.where(low, cos, cos_r)
        cos_ref[slot, half:] = jnp.where(low, cos_r, cos)
        sin_ref[slot, :half] = jnp.where(low, -sin, sin_r)
        sin_ref[slot, half:] = jnp.where(low, -sin_r, sin)

    @pl.when(i == 0)
    def _():
        prepare(x0_ref[...], pos0_ref[0], mod0_ref[0], 0)

    cur = i % 2
    for g in range(o_ref.shape[1] // tn):
        acc = _dot(h_ref[cur], w_ref[:, g * tn:(g + 1) * tn])
        for k in range(tn // LANES):
            first = g * tn + k * LANES
            a = acc[:, k * LANES:(k + 1) * LANES]
            scale = _rope_scale(first)
            if scale is not None:
                a = a * cos_ref[cur] + pltpu.roll(a, LANES // 2, 1) * sin_ref[cur]
                if scale != 1.0:
                    a = a * scale
            o_ref[:, first:first + LANES] = a.astype(BF16)

    prepare(xn_ref[...], posn_ref[0], modn_ref[0], 1 - cur)


def _inproj(xf, pos, mod, norm_w, w_bf16, seq):
    T, D = xf.shape
    tm = min(TM_INPROJ, seq)
    tn = TN_INPROJ
    half = LANES // 2
    inv = ROPE_THETA ** (-jnp.arange(0, LANES, 2, dtype=F32) / LANES)
    inv = jnp.concatenate([inv, inv]).reshape(1, LANES)
    per_batch = seq // tm
    n_m = T // tm
    nxt = lambda i: jnp.minimum(i + 1, n_m - 1)
    once = dict(pipeline_mode=pl.Buffered(1))
    return pl.pallas_call(
        functools.partial(_inproj_kernel, tn=tn),
        grid=(n_m,),
        in_specs=[pl.BlockSpec((tm, D), lambda i: (0, 0), **once),
                  pl.BlockSpec((1, 1, tm), lambda i: (0, 0, 0), **once),
                  pl.BlockSpec((1, 6, D), lambda i: (0, 0, 0), **once),
                  pl.BlockSpec((tm, D), lambda i: (nxt(i), 0)),
                  pl.BlockSpec((1, 1, tm), lambda i: (nxt(i), 0, 0)),
                  pl.BlockSpec((1, 6, D), lambda i: (nxt(i) // per_batch, 0, 0)),
                  pl.BlockSpec((1, D), lambda i: (0, 0), **once),
                  pl.BlockSpec((1, LANES), lambda i: (0, 0), **once),
                  pl.BlockSpec((D, D_IN), lambda i: (0, 0), **once)],
        out_specs=pl.BlockSpec((tm, D_IN), lambda i: (i, 0)),
        out_shape=jax.ShapeDtypeStruct((T, D_IN), BF16),
        scratch_shapes=[pltpu.VMEM((2, tm, D), BF16),
                        pltpu.VMEM((2, tm, LANES), F32),
                        pltpu.VMEM((2, tm, LANES), F32)],
        compiler_params=pltpu.CompilerParams(dimension_semantics=("arbitrary",),
                                             vmem_limit_bytes=VMEM_LIMIT_INPROJ_BYTES),
        name="inproj",
    )(xf, pos, mod, xf, pos, mod, norm_w.reshape(1, D), inv, w_bf16)


def _ret_kernel(q_ref, k_ref, v_ref, g_ref, dmask_ref, xi_ref, zeta_ref, cd_ref, o_ref, state_ref):
    @pl.when(pl.program_id(1) == 0)
    def _():
        state_ref[...] = jnp.zeros(state_ref.shape, F32)

    C = RET_CHUNK
    for h in range(RET_HEADS):
        qk = slice(h * RET_DK, (h + 1) * RET_DK)
        vv = slice(h * RET_DV, (h + 1) * RET_DV)
        st = state_ref[h]
        for n in range(q_ref.shape[0] // C):
            rows = slice(n * C, (n + 1) * C)
            q = q_ref[rows, qk]
            k = k_ref[rows, qk]
            v = v_ref[rows, vv]
            s = _dot_nt(q, k) * dmask_ref[h]
            qx = (q.astype(F32) * xi_ref[h]).astype(BF16)
            o = _dot(s.astype(BF16), v) + _dot(qx, st.astype(BF16))
            kz_t = (k.astype(F32) * zeta_ref[h]).T.astype(BF16)
            st = cd_ref[h] * st + _dot(kz_t, v)
            g = g_ref[rows, vv].astype(F32)
            o_ref[rows, vv] = (_rms(o) * (g * _sigmoid(g))).astype(BF16)
        state_ref[h] = st


def _retention(proj, batch, seq):
    T = proj.shape[0]
    C = RET_CHUNK
    H = RET_HEADS
    nc = seq // C
    gamma = 1.0 - jnp.exp2(-5.0 - jnp.arange(H, dtype=F32))
    log_g = jnp.log(gamma)
    idx = jnp.arange(C, dtype=F32)
    rel = idx[:, None] - idx[None, :]
    dmask = jnp.where(rel >= 0, jnp.exp(log_g[:, None, None] * jnp.maximum(rel, 0.0)), 0.0)
    zeta = jnp.exp(log_g[:, None] * (C - 1 - idx))
    xi = jnp.exp(log_g[:, None] * (idx + 1))
    cd = jnp.exp(log_g * C)
    zeta_b = jnp.broadcast_to(zeta[:, :, None], (H, C, RET_DK))
    xi_b = jnp.broadcast_to(xi[:, :, None], (H, C, RET_DK))
    cd_b = jnp.broadcast_to(cd[:, None, None], (H, 1, RET_DV))
    wq = H * RET_DK
    wv = H * RET_DV
    tr = min(TR_RETENTION, seq)
    ns = seq // tr
    row = lambda b, n: b * ns + n
    const3 = lambda b, n: (0, 0, 0)
    return pl.pallas_call(
        _ret_kernel,
        grid=(batch, ns),
        in_specs=[pl.BlockSpec((tr, wq), lambda b, n: (row(b, n), OFF_RQ // wq)),
                  pl.BlockSpec((tr, wq), lambda b, n: (row(b, n), OFF_RK // wq)),
                  pl.BlockSpec((tr, wv), lambda b, n: (row(b, n), OFF_RV // wv)),
                  pl.BlockSpec((tr, wv), lambda b, n: (row(b, n), OFF_RG // wv)),
                  pl.BlockSpec((H, C, C), const3),
                  pl.BlockSpec((H, C, RET_DK), const3),
                  pl.BlockSpec((H, C, RET_DK), const3),
                  pl.BlockSpec((H, 1, RET_DV), const3)],
        out_specs=pl.BlockSpec((tr, wv), lambda b, n: (row(b, n), 0)),
        out_shape=jax.ShapeDtypeStruct((T, wv), BF16),
        scratch_shapes=[pltpu.VMEM((H, RET_DK, RET_DV), F32)],
        compiler_params=_params("parallel", "arbitrary"),
        name="retention",
    )(proj, proj, proj, proj, dmask, xi_b, zeta_b, cd_b)


def _transpose_bf16(a):
    return a.astype(F32).T.astype(BF16)


def _attn_kernel(lam_ref, dn_ref, *refs, bq, bk, heads, lambda_init):
    in_refs, (o_ref, vt_ref, qt_ref, s_ref, pm_ref, m_ref, l_ref, acc_ref) = refs[:5 * heads], refs[5 * heads:]
    q_refs = [in_refs[5 * h + c] for h in range(heads) for c in range(2)]
    k_refs = [in_refs[5 * h + 2 + c] for h in range(heads) for c in range(2)]
    v_refs = [in_refs[5 * h + 4] for h in range(heads)]
    streams = range(2 * heads)
    n_sub = vt_ref.shape[1]
    dv = acc_ref.shape[2]

    for h in range(heads):
        for t in range(n_sub):
            vt_ref[h, t] = _transpose_bf16(v_refs[h][t * bk:(t + 1) * bk, :])

    lam4 = lam_ref[...]
    lam = (jnp.exp(jnp.sum(lam4[0:1] * lam4[1:2], axis=1, keepdims=True))
           - jnp.exp(jnp.sum(lam4[2:3] * lam4[3:4], axis=1, keepdims=True)) + lambda_init)

    for i in range(n_sub // 2):
        par = i % 2
        rows = slice(i * bq, (i + 1) * bq)
        m_ref[par] = jnp.full(m_ref.shape[1:], NEG_BIG, F32)
        l_ref[par] = jnp.zeros(l_ref.shape[1:], F32)
        acc_ref[par] = jnp.zeros(acc_ref.shape[1:], F32)
        for n in streams:
            qt_ref[par, n] = _transpose_bf16(q_refs[n][rows, :])

        def score(t, slot, diagonal_offset=None, q_from=0, par=par):
            start = pl.multiple_of(t * bk, bk)
            for n in streams:
                st = _dot(k_refs[n][pl.ds(start, bk), :], qt_ref[par, n, :, q_from:])
                if diagonal_offset is not None:
                    key = lax.broadcasted_iota(jnp.int32, st.shape, 0) + diagonal_offset
                    qry = lax.broadcasted_iota(jnp.int32, st.shape, 1) + q_from
                    st = jnp.where(key <= qry, st, NEG_BIG)
                s_ref[slot, n, :, q_from:] = st
                pm_ref[slot, n, :, q_from:] = jnp.max(st, axis=0, keepdims=True)

        def accumulate(t, slot, q_from=0, par=par):
            for n in streams:
                m_prev = m_ref[par, n, :, q_from:]
                m_new = jnp.maximum(m_prev, pm_ref[slot, n, :, q_from:])
                p = jnp.exp2(s_ref[slot, n, :, q_from:] - m_new)
                alpha = jnp.exp2(m_prev - m_new)
                l_ref[par, n, :, q_from:] = alpha * l_ref[par, n, :, q_from:] + jnp.sum(p, axis=0, keepdims=True)
                acc_ref[par, n, :, q_from:] = (alpha * acc_ref[par, n, :, q_from:]
                                               + _dot(vt_ref[n // 2, t], p.astype(BF16)))
                m_ref[par, n, :, q_from:] = m_new

        def pair(jj, next_is_diagonal, score=score, accumulate=accumulate):
            t = 2 * jj
            score(t + 1, 1)
            accumulate(t, 0)
            score(t + 2, 0, 0 if next_is_diagonal else None)
            accumulate(t + 1, 1)

        score(0, 0, 0 if i == 0 else None)
        if i > 0:
            def body(jj, carry, pair=pair):
                pair(jj, False)
                return carry

            lax.fori_loop(0, i - 1, body, 0)
            pair(i - 1, True)
        score(2 * i + 1, 1, bk, q_from=bk)
        accumulate(2 * i, 0)
        accumulate(2 * i + 1, 1, q_from=bk)

        for h in range(heads):
            a, b = 2 * h, 2 * h + 1
            ot = (acc_ref[par, a] * (1.0 / l_ref[par, a])
                  - lam * (acc_ref[par, b] * (1.0 / l_ref[par, b])))
            ot = ot * lax.rsqrt(jnp.mean(ot * ot, axis=0, keepdims=True) + EPS)
            o_ref[rows, h * dv:(h + 1) * dv] = (ot.T * dn_ref[...] * (1.0 - lambda_init)).astype(BF16)


def _diff_attention(proj, lam4, diff_norm, lambda_init, batch, seq):
    T = proj.shape[0]
    H = DIFF_HEADS
    hp = HEADS_PER_ATTN_STEP
    bq = min(BQ_ATTN, seq)
    bk = bq // 2
    nq = seq // bq
    q0 = OFF_DQ // DIFF_DK
    k0 = OFF_DK // DIFF_DK
    v0 = OFF_DV // DIFF_DV
    head_specs = []
    for j in range(hp):
        head = lambda g, j=j: hp * g + j
        head_specs += [
            pl.BlockSpec((seq, DIFF_DK), lambda b, g, head=head: (b, q0 + 2 * head(g))),
            pl.BlockSpec((seq, DIFF_DK), lambda b, g, head=head: (b, q0 + 2 * head(g) + 1)),
            pl.BlockSpec((seq, DIFF_DK), lambda b, g, head=head: (b, k0 + 2 * head(g))),
            pl.BlockSpec((seq, DIFF_DK), lambda b, g, head=head: (b, k0 + 2 * head(g) + 1)),
            pl.BlockSpec((seq, DIFF_DV), lambda b, g, head=head: (b, v0 + head(g)))]
    ns = 2 * hp
    return pl.pallas_call(
        functools.partial(_attn_kernel, bq=bq, bk=bk, heads=hp, lambda_init=lambda_init),
        grid=(batch, H // hp),
        in_specs=[pl.BlockSpec((4, DIFF_DK), lambda b, g: (0, 0)),
                  pl.BlockSpec((1, DIFF_DV), lambda b, g: (0, 0))] + head_specs,
        out_specs=pl.BlockSpec((seq, hp * DIFF_DV), lambda b, g: (b, g)),
        out_shape=jax.ShapeDtypeStruct((T, H * DIFF_DV), BF16),
        scratch_shapes=[pltpu.VMEM((hp, seq // bk, DIFF_DV, bk), BF16),
                        pltpu.VMEM((2, ns, DIFF_DK, bq), BF16),
                        pltpu.VMEM((2, ns, bk, bq), F32),
                        pltpu.VMEM((2, ns, 1, bq), F32),
                        pltpu.VMEM((2, ns, 1, bq), F32),
                        pltpu.VMEM((2, ns, 1, bq), F32),
                        pltpu.VMEM((2, ns, DIFF_DV, bq), F32)],
        compiler_params=pltpu.CompilerParams(dimension_semantics=("parallel", "parallel"),
                                             vmem_limit_bytes=VMEM_LIMIT_ATTN_BYTES),
        name="diff_attn",
    )(lam4, diff_norm.reshape(1, DIFF_DV), *([proj] * (5 * hp)))


def _post_kernel(ro_ref, do_ref, gr_ref, gd_ref, x_ref, mod_ref, modp_ref, wr_ref, wd_ref, wo_ref, nf_ref,
                 wrt_ref, brt_ref, x1_ref, h2_ref, route_ref, rt_ref, cnt_ref, carry_ref, x1s_ref, lower_ref,
                 *, tm):
    i = pl.program_id(0)

    @pl.when(i == 0)
    def _():
        carry_ref[...] = jnp.zeros(carry_ref.shape, F32)
        x1s_ref[...] = jnp.zeros(x1s_ref.shape, F32)
        r_i = lax.broadcasted_iota(jnp.int32, (tm, tm), 0)
        c_i = lax.broadcasted_iota(jnp.int32, (tm, tm), 1)
        lower_ref[...] = jnp.where(c_i < r_i, 1.0, 0.0).astype(BF16)

    modp = modp_ref[0]
    h2 = _rms(x1s_ref[...]) * nf_ref[...] * (1.0 + modp[4:5]) + modp[3:4]
    h2_ref[...] = _pack_bf16_pairs(h2)
    lg = _dot(h2.astype(BF16), wrt_ref[...]) + brt_ref[...]

    mod = mod_ref[0]
    ret_out = _dot(ro_ref[...], wr_ref[...])
    diff_out = _dot(do_ref[...], wd_ref[...])
    merged = (_sigmoid(gr_ref[...].astype(F32)) * ret_out
              + _sigmoid(gd_ref[...].astype(F32)) * diff_out)
    x1 = x_ref[...] + mod[2:3] * _dot(merged.astype(BF16), wo_ref[...])
    x1_ref[...] = x1

    lane = lax.broadcasted_iota(jnp.int32, lg.shape, 1)
    far = jnp.int32(LANES)

    def top1(vals):
        best = jnp.max(vals, axis=1, keepdims=True)
        return best, jnp.min(jnp.where(vals == best, lane, far), axis=1, keepdims=True)

    is_group = (lane >= N_EXPERTS) & (lane < N_EXPERTS + N_GROUPS)
    g_best, g_lane = top1(jnp.where(is_group, lg, NEG_BIG))
    g_sum = jnp.sum(jnp.where(is_group, jnp.exp(lg - g_best), 0.0), axis=1, keepdims=True)
    g_top = 1.0 / g_sum
    first = (g_lane - N_EXPERTS) * EXPERTS_PER_GROUP
    in_group = (lane >= first) & (lane < first + EXPERTS_PER_GROUP)
    el = jnp.where(in_group, lg, NEG_BIG)
    e_a, i_a = top1(el)
    e_b, i_b = top1(jnp.where(lane == i_a, NEG_BIG, el))
    t = jnp.exp(e_b - e_a)
    w_a = g_top / (1.0 + t)
    w_b = g_top * t / (1.0 + t)

    hot_a = lane == i_a
    hot_b = lane == i_b
    hot = jnp.where((hot_a | hot_b) & (i > 0), 1.0, 0.0)
    before = _dot(lower_ref[...], hot.astype(BF16)) + carry_ref[...]
    rank_a = jnp.sum(jnp.where(hot_a, before, 0.0), axis=1, keepdims=True)
    rank_b = jnp.sum(jnp.where(hot_b, before, 0.0), axis=1, keepdims=True)
    total = carry_ref[...] + jnp.sum(hot, axis=0, keepdims=True)
    carry_ref[...] = total
    cnt_ref[...] = total

    fields = (i_a.astype(F32), i_b.astype(F32), w_a, w_b, rank_a, rank_b)
    route = jnp.zeros(lg.shape, F32)
    for n, f in enumerate(fields):
        route = jnp.where(lane == n, f, route)
    route_ref[...] = route
    rt_ref[...] = route.T[0:ROUTE_FIELDS, :]
    x1s_ref[...] = x1


def _post_mixer(ro, do, proj, xf, mod, w_ret_o, w_diff_o, w_out, norm_ffn, w_rt, b_rt, seq):
    T, D = xf.shape
    tm = min(TM_POST, seq)
    per_batch = seq // tm
    n = T // tm
    cur = lambda i: jnp.minimum(i, n - 1)
    prev = lambda i: jnp.maximum(i - 1, 0)
    row = lambda i: (cur(i), 0)
    prow = lambda i: (prev(i), 0)
    const = lambda i: (0, 0)
    return pl.pallas_call(
        functools.partial(_post_kernel, tm=tm),
        grid=(n + 1,),
        in_specs=[pl.BlockSpec((tm, D), row),
                  pl.BlockSpec((tm, D), row),
                  pl.BlockSpec((tm, D), lambda i: (cur(i), OFF_GR // D)),
                  pl.BlockSpec((tm, D), lambda i: (cur(i), OFF_GD // D)),
                  pl.BlockSpec((tm, D), row),
                  pl.BlockSpec((1, 6, D), lambda i: (cur(i) // per_batch, 0, 0)),
                  pl.BlockSpec((1, 6, D), lambda i: (prev(i) // per_batch, 0, 0)),
                  pl.BlockSpec((D, D), const),
                  pl.BlockSpec((D, D), const),
                  pl.BlockSpec((D, D), const),
                  pl.BlockSpec((1, D), const),
                  pl.BlockSpec((D, LANES), const),
                  pl.BlockSpec((1, LANES), const)],
        out_specs=[pl.BlockSpec((tm, D), row),
                   pl.BlockSpec((tm, D // 2), prow),
                   pl.BlockSpec((tm, LANES), prow),
                   pl.BlockSpec((ROUTE_FIELDS, tm), lambda i: (0, prev(i))),
                   pl.BlockSpec((1, LANES), const)],
        out_shape=[jax.ShapeDtypeStruct((T, D), F32),
                   jax.ShapeDtypeStruct((T, D // 2), jnp.uint32),
                   jax.ShapeDtypeStruct((T, LANES), F32),
                   jax.ShapeDtypeStruct((ROUTE_FIELDS, T), F32),
                   jax.ShapeDtypeStruct((1, LANES), F32)],
        scratch_shapes=[pltpu.VMEM((1, LANES), F32), pltpu.VMEM((tm, D), F32), pltpu.VMEM((tm, tm), BF16)],
        compiler_params=_params("arbitrary"),
        name="post_mixer",
    )(ro, do, proj, proj, xf, mod, mod, w_ret_o, w_diff_o, w_out, norm_ffn.reshape(1, D), w_rt, b_rt)


def _row_copy(src, s, dst, d, sem):
    return pltpu.make_async_copy(src.at[pl.ds(s, 1), :], dst.at[pl.ds(d, 1), :], sem)


def _dest_blocks(dest, tile):
    n = dest.shape[1] // tile
    return dest.reshape(2, n, tile).transpose(1, 0, 2).reshape(n, 1, 2 * tile)


def _scatter_kernel(zb_ref, dest_ref, h_ref, o_hbm, zero_ref, zsem, sem, *, ts, tb):
    i = pl.program_id(0)

    @pl.when(i == 0)
    def _():
        zero_ref[...] = jnp.zeros(zero_ref.shape, zero_ref.dtype)

        def zero_copy(n):
            start = pl.multiple_of(zb_ref[n] * tb, tb)
            return pltpu.make_async_copy(zero_ref, o_hbm.at[pl.ds(start, tb), :], zsem)

        for n in range(zb_ref.shape[0]):
            pl.when(zb_ref[n] >= 0)(lambda n=n: zero_copy(n).start())
        for n in range(zb_ref.shape[0]):
            pl.when(zb_ref[n] >= 0)(lambda n=n: zero_copy(n).wait())

    def issue(r, carry):
        for k in range(2):
            _row_copy(h_ref, r, o_hbm, dest_ref[0, 0, k * ts + r], sem).start(priority=k)
        return carry

    lax.fori_loop(0, ts, issue, 0, unroll=True)
    pltpu.make_async_copy(o_hbm.at[pl.ds(0, 2 * ts), :], o_hbm.at[pl.ds(0, 2 * ts), :], sem).wait()


def _moe_scatter(h2, dest, zero_blocks, n_rows):
    T, D = h2.shape
    ts = min(TS_SCATTER, T)
    tb = TB_MOE
    grid_spec = pltpu.PrefetchScalarGridSpec(
        num_scalar_prefetch=1,
        grid=(T // ts,),
        in_specs=[pl.BlockSpec((1, 1, 2 * ts), lambda i, zb: (i, 0, 0), memory_space=pltpu.SMEM),
                  pl.BlockSpec((ts, D), lambda i, zb: (i, 0))],
        out_specs=pl.BlockSpec(memory_space=pl.ANY),
        scratch_shapes=[pltpu.VMEM((tb, D), h2.dtype), pltpu.SemaphoreType.DMA(()), pltpu.SemaphoreType.DMA(())],
    )
    return pl.pallas_call(
        functools.partial(_scatter_kernel, ts=ts, tb=tb),
        grid_spec=grid_spec,
        out_shape=jax.ShapeDtypeStruct((n_rows, D), h2.dtype),
        compiler_params=_params("arbitrary"),
        name="moe_scatter",
    )(zero_blocks, _dest_blocks(dest, ts), h2)


def _expert_kernel(be_ref, nv_ref, x_ref, w1_ref, w3_ref, w2_ref, o_ref, w1b_ref, w3b_ref, w2b_ref):
    n = pl.program_id(0)
    nv = nv_ref[n]
    new_expert = (n == 0) | (be_ref[n] != be_ref[jnp.maximum(n - 1, 0)])

    @pl.when(new_expert & (nv > 0))
    def _():
        w1b_ref[...] = w1_ref[0].astype(BF16)
        w3b_ref[...] = w3_ref[0].astype(BF16)
        w2b_ref[...] = w2_ref[0].astype(BF16)

    @pl.when(nv > 0)
    def _():
        x_lo, x_hi = _unpack_bf16_pairs(x_ref[...])
        half = x_lo.shape[1]
        g = _dot(x_lo, w1b_ref[:half, :]) + _dot(x_hi, w1b_ref[half:, :])
        u = _dot(x_lo, w3b_ref[:half, :]) + _dot(x_hi, w3b_ref[half:, :])
        a = (g * _sigmoid(g) * u).astype(BF16)
        o_ref[...] = _pack_bf16_pairs(_dot(a, w2b_ref[...]))

    @pl.when(nv == 0)
    def _():
        o_ref[...] = jnp.zeros(o_ref.shape, o_ref.dtype)


def _moe_experts(h_pad, block_e, block_nv, w1, w3, w2):
    P = h_pad.shape[0]
    D = 2 * h_pad.shape[1]
    tb = TB_MOE
    wmap = lambda n, be, nv: (be[n], 0, 0)
    grid_spec = pltpu.PrefetchScalarGridSpec(
        num_scalar_prefetch=2,
        grid=(P // tb,),
        in_specs=[pl.BlockSpec((tb, D // 2), lambda n, be, nv: (n, 0)),
                  pl.BlockSpec((1, D, D_EXPERT), wmap),
                  pl.BlockSpec((1, D, D_EXPERT), wmap),
                  pl.BlockSpec((1, D_EXPERT, D), wmap)],
        out_specs=pl.BlockSpec((tb, D // 2), lambda n, be, nv: (n, 0)),
        scratch_shapes=[pltpu.VMEM((D, D_EXPERT), BF16), pltpu.VMEM((D, D_EXPERT), BF16),
                        pltpu.VMEM((D_EXPERT, D), BF16)],
    )
    return pl.pallas_call(
        _expert_kernel,
        grid_spec=grid_spec,
        out_shape=jax.ShapeDtypeStruct((P, D // 2), h_pad.dtype),
        compiler_params=_params("arbitrary"),
        name="moe_experts",
    )(block_e, block_nv, h_pad, w1, w3, w2)


def _combine_kernel(dcur_ref, dnext_ref, x1_ref, route_ref, mod_ref, nw_ref, y_hbm, o_ref, buf, sem, *, tc, final):
    i = pl.program_id(0)
    slot = i % 2

    def gather(d_ref, s):
        def issue(r, carry):
            for k in range(2):
                _row_copy(y_hbm, d_ref[0, 0, k * tc + r], buf.at[s], k * tc + r, sem.at[s]).start(priority=k)
            return carry

        lax.fori_loop(0, tc, issue, 0, unroll=True)

    def wait_slot(s):
        pltpu.make_async_copy(y_hbm.at[pl.ds(0, 2 * tc), :], buf.at[s], sem.at[s]).wait()

    pl.when(i == 0)(lambda: gather(dcur_ref, 0))
    wait_slot(slot)
    gather(dnext_ref, 1 - slot)
    route = route_ref[...]

    def rows(first):
        lo, hi = _unpack_bf16_pairs(buf[slot, first:first + tc, :])
        return jnp.concatenate([lo, hi], axis=1).astype(F32)

    y = route[:, 2:3] * rows(0) + route[:, 3:4] * rows(tc)
    x2 = x1_ref[...] + mod_ref[0][5:6] * y
    o_ref[...] = _rms(x2) * nw_ref[...] if final else x2
    pl.when(i == pl.num_programs(0) - 1)(lambda: wait_slot(1 - slot))


def _moe_combine(y_pad, dest, x1, route, mod, norm_w, seq, final):
    T, D = x1.shape
    tc = min(TC_COMBINE, seq)
    per_batch = seq // tc
    row = lambda i: (i, 0)
    n = T // tc
    dest3 = _dest_blocks(dest, tc)
    return pl.pallas_call(
        functools.partial(_combine_kernel, tc=tc, final=final),
        grid=(n,),
        in_specs=[pl.BlockSpec((1, 1, 2 * tc), lambda i: (i, 0, 0), memory_space=pltpu.SMEM),
                  pl.BlockSpec((1, 1, 2 * tc), lambda i: (jnp.minimum(i + 1, n - 1), 0, 0),
                               memory_space=pltpu.SMEM),
                  pl.BlockSpec((tc, D), row),
                  pl.BlockSpec((tc, LANES), row),
                  pl.BlockSpec((1, 6, D), lambda i: (i // per_batch, 0, 0)),
                  pl.BlockSpec((1, D), lambda i: (0, 0)),
                  pl.BlockSpec(memory_space=pl.ANY)],
        out_specs=pl.BlockSpec((tc, D), row),
        out_shape=jax.ShapeDtypeStruct((T, D), F32),
        scratch_shapes=[pltpu.VMEM((2, 2 * tc, D // 2), y_pad.dtype), pltpu.SemaphoreType.DMA((2,))],
        compiler_params=_params("arbitrary"),
        name="moe_combine",
    )(dest3, dest3, x1, route, mod, norm_w.reshape(1, D), y_pad)


def _dest_kernel(start_ref, rt_ref, o_ref):
    expert = rt_ref[0:2, :].astype(jnp.int32)
    rank = rt_ref[4:6, :].astype(jnp.int32)
    base = jnp.zeros(expert.shape, jnp.int32)
    for e in range(N_EXPERTS):
        base = jnp.where(expert == e, start_ref[e], base)
    o_ref[...] = base + rank


def _dest_slots(rt, region_start):
    T = rt.shape[1]
    grid_spec = pltpu.PrefetchScalarGridSpec(
        num_scalar_prefetch=1,
        grid=(1,),
        in_specs=[pl.BlockSpec((ROUTE_FIELDS, T), lambda i, s: (0, 0))],
        out_specs=pl.BlockSpec((2, T), lambda i, s: (0, 0)),
    )
    return pl.pallas_call(
        _dest_kernel,
        grid_spec=grid_spec,
        out_shape=jax.ShapeDtypeStruct((2, T), jnp.int32),
        compiler_params=_params("arbitrary"),
        name="moe_dest",
    )(region_start, rt)


def _routing_tables(rt, counts, n_blocks):
    tb = TB_MOE
    cnt = counts[0, :N_EXPERTS].astype(jnp.int32)
    nblk = (cnt + tb - 1) // tb
    blk_end = jnp.cumsum(nblk)
    blk_start = blk_end - nblk
    dest = _dest_slots(rt, (blk_start * tb).astype(jnp.int32))
    blocks = jnp.arange(n_blocks, dtype=jnp.int32)
    block_e = jnp.minimum(jnp.sum(blocks[:, None] >= blk_end[None, :], axis=1), N_EXPERTS - 1).astype(jnp.int32)
    mine = block_e[:, None] == jnp.arange(N_EXPERTS, dtype=jnp.int32)[None, :]
    left = jnp.sum(jnp.where(mine, cnt[None, :] - (blocks[:, None] - blk_start[None, :]) * tb, 0), axis=1)
    block_nv = jnp.where(blocks < blk_end[-1], jnp.clip(left, 0, tb), 0).astype(jnp.int32)
    tail = blk_end[-1] + jnp.arange(N_EXPERTS, dtype=jnp.int32)
    zero_blocks = jnp.concatenate([jnp.where(nblk > 0, blk_end - 1, -1),
                                   jnp.where(tail < n_blocks, tail, -1)]).astype(jnp.int32)
    return dest.astype(jnp.int32), block_e, block_nv, zero_blocks


def kernel(x, c, positions, w_ada, b_ada, norm_mix, w_in, w_ret_o, w_diff_o, lam_q1, lam_k1, lam_q2, lam_k2,
           diff_norm, w_out, norm_ffn, w_router_group, b_router_group, w_router_expert, b_router_expert,
           w_exp_gate, w_exp_up, w_exp_down, norm_final):
    B, S, D = x.shape
    T = B * S
    depth = w_ada.shape[0]
    assert D_IN == w_in.shape[2] and S % RET_CHUNK == 0
    xf = x.reshape(T, D)
    tm_in = min(TM_INPROJ, S)
    pos = positions.reshape(T // tm_in, 1, tm_in)
    n_blocks = (2 * T) // TB_MOE + N_EXPERTS
    for l in range(depth):
        lambda_init = 0.8 - 0.6 * math.exp(-0.3 * l)
        mod = _adaln(c, w_ada[l], b_ada[l]).reshape(B, 6, D)
        proj = _inproj(xf, pos, mod, norm_mix[l], w_in[l].astype(BF16), S)
        ro = _retention(proj, B, S)
        lam4 = jnp.stack([lam_q1[l], lam_k1[l], lam_q2[l], lam_k2[l]]).astype(F32)
        do = _diff_attention(proj, lam4, diff_norm[l].astype(F32), lambda_init, B, S)

        pad = LANES - N_EXPERTS - N_GROUPS
        w_rt = jnp.concatenate([w_router_expert[l], w_router_group[l], jnp.zeros((D, pad), F32)], axis=1)
        b_rt = jnp.concatenate([b_router_expert[l], b_router_group[l], jnp.zeros((pad,), F32)]).reshape(1, LANES)
        x1, h2, route, rt, counts = _post_mixer(
            ro, do, proj, xf, mod, w_ret_o[l].astype(BF16), w_diff_o[l].astype(BF16), w_out[l].astype(BF16),
            norm_ffn[l], w_rt.astype(BF16), b_rt, S)

        dest, block_e, block_nv, zero_blocks = _routing_tables(rt, counts, n_blocks)
        h_pad = _moe_scatter(h2, dest, zero_blocks, n_blocks * TB_MOE)
        y_pad = _moe_experts(h_pad, block_e, block_nv, w_exp_gate[l], w_exp_up[l], w_exp_down[l])
        xf = _moe_combine(y_pad, dest, x1, route, mod, norm_final, S, l == depth - 1)
    return xf.reshape(B, S, D)
```

```python
import functools
import math

import jax
import jax.numpy as jnp
from jax import lax
from jax.experimental import pallas as pl
from jax.experimental.pallas import tpu as pltpu

F32 = jnp.float32
BF16 = jnp.bfloat16

EPS = 1e-6
ROPE_THETA = 10000.0
LANES = 128
RET_HEADS, RET_DK, RET_DV, RET_CHUNK = 4, 128, 256, 128
DIFF_HEADS, DIFF_DK, DIFF_DV = 4, 128, 256
N_GROUPS, EXPERTS_PER_GROUP, N_EXPERTS, D_EXPERT = 4, 8, 32, 512
OFF_RQ, OFF_RK, OFF_RV, OFF_RG, OFF_DQ, OFF_DK, OFF_DV, OFF_GR, OFF_GD, D_IN = (
    0, 512, 1024, 2048, 3072, 4096, 5120, 6144, 7168, 8192)
NEG_BIG = -1e30
ROUTE_FIELDS = 8
VMEM_LIMIT_BYTES = 48 * 1024 * 1024
VMEM_LIMIT_INPROJ_BYTES = 56 * 1024 * 1024
VMEM_LIMIT_ATTN_BYTES = 56 * 1024 * 1024

TM_INPROJ, TN_INPROJ = 512, 1024
TR_RETENTION = 1024
BQ_ATTN = 512
HEADS_PER_ATTN_STEP = 2
TM_POST = 512
TB_MOE = 512
TS_SCATTER = 512
TC_COMBINE = 512
COMBINE_CHUNKS = 8


def _params(*sem):
    return pltpu.CompilerParams(dimension_semantics=sem, vmem_limit_bytes=VMEM_LIMIT_BYTES)


def _sigmoid(v):
    return 1.0 / (1.0 + jnp.exp(-v))


def _rms(v):
    return v * lax.rsqrt(jnp.mean(v * v, axis=-1, keepdims=True) + EPS)


def _dot(a, b):
    return jnp.dot(a, b, preferred_element_type=F32)


def _dot_nt(a, b):
    return lax.dot_general(a, b, (((1,), (1,)), ((), ())), preferred_element_type=F32)


def _pack_bf16_pairs(x):
    m = x.shape[1] // 2

    def rounded_bits(v):
        return lax.bitcast_convert_type(v.astype(BF16).astype(F32), jnp.uint32)

    return (rounded_bits(x[:, :m]) >> 16) | rounded_bits(x[:, m:])


def _unpack_bf16_pairs(w):
    lo = lax.bitcast_convert_type(w << 16, F32)
    hi = lax.bitcast_convert_type(w & jnp.uint32(0xFFFF0000), F32)
    return lo.astype(BF16), hi.astype(BF16)


def _ada_kernel(c_ref, w_ref, b_ref, o_ref):
    c = c_ref[...]
    a = (c * _sigmoid(c)).astype(BF16)
    o_ref[...] = _dot(a, w_ref[...].astype(BF16)) + b_ref[...]


def _adaln(c, w, b):
    B, D = c.shape
    n = w.shape[1] // D
    return pl.pallas_call(
        _ada_kernel,
        grid=(n,),
        in_specs=[pl.BlockSpec((B, D), lambda j: (0, 0)),
                  pl.BlockSpec((D, D), lambda j: (0, j)),
                  pl.BlockSpec((1, D), lambda j: (0, j))],
        out_specs=pl.BlockSpec((B, D), lambda j: (0, j)),
        out_shape=jax.ShapeDtypeStruct((B, n * D), F32),
        compiler_params=_params("parallel"),
        name="adaln",
    )(c, w, b.reshape(1, -1))


ROPE_GROUPS = ((OFF_RQ, OFF_RK, 1.0), (OFF_RK, OFF_RV, RET_DK ** -0.5),
               (OFF_DQ, OFF_DK, DIFF_DK ** -0.5 * math.log2(math.e)), (OFF_DK, OFF_DV, 1.0))


def _rope_scale(col):
    for lo, hi, scale in ROPE_GROUPS:
        if lo <= col < hi:
            return scale
    return None


def _inproj_kernel(x0_ref, pos0_ref, mod0_ref, xn_ref, posn_ref, modn_ref, nw_ref, inv_ref,
                   w_ref, o_ref, h_ref, cos_ref, sin_ref, *, tn):
    i = pl.program_id(0)

    def prepare(x, pos, mod, slot):
        h = _rms(x) * nw_ref[...] * (1.0 + mod[1:2]) + mod[0:1]
        h_ref[slot] = h.astype(BF16)
        pos_t = jnp.broadcast_to(pos.astype(F32), (LANES, pos.shape[1])).T
        half = pos_t.shape[0] // 2
        low = lax.broadcasted_iota(jnp.int32, (half, LANES), 1) < LANES // 2
        ang = jnp.where(low, pos_t[:half], pos_t[half:]) * inv_ref[...]
        cos = jnp.cos(ang)
        sin = jnp.sin(ang)
        cos_r = pltpu.roll(cos, LANES // 2, 1)
        sin_r = pltpu.roll(sin, LANES // 2, 1)
        cos_ref[slot, :half] = jnp.where(low, cos, cos_r)
        cos_ref[slot, half:] = jnp.where(low, cos_r, cos)
        sin_ref[slot, :half] = jnp.where(low, -sin, sin_r)
        sin_ref[slot, half:] = jnp.where(low, -sin_r, sin)

    @pl.when(i == 0)
    def _():
        prepare(x0_ref[...], pos0_ref[0], mod0_ref[0], 0)

    cur = i % 2
    for g in range(o_ref.shape[1] // tn):
        acc = _dot(h_ref[cur], w_ref[:, g * tn:(g + 1) * tn])
        for k in range(tn // LANES):
            first = g * tn + k * LANES
            a = acc[:, k * LANES:(k + 1) * LANES]
            scale = _rope_scale(first)
            if scale is not None:
                a = a * cos_ref[cur] + pltpu.roll(a, LANES // 2, 1) * sin_ref[cur]
                if scale != 1.0:
                    a = a * scale
            o_ref[:, first:first + LANES] = a.astype(BF16)

    prepare(xn_ref[...], posn_ref[0], modn_ref[0], 1 - cur)


def _inproj(xf, pos, mod, norm_w, w_bf16, seq):
    T, D = xf.shape
    tm = min(TM_INPROJ, seq)
    tn = TN_INPROJ
    inv = ROPE_THETA ** (-jnp.arange(0, LANES, 2, dtype=F32) / LANES)
    inv = jnp.concatenate([inv, inv]).reshape(1, LANES)
    per_batch = seq // tm
    n_m = T // tm
    nxt = lambda i: jnp.minimum(i + 1, n_m - 1)
    once = dict(pipeline_mode=pl.Buffered(1))
    return pl.pallas_call(
        functools.partial(_inproj_kernel, tn=tn),
        grid=(n_m,),
        in_specs=[pl.BlockSpec((tm, D), lambda i: (0, 0), **once),
                  pl.BlockSpec((1, 1, tm), lambda i: (0, 0, 0), **once),
                  pl.BlockSpec((1, 6, D), lambda i: (0, 0, 0), **once),
                  pl.BlockSpec((tm, D), lambda i: (nxt(i), 0)),
                  pl.BlockSpec((1, 1, tm), lambda i: (nxt(i), 0, 0)),
                  pl.BlockSpec((1, 6, D), lambda i: (nxt(i) // per_batch, 0, 0)),
                  pl.BlockSpec((1, D), lambda i: (0, 0), **once),
                  pl.BlockSpec((1, LANES), lambda i: (0, 0), **once),
                  pl.BlockSpec((D, D_IN), lambda i: (0, 0), **once)],
        out_specs=pl.BlockSpec((tm, D_IN), lambda i: (i, 0)),
        out_shape=jax.ShapeDtypeStruct((T, D_IN), BF16),
        scratch_shapes=[pltpu.VMEM((2, tm, D), BF16),
                        pltpu.VMEM((2, tm, LANES), F32),
                        pltpu.VMEM((2, tm, LANES), F32)],
        compiler_params=pltpu.CompilerParams(dimension_semantics=("arbitrary",),
                                             vmem_limit_bytes=VMEM_LIMIT_INPROJ_BYTES),
        name="inproj",
    )(xf, pos, mod, xf, pos, mod, norm_w.reshape(1, D), inv, w_bf16)


def _ret_kernel(q_ref, k_ref, v_ref, g_ref, dmask_ref, xi_ref, zeta_ref, cd_ref, o_ref, state_ref):
    @pl.when(pl.program_id(1) == 0)
    def _():
        state_ref[...] = jnp.zeros(state_ref.shape, F32)

    C = RET_CHUNK
    for h in range(RET_HEADS):
        qk = slice(h * RET_DK, (h + 1) * RET_DK)
        vv = slice(h * RET_DV, (h + 1) * RET_DV)
        st = state_ref[h]
        for n in range(q_ref.shape[0] // C):
            rows = slice(n * C, (n + 1) * C)
            q = q_ref[rows, qk]
            k = k_ref[rows, qk]
            v = v_ref[rows, vv]
            s = _dot_nt(q, k) * dmask_ref[h]
            qx = (q.astype(F32) * xi_ref[h]).astype(BF16)
            o = _dot(s.astype(BF16), v) + _dot(qx, st.astype(BF16))
            kz_t = (k.astype(F32) * zeta_ref[h]).T.astype(BF16)
            st = cd_ref[h] * st + _dot(kz_t, v)
            g = g_ref[rows, vv].astype(F32)
            o_ref[rows, vv] = (_rms(o) * (g * _sigmoid(g))).astype(BF16)
        state_ref[h] = st


def _retention(proj, batch, seq):
    T = proj.shape[0]
    C = RET_CHUNK
    H = RET_HEADS
    nc = seq // C
    gamma = 1.0 - jnp.exp2(-5.0 - jnp.arange(H, dtype=F32))
    log_g = jnp.log(gamma)
    idx = jnp.arange(C, dtype=F32)
    rel = idx[:, None] - idx[None, :]
    dmask = jnp.where(rel >= 0, jnp.exp(log_g[:, None, None] * jnp.maximum(rel, 0.0)), 0.0)
    zeta = jnp.exp(log_g[:, None] * (C - 1 - idx))
    xi = jnp.exp(log_g[:, None] * (idx + 1))
    cd = jnp.exp(log_g * C)
    zeta_b = jnp.broadcast_to(zeta[:, :, None], (H, C, RET_DK))
    xi_b = jnp.broadcast_to(xi[:, :, None], (H, C, RET_DK))
    cd_b = jnp.broadcast_to(cd[:, None, None], (H, 1, RET_DV))
    wq = H * RET_DK
    wv = H * RET_DV
    tr = min(TR_RETENTION, seq)
    ns = seq // tr
    row = lambda b, n: b * ns + n
    const3 = lambda b, n: (0, 0, 0)
    return pl.pallas_call(
        _ret_kernel,
        grid=(batch, ns),
        in_specs=[pl.BlockSpec((tr, wq), lambda b, n: (row(b, n), OFF_RQ // wq)),
                  pl.BlockSpec((tr, wq), lambda b, n: (row(b, n), OFF_RK // wq)),
                  pl.BlockSpec((tr, wv), lambda b, n: (row(b, n), OFF_RV // wv)),
                  pl.BlockSpec((tr, wv), lambda b, n: (row(b, n), OFF_RG // wv)),
                  pl.BlockSpec((H, C, C), const3),
                  pl.BlockSpec((H, C, RET_DK), const3),
                  pl.BlockSpec((H, C, RET_DK), const3),
                  pl.BlockSpec((H, 1, RET_DV), const3)],
        out_specs=pl.BlockSpec((tr, wv), lambda b, n: (row(b, n), 0)),
        out_shape=jax.ShapeDtypeStruct((T, wv), BF16),
        scratch_shapes=[pltpu.VMEM((H, RET_DK, RET_DV), F32)],
        compiler_params=_params("parallel", "arbitrary"),
        name="retention",
    )(proj, proj, proj, proj, dmask, xi_b, zeta_b, cd_b)


def _transpose_bf16(a):
    return a.astype(F32).T.astype(BF16)


def _attn_kernel(lam_ref, dn_ref, *refs, bq, bk, heads, lambda_init):
    in_refs, (o_ref, vt_ref, qt_ref, s_ref, pm_ref, m_ref, l_ref, acc_ref) = refs[:5 * heads], refs[5 * heads:]
    q_refs = [in_refs[5 * h + c] for h in range(heads) for c in range(2)]
    k_refs = [in_refs[5 * h + 2 + c] for h in range(heads) for c in range(2)]
    v_refs = [in_refs[5 * h + 4] for h in range(heads)]
    streams = range(2 * heads)
    n_sub = vt_ref.shape[1]
    dv = acc_ref.shape[2]

    for h in range(heads):
        for t in range(n_sub):
            vt_ref[h, t] = _transpose_bf16(v_refs[h][t * bk:(t + 1) * bk, :])

    lam4 = lam_ref[...]
    lam = (jnp.exp(jnp.sum(lam4[0:1] * lam4[1:2], axis=1, keepdims=True))
           - jnp.exp(jnp.sum(lam4[2:3] * lam4[3:4], axis=1, keepdims=True)) + lambda_init)

    for i in range(n_sub // 2):
        par = i % 2
        rows = slice(i * bq, (i + 1) * bq)
        m_ref[par] = jnp.full(m_ref.shape[1:], NEG_BIG, F32)
        l_ref[par] = jnp.zeros(l_ref.shape[1:], F32)
        acc_ref[par] = jnp.zeros(acc_ref.shape[1:], F32)
        for n in streams:
            qt_ref[par, n] = _transpose_bf16(q_refs[n][rows, :])

        def score(t, slot, diagonal_offset=None, q_from=0, par=par):
            start = pl.multiple_of(t * bk, bk)
            for n in streams:
                st = _dot(k_refs[n][pl.ds(start, bk), :], qt_ref[par, n, :, q_from:])
                if diagonal_offset is not None:
                    key = lax.broadcasted_iota(jnp.int32, st.shape, 0) + diagonal_offset
                    qry = lax.broadcasted_iota(jnp.int32, st.shape, 1) + q_from
                    st = jnp.where(key <= qry, st, NEG_BIG)
                s_ref[slot, n, :, q_from:] = st
                pm_ref[slot, n, :, q_from:] = jnp.max(st, axis=0, keepdims=True)

        def accumulate(t, slot, q_from=0, par=par):
            for n in streams:
                m_prev = m_ref[par, n, :, q_from:]
                m_new = jnp.maximum(m_prev, pm_ref[slot, n, :, q_from:])
                p = jnp.exp2(s_ref[slot, n, :, q_from:] - m_new)
                alpha = jnp.exp2(m_prev - m_new)
                l_ref[par, n, :, q_from:] = alpha * l_ref[par, n, :, q_from:] + jnp.sum(p, axis=0, keepdims=True)
                acc_ref[par, n, :, q_from:] = (alpha * acc_ref[par, n, :, q_from:]
                                               + _dot(vt_ref[n // 2, t], p.astype(BF16)))
                m_ref[par, n, :, q_from:] = m_new

        def pair(jj, next_is_diagonal, score=score, accumulate=accumulate):
            t = 2 * jj
            score(t + 1, 1)
            accumulate(t, 0)
            score(t + 2, 0, 0 if next_is_diagonal else None)
            accumulate(t + 1, 1)

        score(0, 0, 0 if i == 0 else None)
        if i > 0:
            def body(jj, carry, pair=pair):
                pair(jj, False)
                return carry

            lax.fori_loop(0, i - 1, body, 0)
            pair(i - 1, True)
        score(2 * i + 1, 1, bk, q_from=bk)
        accumulate(2 * i, 0)
        accumulate(2 * i + 1, 1, q_from=bk)

        for h in range(heads):
            a, b = 2 * h, 2 * h + 1
            ot = (acc_ref[par, a] * (1.0 / l_ref[par, a])
                  - lam * (acc_ref[par, b] * (1.0 / l_ref[par, b])))
            ot = ot * lax.rsqrt(jnp.mean(ot * ot, axis=0, keepdims=True) + EPS)
            o_ref[rows, h * dv:(h + 1) * dv] = (ot.T * dn_ref[...] * (1.0 - lambda_init)).astype(BF16)


def _diff_attention(proj, lam4, diff_norm, lambda_init, batch, seq):
    T = proj.shape[0]
    H = DIFF_HEADS
    hp = HEADS_PER_ATTN_STEP
    bq = min(BQ_ATTN, seq)
    bk = bq // 2
    nq = seq // bq
    q0 = OFF_DQ // DIFF_DK
    k0 = OFF_DK // DIFF_DK
    v0 = OFF_DV // DIFF_DV
    head_specs = []
    for j in range(hp):
        head = lambda g, j=j: hp * g + j
        head_specs += [
            pl.BlockSpec((seq, DIFF_DK), lambda b, g, head=head: (b, q0 + 2 * head(g))),
            pl.BlockSpec((seq, DIFF_DK), lambda b, g, head=head: (b, q0 + 2 * head(g) + 1)),
            pl.BlockSpec((seq, DIFF_DK), lambda b, g, head=head: (b, k0 + 2 * head(g))),
            pl.BlockSpec((seq, DIFF_DK), lambda b, g, head=head: (b, k0 + 2 * head(g) + 1)),
            pl.BlockSpec((seq, DIFF_DV), lambda b, g, head=head: (b, v0 + head(g)))]
    ns = 2 * hp
    return pl.pallas_call(
        functools.partial(_attn_kernel, bq=bq, bk=bk, heads=hp, lambda_init=lambda_init),
        grid=(batch, H // hp),
        in_specs=[pl.BlockSpec((4, DIFF_DK), lambda b, g: (0, 0)),
                  pl.BlockSpec((1, DIFF_DV), lambda b, g: (0, 0))] + head_specs,
        out_specs=pl.BlockSpec((seq, hp * DIFF_DV), lambda b, g: (b, g)),
        out_shape=jax.ShapeDtypeStruct((T, H * DIFF_DV), BF16),
        scratch_shapes=[pltpu.VMEM((hp, seq // bk, DIFF_DV, bk), BF16),
                        pltpu.VMEM((2, ns, DIFF_DK, bq), BF16),
                        pltpu.VMEM((2, ns, bk, bq), F32),
                        pltpu.VMEM((2, ns, 1, bq), F32),
                        pltpu.VMEM((2, ns, 1, bq), F32),
                        pltpu.VMEM((2, ns, 1, bq), F32),
                        pltpu.VMEM((2, ns, DIFF_DV, bq), F32)],
        compiler_params=pltpu.CompilerParams(dimension_semantics=("parallel", "parallel"),
                                             vmem_limit_bytes=VMEM_LIMIT_ATTN_BYTES),
        name="diff_attn",
    )(lam4, diff_norm.reshape(1, DIFF_DV), *([proj] * (5 * hp)))


def _post_kernel(ro_ref, do_ref, gr_ref, gd_ref, x_ref, mod_ref, modp_ref, wr_ref, wd_ref, wo_ref, nf_ref,
                 wrt_ref, brt_ref, x1_ref, h2_ref, route_ref, rt_ref, cnt_ref, carry_ref, x1s_ref, lower_ref,
                 *, tm):
    i = pl.program_id(0)

    @pl.when(i == 0)
    def _():
        carry_ref[...] = jnp.zeros(carry_ref.shape, F32)
        x1s_ref[...] = jnp.zeros(x1s_ref.shape, F32)
        r_i = lax.broadcasted_iota(jnp.int32, (tm, tm), 0)
        c_i = lax.broadcasted_iota(jnp.int32, (tm, tm), 1)
        lower_ref[...] = jnp.where(c_i < r_i, 1.0, 0.0).astype(BF16)

    modp = modp_ref[0]
    h2 = _rms(x1s_ref[...]) * nf_ref[...] * (1.0 + modp[4:5]) + modp[3:4]
    h2_ref[...] = _pack_bf16_pairs(h2)
    lg = _dot(h2.astype(BF16), wrt_ref[...]) + brt_ref[...]

    mod = mod_ref[0]
    ret_out = _dot(ro_ref[...], wr_ref[...])
    diff_out = _dot(do_ref[...], wd_ref[...])
    merged = (_sigmoid(gr_ref[...].astype(F32)) * ret_out
              + _sigmoid(gd_ref[...].astype(F32)) * diff_out)
    x1 = x_ref[...] + mod[2:3] * _dot(merged.astype(BF16), wo_ref[...])
    x1_ref[...] = x1

    lane = lax.broadcasted_iota(jnp.int32, lg.shape, 1)
    far = jnp.int32(LANES)

    def top1(vals):
        best = jnp.max(vals, axis=1, keepdims=True)
        return best, jnp.min(jnp.where(vals == best, lane, far), axis=1, keepdims=True)

    is_group = (lane >= N_EXPERTS) & (lane < N_EXPERTS + N_GROUPS)
    g_best, g_lane = top1(jnp.where(is_group, lg, NEG_BIG))
    g_sum = jnp.sum(jnp.where(is_group, jnp.exp(lg - g_best), 0.0), axis=1, keepdims=True)
    g_top = 1.0 / g_sum
    first = (g_lane - N_EXPERTS) * EXPERTS_PER_GROUP
    in_group = (lane >= first) & (lane < first + EXPERTS_PER_GROUP)
    el = jnp.where(in_group, lg, NEG_BIG)
    e_a, i_a = top1(el)
    e_b, i_b = top1(jnp.where(lane == i_a, NEG_BIG, el))
    t = jnp.exp(e_b - e_a)
    w_a = g_top / (1.0 + t)
    w_b = g_top * t / (1.0 + t)

    hot_a = lane == i_a
    hot_b = lane == i_b
    hot = jnp.where((hot_a | hot_b) & (i > 0), 1.0, 0.0)
    before = _dot(lower_ref[...], hot.astype(BF16)) + carry_ref[...]
    rank_a = jnp.sum(jnp.where(hot_a, before, 0.0), axis=1, keepdims=True)
    rank_b = jnp.sum(jnp.where(hot_b, before, 0.0), axis=1, keepdims=True)
    total = carry_ref[...] + jnp.sum(hot, axis=0, keepdims=True)
    carry_ref[...] = total
    cnt_ref[...] = total

    fields = (i_a.astype(F32), i_b.astype(F32), w_a, w_b, rank_a, rank_b)
    route = jnp.zeros(lg.shape, F32)
    for n, f in enumerate(fields):
        route = jnp.where(lane == n, f, route)
    route_ref[...] = route
    rt_ref[...] = route.T[0:ROUTE_FIELDS, :]
    x1s_ref[...] = x1


def _post_mixer(ro, do, proj, xf, mod, w_ret_o, w_diff_o, w_out, norm_ffn, w_rt, b_rt, seq):
    T, D = xf.shape
    tm = min(TM_POST, seq)
    per_batch = seq // tm
    n = T // tm
    cur = lambda i: jnp.minimum(i, n - 1)
    prev = lambda i: jnp.maximum(i - 1, 0)
    row = lambda i: (cur(i), 0)
    prow = lambda i: (prev(i), 0)
    const = lambda i: (0, 0)
    return pl.pallas_call(
        functools.partial(_post_kernel, tm=tm),
        grid=(n + 1,),
        in_specs=[pl.BlockSpec((tm, D), row),
                  pl.BlockSpec((tm, D), row),
                  pl.BlockSpec((tm, D), lambda i: (cur(i), OFF_GR // D)),
                  pl.BlockSpec((tm, D), lambda i: (cur(i), OFF_GD // D)),
                  pl.BlockSpec((tm, D), row),
                  pl.BlockSpec((1, 6, D), lambda i: (cur(i) // per_batch, 0, 0)),
                  pl.BlockSpec((1, 6, D), lambda i: (prev(i) // per_batch, 0, 0)),
                  pl.BlockSpec((D, D), const),
                  pl.BlockSpec((D, D), const),
                  pl.BlockSpec((D, D), const),
                  pl.BlockSpec((1, D), const),
                  pl.BlockSpec((D, LANES), const),
                  pl.BlockSpec((1, LANES), const)],
        out_specs=[pl.BlockSpec((tm, D), row),
                   pl.BlockSpec((tm, D // 2), prow),
                   pl.BlockSpec((tm, LANES), prow),
                   pl.BlockSpec((ROUTE_FIELDS, tm), lambda i: (0, prev(i))),
                   pl.BlockSpec((1, LANES), const)],
        out_shape=[jax.ShapeDtypeStruct((T, D), F32),
                   jax.ShapeDtypeStruct((T, D // 2), jnp.uint32),
                   jax.ShapeDtypeStruct((T, LANES), F32),
                   jax.ShapeDtypeStruct((ROUTE_FIELDS, T), F32),
                   jax.ShapeDtypeStruct((1, LANES), F32)],
        scratch_shapes=[pltpu.VMEM((1, LANES), F32), pltpu.VMEM((tm, D), F32), pltpu.VMEM((tm, tm), BF16)],
        compiler_params=_params("arbitrary"),
        name="post_mixer",
    )(ro, do, proj, proj, xf, mod, mod, w_ret_o, w_diff_o, w_out, norm_ffn.reshape(1, D), w_rt, b_rt)


def _row_copy(src, s, dst, d, sem):
    return pltpu.make_async_copy(src.at[pl.ds(s, 1), :], dst.at[pl.ds(d, 1), :], sem)


def _dest_blocks(dest, tile):
    n = dest.shape[1] // tile
    return dest.reshape(2, n, tile).transpose(1, 0, 2).reshape(n, 1, 2 * tile)


def _scatter_kernel(zb_ref, dest_ref, h_ref, o_hbm, zero_ref, zsem, sem, *, ts, tb):
    i = pl.program_id(0)

    @pl.when(i == 0)
    def _():
        zero_ref[...] = jnp.zeros(zero_ref.shape, zero_ref.dtype)

        def zero_copy(n):
            start = pl.multiple_of(zb_ref[n] * tb, tb)
            return pltpu.make_async_copy(zero_ref, o_hbm.at[pl.ds(start, tb), :], zsem)

        for n in range(zb_ref.shape[0]):
            pl.when(zb_ref[n] >= 0)(lambda n=n: zero_copy(n).start())
        for n in range(zb_ref.shape[0]):
            pl.when(zb_ref[n] >= 0)(lambda n=n: zero_copy(n).wait())

    def issue(r, carry):
        for k in range(2):
            _row_copy(h_ref, r, o_hbm, dest_ref[0, 0, k * ts + r], sem).start(priority=k)
        return carry

    lax.fori_loop(0, ts, issue, 0, unroll=True)
    pltpu.make_async_copy(o_hbm.at[pl.ds(0, 2 * ts), :], o_hbm.at[pl.ds(0, 2 * ts), :], sem).wait()


def _moe_scatter(h2, dest, zero_blocks, n_rows):
    T, D = h2.shape
    ts = min(TS_SCATTER, T)
    tb = TB_MOE
    grid_spec = pltpu.PrefetchScalarGridSpec(
        num_scalar_prefetch=1,
        grid=(T // ts,),
        in_specs=[pl.BlockSpec((1, 1, 2 * ts), lambda i, zb: (i, 0, 0), memory_space=pltpu.SMEM),
                  pl.BlockSpec((ts, D), lambda i, zb: (i, 0))],
        out_specs=pl.BlockSpec(memory_space=pl.ANY),
        scratch_shapes=[pltpu.VMEM((tb, D), h2.dtype), pltpu.SemaphoreType.DMA(()), pltpu.SemaphoreType.DMA(())],
    )
    return pl.pallas_call(
        functools.partial(_scatter_kernel, ts=ts, tb=tb),
        grid_spec=grid_spec,
        out_shape=jax.ShapeDtypeStruct((n_rows, D), h2.dtype),
        compiler_params=_params("arbitrary"),
        name="moe_scatter",
    )(zero_blocks, _dest_blocks(dest, ts), h2)


def _expert_kernel(be_ref, nv_ref, x_ref, w1_ref, w3_ref, w2_ref, o_ref, w1b_ref, w3b_ref, w2b_ref):
    n = pl.program_id(0)
    nv = nv_ref[n]
    new_expert = (n == 0) | (be_ref[n] != be_ref[jnp.maximum(n - 1, 0)])

    @pl.when(new_expert & (nv > 0))
    def _():
        w1b_ref[...] = w1_ref[0].astype(BF16)
        w3b_ref[...] = w3_ref[0].astype(BF16)
        w2b_ref[...] = w2_ref[0].astype(BF16)

    @pl.when(nv > 0)
    def _():
        x_lo, x_hi = _unpack_bf16_pairs(x_ref[...])
        half = x_lo.shape[1]
        g = _dot(x_lo, w1b_ref[:half, :]) + _dot(x_hi, w1b_ref[half:, :])
        u = _dot(x_lo, w3b_ref[:half, :]) + _dot(x_hi, w3b_ref[half:, :])
        a = (g * _sigmoid(g) * u).astype(BF16)
        o_ref[...] = _pack_bf16_pairs(_dot(a, w2b_ref[...]))

    @pl.when(nv == 0)
    def _():
        o_ref[...] = jnp.zeros(o_ref.shape, o_ref.dtype)


def _moe_experts(h_pad, block_e, block_nv, w1, w3, w2):
    P = h_pad.shape[0]
    D = 2 * h_pad.shape[1]
    tb = TB_MOE
    wmap = lambda n, be, nv: (be[n], 0, 0)
    grid_spec = pltpu.PrefetchScalarGridSpec(
        num_scalar_prefetch=2,
        grid=(P // tb,),
        in_specs=[pl.BlockSpec((tb, D // 2), lambda n, be, nv: (n, 0)),
                  pl.BlockSpec((1, D, D_EXPERT), wmap),
                  pl.BlockSpec((1, D, D_EXPERT), wmap),
                  pl.BlockSpec((1, D_EXPERT, D), wmap)],
        out_specs=pl.BlockSpec((tb, D // 2), lambda n, be, nv: (n, 0)),
        scratch_shapes=[pltpu.VMEM((D, D_EXPERT), BF16), pltpu.VMEM((D, D_EXPERT), BF16),
                        pltpu.VMEM((D_EXPERT, D), BF16)],
    )
    return pl.pallas_call(
        _expert_kernel,
        grid_spec=grid_spec,
        out_shape=jax.ShapeDtypeStruct((P, D // 2), h_pad.dtype),
        compiler_params=_params("arbitrary"),
        name="moe_experts",
    )(block_e, block_nv, h_pad, w1, w3, w2)


def _combine_kernel(dcur_ref, dnext_ref, x1_ref, route_ref, mod_ref, nw_ref, y_hbm, o_ref, buf, sem, *, tc, final):
    i = pl.program_id(0)
    slot = i % 2

    def gather(d_ref, s, rows):
        for r in rows:
            for k in range(2):
                _row_copy(y_hbm, d_ref[0, 0, k * tc + r], buf.at[s], k * tc + r, sem.at[s]).start(priority=k)

    def wait_slot(s):
        pltpu.make_async_copy(y_hbm.at[pl.ds(0, 2 * tc), :], buf.at[s], sem.at[s]).wait()

    def unpacked(first, n):
        lo, hi = _unpack_bf16_pairs(buf[slot, first:first + n, :])
        return jnp.concatenate([lo, hi], axis=1).astype(F32)

    pl.when(i == 0)(lambda: gather(dcur_ref, 0, range(tc)))
    wait_slot(slot)
    chunk = tc // COMBINE_CHUNKS
    for c in range(COMBINE_CHUNKS):
        rows = slice(c * chunk, (c + 1) * chunk)
        route = route_ref[rows, :]
        y = route[:, 2:3] * unpacked(c * chunk, chunk) + route[:, 3:4] * unpacked(tc + c * chunk, chunk)
        x2 = x1_ref[rows, :] + mod_ref[0][5:6] * y
        o_ref[rows, :] = _rms(x2) * nw_ref[...] if final else x2
        gather(dnext_ref, 1 - slot, range(c * chunk, (c + 1) * chunk))
    pl.when(i == pl.num_programs(0) - 1)(lambda: wait_slot(1 - slot))


def _moe_combine(y_pad, dest, x1, route, mod, norm_w, seq, final):
    T, D = x1.shape
    tc = min(TC_COMBINE, seq)
    per_batch = seq // tc
    row = lambda i: (i, 0)
    n = T // tc
    dest3 = _dest_blocks(dest, tc)
    return pl.pallas_call(
        functools.partial(_combine_kernel, tc=tc, final=final),
        grid=(n,),
        in_specs=[pl.BlockSpec((1, 1, 2 * tc), lambda i: (i, 0, 0), memory_space=pltpu.SMEM),
                  pl.BlockSpec((1, 1, 2 * tc), lambda i: (jnp.minimum(i + 1, n - 1), 0, 0),
                               memory_space=pltpu.SMEM),
                  pl.BlockSpec((tc, D), row),
                  pl.BlockSpec((tc, LANES), row),
                  pl.BlockSpec((1, 6, D), lambda i: (i // per_batch, 0, 0)),
                  pl.BlockSpec((1, D), lambda i: (0, 0)),
                  pl.BlockSpec(memory_space=pl.ANY)],
        out_specs=pl.BlockSpec((tc, D), row),
        out_shape=jax.ShapeDtypeStruct((T, D), F32),
        scratch_shapes=[pltpu.VMEM((2, 2 * tc, D // 2), y_pad.dtype), pltpu.SemaphoreType.DMA((2,))],
        compiler_params=_params("arbitrary"),
        name="moe_combine",
    )(dest3, dest3, x1, route, mod, norm_w.reshape(1, D), y_pad)


def _dest_kernel(start_ref, rt_ref, o_ref):
    expert = rt_ref[0:2, :].astype(jnp.int32)
    rank = rt_ref[4:6, :].astype(jnp.int32)
    base = jnp.zeros(expert.shape, jnp.int32)
    for e in range(N_EXPERTS):
        base = jnp.where(expert == e, start_ref[e], base)
    o_ref[...] = base + rank


def _dest_slots(rt, region_start):
    T = rt.shape[1]
    grid_spec = pltpu.PrefetchScalarGridSpec(
        num_scalar_prefetch=1,
        grid=(1,),
        in_specs=[pl.BlockSpec((ROUTE_FIELDS, T), lambda i, s: (0, 0))],
        out_specs=pl.BlockSpec((2, T), lambda i, s: (0, 0)),
    )
    return pl.pallas_call(
        _dest_kernel,
        grid_spec=grid_spec,
        out_shape=jax.ShapeDtypeStruct((2, T), jnp.int32),
        compiler_params=_params("arbitrary"),
        name="moe_dest",
    )(region_start, rt)


def _routing_tables(rt, counts, n_blocks):
    tb = TB_MOE
    cnt = counts[0, :N_EXPERTS].astype(jnp.int32)
    nblk = (cnt + tb - 1) // tb
    blk_end = jnp.cumsum(nblk)
    blk_start = blk_end - nblk
    dest = _dest_slots(rt, (blk_start * tb).astype(jnp.int32))
    blocks = jnp.arange(n_blocks, dtype=jnp.int32)
    block_e = jnp.minimum(jnp.sum(blocks[:, None] >= blk_end[None, :], axis=1), N_EXPERTS - 1).astype(jnp.int32)
    mine = block_e[:, None] == jnp.arange(N_EXPERTS, dtype=jnp.int32)[None, :]
    left = jnp.sum(jnp.where(mine, cnt[None, :] - (blocks[:, None] - blk_start[None, :]) * tb, 0), axis=1)
    block_nv = jnp.where(blocks < blk_end[-1], jnp.clip(left, 0, tb), 0).astype(jnp.int32)
    tail = blk_end[-1] + jnp.arange(N_EXPERTS, dtype=jnp.int32)
    zero_blocks = jnp.concatenate([jnp.where(nblk > 0, blk_end - 1, -1),
                                   jnp.where(tail < n_blocks, tail, -1)]).astype(jnp.int32)
    return dest.astype(jnp.int32), block_e, block_nv, zero_blocks


def kernel(x, c, positions, w_ada, b_ada, norm_mix, w_in, w_ret_o, w_diff_o, lam_q1, lam_k1, lam_q2, lam_k2,
           diff_norm, w_out, norm_ffn, w_router_group, b_router_group, w_router_expert, b_router_expert,
           w_exp_gate, w_exp_up, w_exp_down, norm_final):
    B, S, D = x.shape
    T = B * S
    depth = w_ada.shape[0]
    assert D_IN == w_in.shape[2] and S % RET_CHUNK == 0
    xf = x.reshape(T, D)
    tm_in = min(TM_INPROJ, S)
    pos = positions.reshape(T // tm_in, 1, tm_in)
    n_blocks = (2 * T) // TB_MOE + N_EXPERTS
    for l in range(depth):
        lambda_init = 0.8 - 0.6 * math.exp(-0.3 * l)
        mod = _adaln(c, w_ada[l], b_ada[l]).reshape(B, 6, D)
        proj = _inproj(xf, pos, mod, norm_mix[l], w_in[l].astype(BF16), S)
        ro = _retention(proj, B, S)
        lam4 = jnp.stack([lam_q1[l], lam_k1[l], lam_q2[l], lam_k2[l]]).astype(F32)
        do = _diff_attention(proj, lam4, diff_norm[l].astype(F32), lambda_init, B, S)

        pad = LANES - N_EXPERTS - N_GROUPS
        w_rt = jnp.concatenate([w_router_expert[l], w_router_group[l], jnp.zeros((D, pad), F32)], axis=1)
        b_rt = jnp.concatenate([b_router_expert[l], b_router_group[l], jnp.zeros((pad,), F32)]).reshape(1, LANES)
        x1, h2, route, rt, counts = _post_mixer(
            ro, do, proj, xf, mod, w_ret_o[l].astype(BF16), w_diff_o[l].astype(BF16), w_out[l].astype(BF16),
            norm_ffn[l], w_rt.astype(BF16), b_rt, S)

        dest, block_e, block_nv, zero_blocks = _routing_tables(rt, counts, n_blocks)
        h_pad = _moe_scatter(h2, dest, zero_blocks, n_blocks * TB_MOE)
        y_pad = _moe_experts(h_pad, block_e, block_nv, w_exp_gate[l], w_exp_up[l], w_exp_down[l])
        xf = _moe_combine(y_pad, dest, x1, route, mod, norm_final, S, l == depth - 1)
    return xf.reshape(B, S, D)
```
